```python
import math
import jax, jax.numpy as jnp
from jax import lax
import numpy as np

D_MODEL = 2048
BATCH = 4
SEQ = 2048
DEPTH = 2
DEC_BATCH = 32
DEC_SEQ = 32
PAST_LEN = 1024

CHUNK = 64
N_MIXERS = 2
N_SSD_LAYERS = (DEPTH + 1) // 2
N_HGRN_LAYERS = DEPTH // 2
EPS = 1e-6

SSD_EXPAND = 2
SSD_D_INNER = SSD_EXPAND * D_MODEL
SSD_HEADDIM = 64
SSD_HEADS = SSD_D_INNER // SSD_HEADDIM
SSD_GROUPS = 8
SSD_HPG = SSD_HEADS // SSD_GROUPS
SSD_STATE = 128
SSD_CONV = 4
SSD_CONV_DIM = SSD_D_INNER + 2 * SSD_GROUPS * SSD_STATE
SSD_IN_DIM = SSD_D_INNER + SSD_CONV_DIM + SSD_HEADS

HGRN_EXPAND = 128
HGRN_HEADS = D_MODEL // HGRN_EXPAND
HGRN_DK = HGRN_EXPAND
HGRN_DV = D_MODEL // HGRN_HEADS
HGRN_F = HGRN_HEADS * HGRN_DK
HGRN_IN_DIM = 2 * HGRN_F + 2 * D_MODEL

FFN_DIM = 5632
N_EXPERTS = 8
TOP_K = 2
EXPERT_DIM = 7168
N_DENSE_LAYERS = (DEPTH + 1) // 2
N_MOE_LAYERS = DEPTH // 2

kernel_name = "hybrid_ssd_hgrn2_adaln_stream_step"


def rmsnorm(x, w):
    xf = x.astype(jnp.float32)
    y = xf * lax.rsqrt(jnp.mean(xf * xf, axis=-1, keepdims=True) + EPS)
    return y.astype(x.dtype) * w


def chunk_len(L):
    return L if L <= CHUNK else math.gcd(L, CHUNK)


def to_chunks(t, c):
    b, L = t.shape[:2]
    return jnp.moveaxis(t.reshape((b, L // c, c) + t.shape[2:]), 1, 0)


def from_chunks(t):
    t = jnp.moveaxis(t, 0, 1)
    return t.reshape((t.shape[0], t.shape[1] * t.shape[2]) + t.shape[3:])


def causal_conv(u, buf, w, b):
    L = u.shape[1]
    full = jnp.concatenate([buf.astype(u.dtype), u], axis=1)
    y = b
    for k in range(SSD_CONV):
        y = y + full[:, k:k + L] * w[k]
    return y, full[:, L:]


def ssd_chunked(x, dt, a, bm, cm, h0):
    bsz, L = x.shape[:2]
    c = chunk_len(L)
    mask = jnp.tril(jnp.ones((c, c), dtype=bool))[None, :, :, None, None]
    h_init = h0.astype(jnp.float32).reshape(bsz, SSD_GROUPS, SSD_HPG, SSD_HEADDIM, SSD_STATE)

    def step(h, inp):
        xc, dtc, bc, cc = inp
        b = jnp.cumsum(dtc * a, axis=1)
        seg = b[:, :, None] - b[:, None]
        lmat = jnp.exp(jnp.where(mask, seg, -jnp.inf))
        cb = jnp.einsum('btgn,bsgn->btsg', cc, bc).astype(jnp.float32)
        xdt = xc.astype(jnp.float32) * dtc[..., None]
        y_intra = jnp.einsum('btsge,bsgep->btgep', cb[..., None] * lmat, xdt)
        y_inter = jnp.einsum('btgn,bgepn->btgep', cc, h) * jnp.exp(b)[..., None]
        decay_last = jnp.exp(b[:, -1:] - b)
        h_new = h * jnp.exp(b[:, -1])[..., None, None] + jnp.einsum(
            'bsgn,bsgep->bgepn', bc, xdt * decay_last[..., None])
        return h_new, y_intra + y_inter

    h, ys = lax.scan(step, h_init, (to_chunks(x, c), to_chunks(dt, c), to_chunks(bm, c), to_chunks(cm, c)))
    return from_chunks(ys), h.reshape(h0.shape).astype(h0.dtype)


def gla_chunked(q, k, log_f, v, s0):
    L = q.shape[1]
    c = chunk_len(L)
    mask = jnp.tril(jnp.ones((c, c), dtype=bool))[None, :, :, None, None]

    def step(s, inp):
        qc, kc, lfc, vc = inp
        b = jnp.cumsum(lfc, axis=1)
        seg = b[:, :, None] - b[:, None]
        decay = jnp.exp(jnp.where(mask, seg, -jnp.inf))
        att = jnp.einsum('btshk,bshk->bhts', decay * qc[:, :, None], kc)
        o = jnp.einsum('bhts,bshv->bthv', att, vc) + jnp.einsum('bthk,bhkv->bthv', qc * jnp.exp(b), s)
        s_new = s * jnp.exp(b[:, -1])[..., None] + jnp.einsum(
            'bshk,bshv->bhkv', kc * jnp.exp(b[:, -1:] - b), vc)
        return s_new, o

    s, os_ = lax.scan(step, s0.astype(jnp.float32),
                      (to_chunks(q, c), to_chunks(k, c), to_chunks(log_f, c), to_chunks(v, c)))
    return from_chunks(os_), s.astype(s0.dtype)


def ssd_mixer(u, conv_buf, h0, w_in, conv_w, conv_b, dt_bias, a_log, d_skip, norm_w, w_out):
    bsz, L, _ = u.shape
    proj = u @ w_in
    z, xbc, dt_raw = jnp.split(proj, [SSD_D_INNER, SSD_D_INNER + SSD_CONV_DIM], axis=-1)
    xbc, new_buf = causal_conv(xbc, conv_buf, conv_w, conv_b)
    xbc = jax.nn.silu(xbc)
    xs, bm, cm = jnp.split(xbc, [SSD_D_INNER, SSD_D_INNER + SSD_GROUPS * SSD_STATE], axis=-1)
    xs = xs.reshape(bsz, L, SSD_GROUPS, SSD_HPG, SSD_HEADDIM)
    bm = bm.reshape(bsz, L, SSD_GROUPS, SSD_STATE)
    cm = cm.reshape(bsz, L, SSD_GROUPS, SSD_STATE)
    dt = jax.nn.softplus(dt_raw.astype(jnp.float32) + dt_bias.astype(jnp.float32))
    dt = dt.reshape(bsz, L, SSD_GROUPS, SSD_HPG)
    a = -jnp.exp(a_log.astype(jnp.float32)).reshape(SSD_GROUPS, SSD_HPG)
    y, h = ssd_chunked(xs, dt, a, bm, cm, h0)
    y = y + xs.astype(jnp.float32) * d_skip.astype(jnp.float32).reshape(SSD_GROUPS, SSD_HPG, 1)
    y = y.reshape(bsz, L, SSD_D_INNER) * jax.nn.silu(z.astype(jnp.float32))
    y = rmsnorm(y.reshape(bsz, L, SSD_GROUPS, SSD_D_INNER // SSD_GROUPS),
                norm_w.reshape(SSD_GROUPS, SSD_D_INNER // SSD_GROUPS)).reshape(bsz, L, SSD_D_INNER)
    return y.astype(u.dtype) @ w_out, new_buf, h


def hgrn_mixer(u, s0, w_in, lb, norm_w, w_out):
    bsz, L, _ = u.shape
    proj = u @ w_in
    q, fz, i, g = jnp.split(proj, [HGRN_F, 2 * HGRN_F, 2 * HGRN_F + D_MODEL], axis=-1)
    lb = lb.reshape(HGRN_HEADS, HGRN_DK)
    fz = fz.astype(jnp.float32).reshape(bsz, L, HGRN_HEADS, HGRN_DK)
    log_f = jnp.logaddexp(jnp.log(lb), jnp.log1p(-lb) + jax.nn.log_sigmoid(fz))
    k = (1.0 - lb) * jax.nn.sigmoid(-fz)
    q = q.astype(jnp.float32).reshape(bsz, L, HGRN_HEADS, HGRN_DK)
    v = i.astype(jnp.float32).reshape(bsz, L, HGRN_HEADS, HGRN_DV)
    o, s = gla_chunked(q, k, log_f, v, s0)
    o = rmsnorm(o, norm_w) * jax.nn.silu(g.astype(jnp.float32).reshape(bsz, L, HGRN_HEADS, HGRN_DV))
    return o.reshape(bsz, L, D_MODEL).astype(u.dtype) @ w_out, s


def swiglu(u, w_gu, w_down):
    gate, up = jnp.split(u @ w_gu, 2, axis=-1)
    return (jax.nn.silu(gate) * up) @ w_down


def moe_swiglu(u, router, w_gu, w_down):
    logits = (u @ router).astype(jnp.float32)
    top_v, top_i = lax.top_k(logits, TOP_K)
    gates = jax.nn.softmax(top_v, axis=-1)
    combine = jnp.sum(jax.nn.one_hot(top_i, N_EXPERTS, dtype=jnp.float32) * gates[..., None], axis=-2)
    combine = combine.astype(u.dtype)
    out = jnp.zeros_like(u)
    for e in range(N_EXPERTS):
        out = out + combine[..., e:e + 1] * swiglu(u, w_gu[e], w_down[e])
    return out


def adaln_modulation(c, w, b):
    mod = jax.nn.silu(c) @ w + b
    return jnp.split(mod[:, None, :], 6, axis=-1)


def trunk(x, c, conv_bufs, ssd_states, hgrn_states,
          ada_w, ada_b, norm_w,
          ssd_w_in, ssd_conv_w, ssd_conv_b, ssd_dt_bias, ssd_a_log, ssd_d, ssd_norm_w, ssd_w_out,
          hgrn_w_in, hgrn_lb_logits, hgrn_norm_w, hgrn_w_out,
          ffn_w_gu, ffn_w_down, moe_router, moe_w_gu, moe_w_down, final_norm_w):
    p = jax.nn.softmax(hgrn_lb_logits.astype(jnp.float32), axis=0)
    lb_all = jnp.cumsum(p, axis=0) - p[0]
    new_conv, new_ssd, new_hgrn = [], [], []
    for i in range(DEPTH):
        j = i // N_MIXERS
        sh1, sc1, g1, sh2, sc2, g2 = adaln_modulation(c, ada_w[i], ada_b[i])
        h = rmsnorm(x, norm_w[i, 0]) * (1.0 + sc1) + sh1
        if i % N_MIXERS == 0:
            m, cbuf, st = ssd_mixer(h, conv_bufs[j], ssd_states[j], ssd_w_in[j], ssd_conv_w[j], ssd_conv_b[j],
                                    ssd_dt_bias[j], ssd_a_log[j], ssd_d[j], ssd_norm_w[j], ssd_w_out[j])
            new_conv.append(cbuf)
            new_ssd.append(st)
        else:
            m, st = hgrn_mixer(h, hgrn_states[j], hgrn_w_in[j], lb_all[i], hgrn_norm_w[j], hgrn_w_out[j])
            new_hgrn.append(st)
        x = x + g1 * m
        h = rmsnorm(x, norm_w[i, 1]) * (1.0 + sc2) + sh2
        if i % 2 == 0:
            f = swiglu(h, ffn_w_gu[j], ffn_w_down[j])
        else:
            f = moe_swiglu(h, moe_router[j], moe_w_gu[j], moe_w_down[j])
        x = x + g2 * f
    return rmsnorm(x, final_norm_w), jnp.stack(new_conv), jnp.stack(new_ssd), jnp.stack(new_hgrn)


def setup_inputs(seed: int = 0) -> dict:
    key = jax.random.key(seed)
    ks = jax.random.split(key, 32)
    f32 = jnp.float32
    D = D_MODEL

    def nrm(k, shape, s):
        return jax.random.normal(k, shape, f32) * s

    def gain(k, shape):
        return 1.0 + 0.01 * jax.random.normal(k, shape, f32)

    dt0 = jnp.exp(jax.random.uniform(ks[14], (N_SSD_LAYERS, SSD_HEADS), f32, math.log(1e-3), math.log(1e-1)))
    return {
        "x_prompt": nrm(ks[0], (BATCH, SEQ, D), 1.0),
        "x_sample": nrm(ks[1], (DEC_BATCH, DEC_SEQ, D), 1.0),
        "c_prompt": nrm(ks[2], (BATCH, D), 1.0),
        "c_sample": nrm(ks[3], (DEC_BATCH, D), 1.0),
        "state_ssd_conv": nrm(ks[4], (N_SSD_LAYERS, DEC_BATCH, SSD_CONV - 1, SSD_CONV_DIM), 1.0),
        "state_ssd": nrm(ks[5], (N_SSD_LAYERS, DEC_BATCH, SSD_HEADS, SSD_HEADDIM, SSD_STATE), 0.5),
        "state_hgrn": nrm(ks[6], (N_HGRN_LAYERS, DEC_BATCH, HGRN_HEADS, HGRN_DK, HGRN_DV), 0.5),
        "ada_w": nrm(ks[7], (DEPTH, D, 6 * D), 0.5 * D ** -0.5),
        "ada_b": nrm(ks[8], (DEPTH, 6 * D), 0.02),
        "norm_w": gain(ks[9], (DEPTH, 2, D)),
        "ssd_w_in": nrm(ks[10], (N_SSD_LAYERS, D, SSD_IN_DIM), D ** -0.5),
        "ssd_conv_w": nrm(ks[11], (N_SSD_LAYERS, SSD_CONV, SSD_CONV_DIM), SSD_CONV ** -0.5),
        "ssd_conv_b": nrm(ks[12], (N_SSD_LAYERS, SSD_CONV_DIM), 0.02),
        "ssd_dt_bias": dt0 + jnp.log(-jnp.expm1(-dt0)),
        "ssd_a_log": jnp.log(jax.random.uniform(ks[13], (N_SSD_LAYERS, SSD_HEADS), f32, 1.0, 16.0)),
        "ssd_d": 1.0 + 0.1 * jax.random.normal(ks[15], (N_SSD_LAYERS, SSD_HEADS), f32),
        "ssd_norm_w": gain(ks[16], (N_SSD_LAYERS, SSD_D_INNER)),
        "ssd_w_out": nrm(ks[17], (N_SSD_LAYERS, SSD_D_INNER, D), SSD_D_INNER ** -0.5),
        "hgrn_w_in": nrm(ks[18], (N_HGRN_LAYERS, D, HGRN_IN_DIM), D ** -0.5),
        "hgrn_lb_logits": nrm(ks[19], (DEPTH, HGRN_F), 1.0),
        "hgrn_norm_w": gain(ks[20], (N_HGRN_LAYERS, HGRN_DV)),
        "hgrn_w_out": nrm(ks[21], (N_HGRN_LAYERS, D, D), D ** -0.5),
        "ffn_w_gu": nrm(ks[22], (N_DENSE_LAYERS, D, 2 * FFN_DIM), D ** -0.5),
        "ffn_w_down": nrm(ks[23], (N_DENSE_LAYERS, FFN_DIM, D), FFN_DIM ** -0.5),
        "moe_router": nrm(ks[24], (N_MOE_LAYERS, D, N_EXPERTS), D ** -0.5),
        "moe_w_gu": nrm(ks[25], (N_MOE_LAYERS, N_EXPERTS, D, 2 * EXPERT_DIM), D ** -0.5),
        "moe_w_down": nrm(ks[26], (N_MOE_LAYERS, N_EXPERTS, EXPERT_DIM, D), EXPERT_DIM ** -0.5),
        "final_norm_w": gain(ks[27], (D,)),
    }


def reference(x_prompt, x_sample, c_prompt, c_sample, state_ssd_conv, state_ssd, state_hgrn,
              ada_w, ada_b, norm_w,
              ssd_w_in, ssd_conv_w, ssd_conv_b, ssd_dt_bias, ssd_a_log, ssd_d, ssd_norm_w, ssd_w_out,
              hgrn_w_in, hgrn_lb_logits, hgrn_norm_w, hgrn_w_out,
              ffn_w_gu, ffn_w_down, moe_router, moe_w_gu, moe_w_down, final_norm_w):
    weights = (ada_w, ada_b, norm_w,
               ssd_w_in, ssd_conv_w, ssd_conv_b, ssd_dt_bias, ssd_a_log, ssd_d, ssd_norm_w, ssd_w_out,
               hgrn_w_in, hgrn_lb_logits, hgrn_norm_w, hgrn_w_out,
               ffn_w_gu, ffn_w_down, moe_router, moe_w_gu, moe_w_down, final_norm_w)
    nb = x_prompt.shape[0]
    dtp = x_prompt.dtype
    conv0 = jnp.zeros((N_SSD_LAYERS, nb, SSD_CONV - 1, SSD_CONV_DIM), dtp)
    ssd0 = jnp.zeros((N_SSD_LAYERS, nb, SSD_HEADS, SSD_HEADDIM, SSD_STATE), dtp)
    hgrn0 = jnp.zeros((N_HGRN_LAYERS, nb, HGRN_HEADS, HGRN_DK, HGRN_DV), dtp)
    y_prompt, p_conv, p_ssd, p_hgrn = trunk(x_prompt, c_prompt, conv0, ssd0, hgrn0, *weights)
    y_sample, s_conv, s_ssd, s_hgrn = trunk(x_sample, c_sample, state_ssd_conv, state_ssd, state_hgrn, *weights)
    return (y_prompt, y_sample, p_conv, p_ssd, p_hgrn, s_conv, s_ssd, s_hgrn)
```

```python
import functools

import jax
import jax.numpy as jnp
from jax import lax
from jax.experimental import pallas as pl
from jax.experimental.pallas import tpu as pltpu

F32 = jnp.float32
BF16 = jnp.bfloat16
I32 = jnp.int32

EPS = 1e-6
D = 2048
NP, LP = 4, 2048
NS, LS = 32, 32
MP, MS = NP * LP, NS * LS
M = MP + MS
MOD_S0 = 8
MOD_ROWS = MOD_S0 + NS

SSD_DI = 4096
SSD_HEADS = 64
SSD_P = 64
SSD_G = 8
SSD_GW = SSD_DI // SSD_G
SSD_N = 128
SSD_XBC = SSD_DI + 2 * SSD_G * SSD_N
SSD_K = 4

HG_H = 16
HG_DK = 128
HG_DV = 128
HG_BLK = 16

FFN = 5632
NE = 8
EH = 7168
MOE_TM = 256
MOE_TILES = (2 * M) // MOE_TM + NE
MOE_ROWS = MOE_TILES * MOE_TM

VMEM_LIMIT = 56 * 1024 * 1024

NT_DIMS = (((1,), (1,)), ((), ()))
TN_DIMS = (((0,), (0,)), ((), ()))


def _cparams(n_axes):
    return pltpu.CompilerParams(dimension_semantics=("arbitrary",) * n_axes,
                                vmem_limit_bytes=VMEM_LIMIT)


def _silu(x):
    return x * jax.nn.sigmoid(x)


def _softplus(x):
    return jnp.maximum(x, 0.0) + jnp.log1p(jnp.exp(-jnp.abs(x)))


def _split3(x):
    hi = x.astype(BF16)
    r1 = x - hi.astype(F32)
    mid = r1.astype(BF16)
    lo = (r1 - mid.astype(F32)).astype(BF16)
    return hi, mid, lo


def _dot_exact_rhs(mat_bf16, x):
    hi, mid, lo = _split3(x)
    acc = jnp.dot(mat_bf16, hi, preferred_element_type=F32)
    acc = acc + jnp.dot(mat_bf16, mid, preferred_element_type=F32)
    return acc + jnp.dot(mat_bf16, lo, preferred_element_type=F32)


def _grouped_apply(i, tm, mod_refs, fn):
    n_pt = MP // tm
    tpb = LP // tm

    @pl.when(i < n_pt)
    def _():
        r = i // tpb
        fn([m[pl.ds(r, 1), :][None] for m in mod_refs], 1)

    @pl.when(i >= n_pt)
    def _():
        g = tm // LS
        start = pl.multiple_of(MOD_S0 + (i - n_pt) * g, 8)
        fn([m[pl.ds(start, g), :][:, None, :] for m in mod_refs], g)


def _prenorm_kernel(x_ref, nw_ref, sc_ref, sh_ref, o_ref, *, tm):
    i = pl.program_id(0)

    def fn(mods, g):
        sc, sh = mods
        x = x_ref[...]
        ms = jnp.mean(x * x, axis=-1, keepdims=True)
        y = x * lax.rsqrt(ms + EPS) * nw_ref[...]
        y = y.reshape(g, tm // g, D) * (1.0 + sc) + sh
        o_ref[...] = y.reshape(tm, D).astype(o_ref.dtype)

    _grouped_apply(i, tm, [sc_ref, sh_ref], fn)


def _prenorm(x, norm_w4, k, mod, sh_blk, sc_blk, out_dtype, tm=512):
    return pl.pallas_call(
        functools.partial(_prenorm_kernel, tm=tm),
        grid=(M // tm,),
        in_specs=[
            pl.BlockSpec((tm, D), lambda i: (i, 0)),
            pl.BlockSpec((None, 1, D), lambda i: (k, 0, 0)),
            pl.BlockSpec((MOD_ROWS, D), lambda i: (0, sc_blk)),
            pl.BlockSpec((MOD_ROWS, D), lambda i: (0, sh_blk)),
        ],
        out_specs=pl.BlockSpec((tm, D), lambda i: (i, 0)),
        out_shape=jax.ShapeDtypeStruct((M, D), out_dtype),
        compiler_params=_cparams(1),
        name="prenorm",
    )(x, norm_w4, mod, mod)


def _mm_plain_kernel(*refs, act_lhs, has_bias):
    if has_bias:
        lhs_ref, w_ref, b_ref, o_ref, wb = refs
    else:
        lhs_ref, w_ref, o_ref, wb = refs

    @pl.when(pl.program_id(1) == 0)
    def _():
        wb[...] = w_ref[...].astype(BF16)

    lhs = lhs_ref[...]
    if act_lhs:
        lhs = _silu(lhs).astype(BF16)
    acc = jnp.dot(lhs, wb[...], preferred_element_type=F32)
    if has_bias:
        acc = acc + b_ref[...]
    o_ref[...] = acc.astype(o_ref.dtype)


def _mm_plain(lhs, w3, *, layer, col0, n_out, tn, tm, out_dtype=F32, bias=None, act_lhs=False, name="mm"):
    rows, k = lhs.shape
    off = col0 // tn
    in_specs = [
        pl.BlockSpec((tm, k), lambda j, i: (i, 0)),
        pl.BlockSpec((None, k, tn), lambda j, i: (layer, 0, j + off)),
    ]
    args = [lhs, w3]
    if bias is not None:
        in_specs.append(pl.BlockSpec((1, tn), lambda j, i: (0, j)))
        args.append(bias)
    return pl.pallas_call(
        functools.partial(_mm_plain_kernel, act_lhs=act_lhs, has_bias=bias is not None),
        grid=(n_out // tn, rows // tm),
        in_specs=in_specs,
        out_specs=pl.BlockSpec((tm, tn), lambda j, i: (i, j)),
        out_shape=jax.ShapeDtypeStruct((rows, n_out), out_dtype),
        scratch_shapes=[pltpu.VMEM((k, tn), BF16)],
        compiler_params=_cparams(2),
        name=name,
    )(*args)


def _mm_resid_kernel(lhs_ref, w_ref, res_ref, gate_ref, o_ref, wb, *, tm, tn):
    i = pl.program_id(1)

    @pl.when(i == 0)
    def _():
        wb[...] = w_ref[...].astype(BF16)

    def fn(mods, g):
        (gate,) = mods
        acc = jnp.dot(lhs_ref[...], wb[...], preferred_element_type=F32)
        out = res_ref[...].reshape(g, tm // g, tn) + gate * acc.reshape(g, tm // g, tn)
        o_ref[...] = out.reshape(tm, tn)

    _grouped_apply(i, tm, [gate_ref], fn)


def _mm_resid(lhs, w3, res, mod, gate_blk, *, tn, tm, name):
    rows, k = lhs.shape
    n_out = res.shape[1]
    goff = gate_blk * (D // tn)
    return pl.pallas_call(
        functools.partial(_mm_resid_kernel, tm=tm, tn=tn),
        grid=(n_out // tn, rows // tm),
        in_specs=[
            pl.BlockSpec((tm, k), lambda j, i: (i, 0)),
            pl.BlockSpec((None, k, tn), lambda j, i: (0, 0, j)),
            pl.BlockSpec((tm, tn), lambda j, i: (i, j)),
            pl.BlockSpec((MOD_ROWS, tn), lambda j, i: (0, goff + j)),
        ],
        out_specs=pl.BlockSpec((tm, tn), lambda j, i: (i, j)),
        out_shape=jax.ShapeDtypeStruct((rows, n_out), F32),
        scratch_shapes=[pltpu.VMEM((k, tn), BF16)],
        compiler_params=_cparams(2),
        name=name,
    )(lhs, w3, res, mod)


def _mm_swiglu_kernel(lhs_ref, wg_ref, wu_ref, o_ref, wgb, wub):
    @pl.when(pl.program_id(1) == 0)
    def _():
        wgb[...] = wg_ref[...].astype(BF16)
        wub[...] = wu_ref[...].astype(BF16)

    lhs = lhs_ref[...]
    gt = jnp.dot(lhs, wgb[...], preferred_element_type=F32)
    up = jnp.dot(lhs, wub[...], preferred_element_type=F32)
    o_ref[...] = (_silu(gt) * up).astype(o_ref.dtype)


def _mm_swiglu(lhs, w3, hidden, *, tn, tm, name):
    rows, k = lhs.shape
    nb = hidden // tn
    return pl.pallas_call(
        _mm_swiglu_kernel,
        grid=(nb, rows // tm),
        in_specs=[
            pl.BlockSpec((tm, k), lambda j, i: (i, 0)),
            pl.BlockSpec((None, k, tn), lambda j, i: (0, 0, j)),
            pl.BlockSpec((None, k, tn), lambda j, i: (0, 0, j + nb)),
        ],
        out_specs=pl.BlockSpec((tm, tn), lambda j, i: (i, j)),
        out_shape=jax.ShapeDtypeStruct((rows, hidden), BF16),
        scratch_shapes=[pltpu.VMEM((k, tn), BF16), pltpu.VMEM((k, tn), BF16)],
        compiler_params=_cparams(2),
        name=name,
    )(lhs, w3, w3)


def _expert_changed(te_ref, i):
    prev = te_ref[jnp.maximum(i - 1, 0)]
    return jnp.logical_or(i == 0, te_ref[i] != prev)


def _moe_gu_kernel(te_ref, nu_ref, lhs_ref, wg_ref, wu_ref, o_ref, wgb, wub):
    i = pl.program_id(1)

    @pl.when(_expert_changed(te_ref, i))
    def _():
        wgb[...] = wg_ref[...].astype(BF16)
        wub[...] = wu_ref[...].astype(BF16)

    @pl.when(i < nu_ref[0])
    def _():
        lhs = lhs_ref[...]
        gt = jnp.dot(lhs, wgb[...], preferred_element_type=F32)
        up = jnp.dot(lhs, wub[...], preferred_element_type=F32)
        o_ref[...] = (_silu(gt) * up).astype(o_ref.dtype)

    @pl.when(i >= nu_ref[0])
    def _():
        o_ref[...] = jnp.zeros_like(o_ref)


def _moe_gu(te, nu, xs, w_gu, *, tn=512):
    tm = MOE_TM
    nb = EH // tn
    grid_spec = pltpu.PrefetchScalarGridSpec(
        num_scalar_prefetch=2,
        grid=(nb, MOE_TILES),
        in_specs=[
            pl.BlockSpec((tm, D), lambda j, i, te, nu: (i, 0)),
            pl.BlockSpec((None, D, tn), lambda j, i, te, nu: (te[i], 0, j)),
            pl.BlockSpec((None, D, tn), lambda j, i, te, nu: (te[i], 0, j + nb)),
        ],
        out_specs=pl.BlockSpec((tm, tn), lambda j, i, te, nu: (i, j)),
        scratch_shapes=[pltpu.VMEM((D, tn), BF16), pltpu.VMEM((D, tn), BF16)],
    )
    return pl.pallas_call(
        _moe_gu_kernel,
        grid_spec=grid_spec,
        out_shape=jax.ShapeDtypeStruct((MOE_ROWS, EH), BF16),
        compiler_params=_cparams(2),
        name="moe_gu",
    )(te, nu, xs, w_gu, w_gu)


def _moe_down_kernel(te_ref, nu_ref, lhs_ref, w_ref, o_ref, wb):
    i = pl.program_id(1)

    @pl.when(_expert_changed(te_ref, i))
    def _():
        wb[...] = w_ref[...].astype(BF16)

    @pl.when(i < nu_ref[0])
    def _():
        o_ref[...] = jnp.dot(lhs_ref[...], wb[...], preferred_element_type=F32)

    @pl.when(i >= nu_ref[0])
    def _():
        o_ref[...] = jnp.zeros_like(o_ref)


def _moe_down(te, nu, act, w_down, *, tn=512):
    tm = MOE_TM
    grid_spec = pltpu.PrefetchScalarGridSpec(
        num_scalar_prefetch=2,
        grid=(D // tn, MOE_TILES),
        in_specs=[
            pl.BlockSpec((tm, EH), lambda j, i, te, nu: (i, 0)),
            pl.BlockSpec((None, EH, tn), lambda j, i, te, nu: (te[i], 0, j)),
        ],
        out_specs=pl.BlockSpec((tm, tn), lambda j, i, te, nu: (i, j)),
        scratch_shapes=[pltpu.VMEM((EH, tn), BF16)],
    )
    return pl.pallas_call(
        _moe_down_kernel,
        grid_spec=grid_spec,
        out_shape=jax.ShapeDtypeStruct((MOE_ROWS, D), F32),
        compiler_params=_cparams(2),
        name="moe_down",
    )(te, nu, act, w_down)


def _router_kernel(h_ref, r_ref, gv_ref, gi_ref):
    logits = jnp.dot(h_ref[...], r_ref[...], preferred_element_type=F32, precision=lax.Precision.HIGHEST)
    lane = lax.broadcasted_iota(I32, logits.shape, 1)
    neg = jnp.float32(-jnp.inf)
    l1 = jnp.where(lane < NE, logits, neg)
    m1 = jnp.max(l1, axis=-1, keepdims=True)
    i1 = jnp.min(jnp.where(l1 == m1, lane, 128), axis=-1, keepdims=True)
    l2 = jnp.where(lane == i1, neg, l1)
    m2 = jnp.max(l2, axis=-1, keepdims=True)
    i2 = jnp.min(jnp.where(l2 == m2, lane, 128), axis=-1, keepdims=True)
    e = jnp.exp(m2 - m1)
    g1 = 1.0 / (1.0 + e)
    g2 = e * g1
    gv_ref[...] = jnp.where(lane == 0, g1, jnp.where(lane == 1, g2, 0.0))
    gi_ref[...] = jnp.where(lane == 0, i1, jnp.where(lane == 1, i2, 0))


def _router(hn, router_pad, tm=512):
    return pl.pallas_call(
        _router_kernel,
        grid=(M // tm,),
        in_specs=[pl.BlockSpec((tm, D), lambda i: (i, 0)), pl.BlockSpec((D, 128), lambda i: (0, 0))],
        out_specs=[pl.BlockSpec((tm, 128), lambda i: (i, 0)), pl.BlockSpec((tm, 128), lambda i: (i, 0))],
        out_shape=[jax.ShapeDtypeStruct((M, 128), F32), jax.ShapeDtypeStruct((M, 128), I32)],
        compiler_params=_cparams(1),
        name="router",
    )(hn, router_pad)


def _row_copy(src_hbm, row, buf, r, sem):
    return pltpu.make_async_copy(src_hbm.at[pl.ds(row, 1), :], buf.at[pl.ds(r, 1), :], sem)


def _gather_kernel(idx_ref, src_hbm, o_ref, buf, sem, *, tm):
    def start(r, carry):
        _row_copy(src_hbm, idx_ref[0, 0, r], buf, r, sem).start()
        return carry

    lax.fori_loop(0, tm, start, 0)

    def wait(r, carry):
        _row_copy(src_hbm, 0, buf, r, sem).wait()
        return carry

    lax.fori_loop(0, tm, wait, 0)
    o_ref[...] = buf[...].astype(o_ref.dtype)


def _gather_rows(idx3, src):
    tm = MOE_TM
    return pl.pallas_call(
        functools.partial(_gather_kernel, tm=tm),
        grid=(MOE_TILES,),
        in_specs=[
            pl.BlockSpec((1, 1, tm), lambda t: (t, 0, 0), memory_space=pltpu.SMEM),
            pl.BlockSpec(memory_space=pl.ANY),
        ],
        out_specs=pl.BlockSpec((tm, D), lambda t: (t, 0)),
        out_shape=jax.ShapeDtypeStruct((MOE_ROWS, D), BF16),
        scratch_shapes=[pltpu.VMEM((tm, D), F32), pltpu.SemaphoreType.DMA(())],
        compiler_params=_cparams(1),
        name="moe_gather",
    )(idx3, src)


def _combine_kernel(pos_ref, ys_hbm, x_ref, gv_ref, gate_ref, fw_ref, o_ref, buf0, buf1, sem, *, tm):
    i = pl.program_id(0)

    def start(r, carry):
        _row_copy(ys_hbm, pos_ref[0, 0, r], buf0, r, sem).start()
        _row_copy(ys_hbm, pos_ref[0, 0, tm + r], buf1, r, sem).start()
        return carry

    lax.fori_loop(0, tm, start, 0)

    def wait(r, carry):
        _row_copy(ys_hbm, 0, buf0, r, sem).wait()
        _row_copy(ys_hbm, 0, buf1, r, sem).wait()
        return carry

    lax.fori_loop(0, tm, wait, 0)

    def fn(mods, g):
        (gate,) = mods
        gv = gv_ref[...]
        f = gv[:, 0:1] * buf0[...] + gv[:, 1:2] * buf1[...]
        x = x_ref[...].reshape(g, tm // g, D) + gate * f.reshape(g, tm // g, D)
        x = x.reshape(tm, D)
        ms = jnp.mean(x * x, axis=-1, keepdims=True)
        o_ref[...] = x * lax.rsqrt(ms + EPS) * fw_ref[...]

    _grouped_apply(i, tm, [gate_ref], fn)


def _combine(pos3, ys, x, gv, mod, gate_blk, final_w, tm=256):
    return pl.pallas_call(
        functools.partial(_combine_kernel, tm=tm),
        grid=(M // tm,),
        in_specs=[
            pl.BlockSpec((1, 1, 2 * tm), lambda t: (t, 0, 0), memory_space=pltpu.SMEM),
            pl.BlockSpec(memory_space=pl.ANY),
            pl.BlockSpec((tm, D), lambda t: (t, 0)),
            pl.BlockSpec((tm, 128), lambda t: (t, 0)),
            pl.BlockSpec((MOD_ROWS, D), lambda t: (0, gate_blk)),
            pl.BlockSpec((1, D), lambda t: (0, 0)),
        ],
        out_specs=pl.BlockSpec((tm, D), lambda t: (t, 0)),
        out_shape=jax.ShapeDtypeStruct((M, D), F32),
        scratch_shapes=[pltpu.VMEM((tm, D), F32), pltpu.VMEM((tm, D), F32), pltpu.SemaphoreType.DMA(())],
        compiler_params=_cparams(1),
        name="moe_combine",
    )(pos3, ys, x, gv, mod, final_w)


def _ssd_kernel(z_ref, xbc_ref, dt_ref, cs_ref, h0_ref, cw_ref, cb_ref, dtb_ref, alog_ref, dsk_ref, nw_ref,
                e2_ref, et2_ref, y_ref, csn_ref, hn_ref, cbuf, hst, *, C, nc):
    c = pl.program_id(1)
    hp = 128 // C
    neg = jnp.float32(-jnp.inf)

    @pl.when(c == 0)
    def _():
        cbuf[5:8, :] = cs_ref[...]
        hst[...] = h0_ref[...]

    cbuf[8:8 + C, :] = xbc_ref[...]

    @pl.when(c == nc - 1)
    def _():
        csn_ref[...] = cbuf[C + 5:C + 8, :]

    dt = _softplus(dt_ref[...] + dtb_ref[...])
    a = -jnp.exp(alog_ref[...])
    row = lax.broadcasted_iota(I32, (C, C), 0)
    col = lax.broadcasted_iota(I32, (C, C), 1)
    tril = jnp.where(row >= col, 1.0, 0.0).astype(BF16)
    b = _dot_exact_rhs(tril, dt * a)
    eb = jnp.exp(b)
    b_last = b[C - 1:C, :]
    dl = dt * jnp.exp(b_last - b)
    stack = jnp.concatenate([dt, eb, dl], axis=0)
    s_hi = stack.astype(BF16)
    s_lo = (stack - s_hi.astype(F32)).astype(BF16)
    lhs2 = jnp.concatenate([s_hi, s_lo], axis=1)
    bst = jnp.concatenate([b] + [pltpu.roll(b, 128 - k, axis=1) for k in range(1, hp)], axis=0)
    bt = bst.T
    rdec = jnp.broadcast_to(jnp.exp(bt[:, C - 1:C]), (128, 128))
    r_hi = rdec.astype(BF16)
    r_lo = (rdec - r_hi.astype(F32)).astype(BF16)
    r2 = jnp.concatenate([r_hi, r_lo], axis=0)

    lane = lax.broadcasted_iota(I32, (C, 128), 1)
    trow = lax.broadcasted_iota(I32, (C, 128), 0)
    causal = trow >= (lane & (C - 1))
    lane_seg = lane >> (C.bit_length() - 1)
    xlane_seg = lax.broadcasted_iota(I32, (C, hp * SSD_P), 1) >> (SSD_P.bit_length() - 1)

    def conv(cols):
        acc = cb_ref[:, cols] + cbuf[5:5 + C, cols] * cw_ref[0:1, cols]
        for k in range(1, SSD_K):
            acc = acc + cbuf[5 + k:5 + k + C, cols] * cw_ref[k:k + 1, cols]
        return _silu(acc)

    for g in range(SSD_G):
        xc = slice(g * SSD_GW, (g + 1) * SSD_GW)
        xg = conv(xc)
        bg = conv(slice(SSD_DI + g * SSD_N, SSD_DI + (g + 1) * SSD_N)).astype(BF16)
        cg = conv(slice(SSD_DI + SSD_G * SSD_N + g * SSD_N, SSD_DI + SSD_G * SSD_N + (g + 1) * SSD_N)).astype(BF16)
        ex = jnp.dot(lhs2, e2_ref[:, xc], preferred_element_type=F32)
        dt_e, eb_e, dl_e = ex[0:C], ex[C:2 * C], ex[2 * C:3 * C]
        xdt = xg * dt_e
        cb2 = lax.dot_general(cg, jnp.concatenate([bg] * hp, axis=0), NT_DIMS, preferred_element_type=F32)
        hg = hst[g * SSD_GW:(g + 1) * SSD_GW, :]
        y_inter = lax.dot_general(cg, hg.astype(BF16), NT_DIMS, preferred_element_type=F32)
        ys = []
        for p in range(8 // hp):
            h0 = g * 8 + p * hp
            bcol = jnp.broadcast_to(b[:, h0:h0 + 1], (C, 128))
            for k in range(1, hp):
                bcol = jnp.where(lane_seg >= k, b[:, h0 + k:h0 + k + 1], bcol)
            lm = jnp.exp(jnp.where(causal, bcol - bt[h0:h0 + 1, :], neg))
            m2 = (cb2 * lm).astype(BF16)
            xp = xdt[:, p * hp * SSD_P:(p + 1) * hp * SSD_P]
            rhs = jnp.concatenate([jnp.where(xlane_seg == k, xp, 0.0) for k in range(hp)], axis=0).astype(BF16)
            ys.append(jnp.dot(m2, rhs, preferred_element_type=F32))
        y = jnp.concatenate(ys, axis=1) + y_inter * eb_e + xg * dsk_ref[:, xc]
        y = y * _silu(z_ref[:, xc])
        ms = jnp.mean(y * y, axis=-1, keepdims=True)
        y_ref[:, xc] = (y * lax.rsqrt(ms + EPS) * nw_ref[:, xc]).astype(y_ref.dtype)
        upd = lax.dot_general((xg * dl_e).astype(BF16), bg, TN_DIMS, preferred_element_type=F32)
        dec = jnp.dot(et2_ref[g * SSD_GW:(g + 1) * SSD_GW, :], r2, preferred_element_type=F32)
        hst[g * SSD_GW:(g + 1) * SSD_GW, :] = hg * dec + upd

    cbuf[5:8, :] = cbuf[C + 5:C + 8, :]

    @pl.when(c == nc - 1)
    def _():
        hn_ref[...] = hst[...]


def _ssd_scan(z, xbc, dtr, conv_state, h0, consts, *, nb, seq, C, row0):
    nc = seq // C
    rb0 = row0 // C
    cw, cb, dtb, alog, dsk, nw, e2, et2 = consts
    rows = lambda b, c: (rb0 + b * nc + c, 0)
    fixed = lambda b, c: (0, 0)
    per_b = lambda b, c: (b, 0, 0)
    return pl.pallas_call(
        functools.partial(_ssd_kernel, C=C, nc=nc),
        grid=(nb, nc),
        in_specs=[
            pl.BlockSpec((C, SSD_DI), rows),
            pl.BlockSpec((C, SSD_XBC), rows),
            pl.BlockSpec((C, 128), rows),
            pl.BlockSpec((None, SSD_K - 1, SSD_XBC), per_b),
            pl.BlockSpec((None, SSD_DI, SSD_N), per_b),
            pl.BlockSpec((SSD_K, SSD_XBC), fixed),
            pl.BlockSpec((1, SSD_XBC), fixed),
            pl.BlockSpec((1, 128), fixed),
            pl.BlockSpec((1, 128), fixed),
            pl.BlockSpec((1, SSD_DI), fixed),
            pl.BlockSpec((1, SSD_DI), fixed),
            pl.BlockSpec((256, SSD_DI), fixed),
            pl.BlockSpec((SSD_DI, 256), fixed),
        ],
        out_specs=[
            pl.BlockSpec((C, SSD_DI), lambda b, c: (b * nc + c, 0)),
            pl.BlockSpec((None, SSD_K - 1, SSD_XBC), per_b),
            pl.BlockSpec((None, SSD_DI, SSD_N), per_b),
        ],
        out_shape=[
            jax.ShapeDtypeStruct((nb * seq, SSD_DI), BF16),
            jax.ShapeDtypeStruct((nb, SSD_K - 1, SSD_XBC), F32),
            jax.ShapeDtypeStruct((nb, SSD_DI, SSD_N), F32),
        ],
        scratch_shapes=[pltpu.VMEM((8 + C, SSD_XBC), F32), pltpu.VMEM((SSD_DI, SSD_N), F32)],
        compiler_params=_cparams(2),
        name=f"ssd_scan_c{C}",
    )(z, xbc, dtr, conv_state, h0, cw, cb, dtb, alog, dsk, nw, e2, et2)


def _gla_kernel(q_ref, f_ref, v_ref, g_ref, s0_ref, lb_ref, nw_ref, o_ref, sn_ref, st, b_s, k_s, *, T, nc):
    c = pl.program_id(1)
    neg = jnp.float32(-jnp.inf)

    @pl.when(c == 0)
    def _():
        for h in range(HG_H):
            st[h * 128:(h + 1) * 128, :] = s0_ref[h * 128:(h + 1) * 128, :].T

    row = lax.broadcasted_iota(I32, (T, T), 0)
    col = lax.broadcasted_iota(I32, (T, T), 1)
    sh = HG_BLK.bit_length() - 1
    same_blk = (row >> sh) == (col >> sh)
    bd = jnp.where(jnp.logical_and(same_blk, row >= col), 1.0, 0.0).astype(BF16)

    for h in range(HG_H):
        hl = slice(h * 128, (h + 1) * 128)
        fz = f_ref[:, hl]
        lb = lb_ref[:, hl]
        ls = jnp.minimum(fz, 0.0) - jnp.log1p(jnp.exp(-jnp.abs(fz)))
        a1 = jnp.log(lb)
        a2 = jnp.log1p(-lb) + ls
        lf = jnp.maximum(a1, a2) + jnp.log1p(jnp.exp(-jnp.abs(a1 - a2)))
        k_s[:, hl] = (1.0 - lb) * jax.nn.sigmoid(-fz)
        b_s[:, hl] = _dot_exact_rhs(bd, lf)

    trow = lax.broadcasted_iota(I32, (HG_BLK, 128), 0)

    def blk(j, carry):
        rows = pl.ds(pl.multiple_of(j * HG_BLK, HG_BLK), HG_BLK)
        for h in range(HG_H):
            hl = slice(h * 128, (h + 1) * 128)
            qb = q_ref[rows, hl]
            bb = b_s[rows, hl]
            kb = k_s[rows, hl]
            vb = v_ref[rows, hl]
            sth = st[hl, :]
            o = lax.dot_general((qb * jnp.exp(bb)).astype(BF16), sth.astype(BF16), NT_DIMS,
                                preferred_element_type=F32)
            for s in range(HG_BLK):
                dec = jnp.exp(jnp.where(trow >= s, bb - bb[s:s + 1, :], neg))
                att = jnp.sum(qb * dec * kb[s:s + 1, :], axis=-1, keepdims=True)
                o = o + att * vb[s:s + 1, :]
            b_end = bb[HG_BLK - 1:HG_BLK, :]
            khat = (kb * jnp.exp(b_end - bb)).astype(BF16)
            upd = lax.dot_general(vb.astype(BF16), khat, TN_DIMS, preferred_element_type=F32)
            st[hl, :] = sth * jnp.exp(b_end) + upd
            ms = jnp.mean(o * o, axis=-1, keepdims=True)
            o_ref[rows, hl] = (o * lax.rsqrt(ms + EPS) * nw_ref[...] * _silu(g_ref[rows, hl])).astype(o_ref.dtype)
        return carry

    lax.fori_loop(0, T // HG_BLK, blk, 0)

    @pl.when(c == nc - 1)
    def _():
        for h in range(HG_H):
            sn_ref[h * 128:(h + 1) * 128, :] = st[h * 128:(h + 1) * 128, :].T


def _gla_scan(qfig, s0, lb, nw, *, nb, seq, T, row0):
    nc = seq // T
    rb0 = row0 // T
    per_b = lambda b, c: (b, 0, 0)

    def cols(k):
        return pl.BlockSpec((T, D), lambda b, c: (rb0 + b * nc + c, k))

    return pl.pallas_call(
        functools.partial(_gla_kernel, T=T, nc=nc),
        grid=(nb, nc),
        in_specs=[
            cols(0), cols(1), cols(2), cols(3),
            pl.BlockSpec((None, D, HG_DV), per_b),
            pl.BlockSpec((1, D), lambda b, c: (0, 0)),
            pl.BlockSpec((1, HG_DV), lambda b, c: (0, 0)),
        ],
        out_specs=[
            pl.BlockSpec((T, D), lambda b, c: (b * nc + c, 0)),
            pl.BlockSpec((None, D, HG_DV), per_b),
        ],
        out_shape=[
            jax.ShapeDtypeStruct((nb * seq, D), BF16),
            jax.ShapeDtypeStruct((nb, D, HG_DV), F32),
        ],
        scratch_shapes=[pltpu.VMEM((D, HG_DK), F32), pltpu.VMEM((T, D), F32), pltpu.VMEM((T, D), F32)],
        compiler_params=_cparams(2),
        name=f"gla_scan_t{T}",
    )(qfig, qfig, qfig, qfig, s0, lb, nw)


def _route_plan(top_i):
    e_flat = top_i.reshape(-1)
    onehot = (e_flat[:, None] == jnp.arange(NE, dtype=I32)[None, :]).astype(I32)
    csum = jnp.cumsum(onehot, axis=0)
    rank = jnp.sum(onehot * (csum - 1), axis=1)
    cnt = csum[-1]
    ntile = (cnt + MOE_TM - 1) // MOE_TM
    cum_t = jnp.cumsum(ntile)
    tile0 = cum_t - ntile
    pos = jnp.sum(onehot * tile0[None, :], axis=1) * MOE_TM + rank
    n_used = cum_t[-1]
    t_ids = jnp.arange(MOE_TILES, dtype=I32)
    te = jnp.sum((t_ids[:, None] >= cum_t[None, :]).astype(I32), axis=1)
    te_last = jnp.sum((n_used - 1 >= cum_t).astype(I32))
    te = jnp.where(t_ids < n_used, te, te_last).astype(I32)
    tok = jnp.arange(2 * M, dtype=I32) // 2
    src = jnp.zeros((MOE_ROWS,), I32).at[pos].set(tok)
    return te, n_used.reshape(1).astype(I32), src, pos.reshape(M, 2)


def kernel(x_prompt, x_sample, c_prompt, c_sample, state_ssd_conv, state_ssd, state_hgrn, ada_w, ada_b, norm_w,
           ssd_w_in, ssd_conv_w, ssd_conv_b, ssd_dt_bias, ssd_a_log, ssd_d, ssd_norm_w, ssd_w_out, hgrn_w_in,
           hgrn_lb_logits, hgrn_norm_w, hgrn_w_out, ffn_w_gu, ffn_w_down, moe_router, moe_w_gu, moe_w_down,
           final_norm_w):
    x = jnp.concatenate([x_prompt.reshape(MP, D), x_sample.reshape(MS, D)], axis=0)
    c_all = jnp.concatenate([c_prompt, jnp.zeros((MOD_S0 - NP, D), F32), c_sample], axis=0)
    norm_w4 = norm_w.reshape(4, 1, D)

    mods = [
        _mm_plain(c_all, ada_w, layer=l, col0=0, n_out=6 * D, tn=1024, tm=MOD_ROWS,
                  bias=ada_b[l].reshape(1, 6 * D), act_lhs=True, name="adaln")
        for l in range(2)
    ]

    hn = _prenorm(x, norm_w4, 0, mods[0], 0, 1, BF16)
    z = _mm_plain(hn, ssd_w_in, layer=0, col0=0, n_out=SSD_DI, tn=1024, tm=512, name="ssd_in_z")
    xbc = _mm_plain(hn, ssd_w_in, layer=0, col0=SSD_DI, n_out=SSD_XBC, tn=1024, tm=512, name="ssd_in_xbc")
    w_dt = jnp.pad(ssd_w_in[:, :, SSD_DI + SSD_XBC:], ((0, 0), (0, 0), (0, 128 - SSD_HEADS)))
    dtr = _mm_plain(hn, w_dt, layer=0, col0=0, n_out=128, tn=128, tm=512, name="ssd_in_dt")

    pad64 = lambda v: jnp.pad(v.reshape(1, SSD_HEADS), ((0, 0), (0, 128 - SSD_HEADS)))
    head_of = jnp.arange(SSD_DI, dtype=I32) // SSD_P
    e1 = (jnp.arange(128, dtype=I32)[:, None] == head_of[None, :]).astype(BF16)
    e2 = jnp.concatenate([e1, e1], axis=0)
    et2 = jnp.concatenate([e1.T, e1.T], axis=1)
    consts = (ssd_conv_w[0], ssd_conv_b[0].reshape(1, SSD_XBC), pad64(ssd_dt_bias[0]), pad64(ssd_a_log[0]),
              jnp.repeat(ssd_d[0], SSD_P).reshape(1, SSD_DI), ssd_norm_w[0].reshape(1, SSD_DI), e2, et2)
    yp, p_conv, p_ssd = _ssd_scan(z, xbc, dtr, jnp.zeros((NP, SSD_K - 1, SSD_XBC), F32),
                                  jnp.zeros((NP, SSD_DI, SSD_N), F32), consts, nb=NP, seq=LP, C=64, row0=0)
    ysm, s_conv, s_ssd = _ssd_scan(z, xbc, dtr, state_ssd_conv[0], state_ssd[0].reshape(NS, SSD_DI, SSD_N), consts,
                                   nb=NS, seq=LS, C=32, row0=MP)
    y = jnp.concatenate([yp, ysm], axis=0)
    x = _mm_resid(y, ssd_w_out, x, mods[0], 2, tn=512, tm=512, name="ssd_out")

    hn = _prenorm(x, norm_w4, 1, mods[0], 3, 4, BF16)
    act = _mm_swiglu(hn, ffn_w_gu, FFN, tn=512, tm=512, name="ffn_gu")
    x = _mm_resid(act, ffn_w_down, x, mods[0], 5, tn=512, tm=512, name="ffn_down")

    hn = _prenorm(x, norm_w4, 2, mods[1], 0, 1, BF16)
    qfig = _mm_plain(hn, hgrn_w_in, layer=0, col0=0, n_out=4 * D, tn=1024, tm=512, name="hgrn_in")
    p = jax.nn.softmax(hgrn_lb_logits.astype(F32), axis=0)
    lb = (jnp.cumsum(p, axis=0) - p[0])[1].reshape(1, D)
    nw_h = hgrn_norm_w[0].reshape(1, HG_DV)
    op, p_hgrn = _gla_scan(qfig, jnp.zeros((NP, D, HG_DV), F32), lb, nw_h, nb=NP, seq=LP, T=64, row0=0)
    osm, s_hgrn = _gla_scan(qfig, state_hgrn[0].reshape(NS, D, HG_DV), lb, nw_h, nb=NS, seq=LS, T=32, row0=MP)
    o = jnp.concatenate([op, osm], axis=0)
    x = _mm_resid(o, hgrn_w_out, x, mods[1], 2, tn=512, tm=512, name="hgrn_out")

    hn32 = _prenorm(x, norm_w4, 3, mods[1], 3, 4, F32)
    gv, gi = _router(hn32, jnp.pad(moe_router[0], ((0, 0), (0, 128 - NE))))
    te, nu, src, pos = _route_plan(gi[:, :2])
    xs = _gather_rows(src.reshape(MOE_TILES, 1, MOE_TM), hn32)
    act = _moe_gu(te, nu, xs, moe_w_gu[0])
    ys = _moe_down(te, nu, act, moe_w_down[0])
    tmc = 256
    pos3 = jnp.concatenate([pos[:, 0].reshape(M // tmc, 1, tmc), pos[:, 1].reshape(M // tmc, 1, tmc)], axis=2)
    yout = _combine(pos3, ys, x, gv, mods[1], 5, final_norm_w.reshape(1, D), tm=tmc)

    return (
        yout[:MP].reshape(NP, LP, D),
        yout[MP:].reshape(NS, LS, D),
        p_conv[None],
        p_ssd.reshape(1, NP, SSD_HEADS, SSD_P, SSD_N),
        p_hgrn.reshape(1, NP, HG_H, HG_DK, HG_DV),
        s_conv[None],
        s_ssd.reshape(1, NS, SSD_HEADS, SSD_P, SSD_N),
        s_hgrn.reshape(1, NS, HG_H, HG_DK, HG_DV),
    )
```

```python
import functools

import jax
import jax.numpy as jnp
from jax import lax
from jax.experimental import pallas as pl
from jax.experimental.pallas import tpu as pltpu

F32 = jnp.float32
BF16 = jnp.bfloat16
I32 = jnp.int32

EPS = 1e-6
LOG2E = 1.4426950408889634
D = 2048
NP, LP = 4, 2048
NS, LS = 32, 32
MP, MS = NP * LP, NS * LS
M = MP + MS
MOD_S0 = 8
MOD_ROWS = MOD_S0 + NS

SSD_DI = 4096
SSD_HEADS = 64
SSD_P = 64
SSD_G = 8
SSD_GW = SSD_DI // SSD_G
SSD_N = 128
SSD_XBC = SSD_DI + 2 * SSD_G * SSD_N
SSD_K = 4

HG_H = 16
HG_DK = 128
HG_DV = 128
HG_BLK = 16

FFN = 5632
NE = 8
EH = 7168
MOE_TM = 512
MOE_TILES = (2 * M) // MOE_TM + NE
MOE_ROWS = MOE_TILES * MOE_TM

VMEM_LIMIT = 56 * 1024 * 1024

NT_DIMS = (((1,), (1,)), ((), ()))
TN_DIMS = (((0,), (0,)), ((), ()))


def _cparams(n_axes):
    return pltpu.CompilerParams(dimension_semantics=("arbitrary",) * n_axes,
                                vmem_limit_bytes=VMEM_LIMIT)


def _silu(x):
    return x * jax.nn.sigmoid(x)


def _softplus(x):
    return jnp.maximum(x, 0.0) + jnp.log1p(jnp.exp(-jnp.abs(x)))


def _split3(x):
    hi = x.astype(BF16)
    r1 = x - hi.astype(F32)
    mid = r1.astype(BF16)
    lo = (r1 - mid.astype(F32)).astype(BF16)
    return hi, mid, lo


def _dot_exact_rhs(mat_bf16, x):
    hi, mid, lo = _split3(x)
    acc = jnp.dot(mat_bf16, hi, preferred_element_type=F32)
    acc = acc + jnp.dot(mat_bf16, mid, preferred_element_type=F32)
    return acc + jnp.dot(mat_bf16, lo, preferred_element_type=F32)


def _grouped_apply(i, tm, mod_refs, fn):
    n_pt = MP // tm
    tpb = LP // tm

    @pl.when(i < n_pt)
    def _():
        r = i // tpb
        fn([m[pl.ds(r, 1), :][None] for m in mod_refs], 1, 0)

    @pl.when(i >= n_pt)
    def _():
        g = tm // LS
        start = pl.multiple_of(MOD_S0 + (i - n_pt) * g, 8)
        fn([m[pl.ds(start, g), :][:, None, :] for m in mod_refs], g, 1)


def _row_operand(op, tm, width, ij):
    if not isinstance(op, tuple):
        return [pl.BlockSpec((tm, width), lambda *g: ij(*g))], [op]
    n_pt = MP // tm
    return [
        pl.BlockSpec((tm, width), lambda *g: (jnp.minimum(ij(*g)[0], n_pt - 1), ij(*g)[1])),
        pl.BlockSpec((tm, width), lambda *g: (jnp.maximum(ij(*g)[0] - n_pt, 0), ij(*g)[1])),
    ], list(op)


def _prenorm_kernel(*refs, tm, nx):
    x_refs = refs[:nx]
    nw_ref, sc_ref, sh_ref, o_ref = refs[nx:]
    i = pl.program_id(0)

    def fn(mods, g, s):
        sc, sh = mods
        x = x_refs[s % nx][...]
        ms = jnp.mean(x * x, axis=-1, keepdims=True)
        y = x * lax.rsqrt(ms + EPS) * nw_ref[...]
        y = y.reshape(g, tm // g, D) * (1.0 + sc) + sh
        o_ref[...] = y.reshape(tm, D).astype(o_ref.dtype)

    _grouped_apply(i, tm, [sc_ref, sh_ref], fn)


def _prenorm(x, norm_w4, k, mod, sh_blk, sc_blk, out_dtype, tm=512):
    x_specs, x_args = _row_operand(x, tm, D, lambda i: (i, 0))
    return pl.pallas_call(
        functools.partial(_prenorm_kernel, tm=tm, nx=len(x_args)),
        grid=(M // tm,),
        in_specs=x_specs + [
            pl.BlockSpec((None, 1, D), lambda i: (k, 0, 0)),
            pl.BlockSpec((MOD_ROWS, D), lambda i: (0, sc_blk)),
            pl.BlockSpec((MOD_ROWS, D), lambda i: (0, sh_blk)),
        ],
        out_specs=pl.BlockSpec((tm, D), lambda i: (i, 0)),
        out_shape=jax.ShapeDtypeStruct((M, D), out_dtype),
        compiler_params=_cparams(1),
        name="prenorm",
    )(*x_args, norm_w4, mod, mod)


def _mm_plain_kernel(*refs, act_lhs, has_bias):
    if has_bias:
        lhs_ref, w_ref, b_ref, o_ref, wb = refs
    else:
        lhs_ref, w_ref, o_ref, wb = refs

    @pl.when(pl.program_id(1) == 0)
    def _():
        wb[...] = w_ref[...].astype(BF16)

    lhs = lhs_ref[...]
    if act_lhs:
        lhs = _silu(lhs).astype(BF16)
    acc = jnp.dot(lhs, wb[...], preferred_element_type=F32)
    if has_bias:
        acc = acc + b_ref[...]
    o_ref[...] = acc.astype(o_ref.dtype)


def _mm_plain(lhs, w3, *, layer, col0, n_out, tn, tm, out_dtype=F32, bias=None, act_lhs=False, name="mm"):
    rows, k = lhs.shape
    off = col0 // tn
    in_specs = [
        pl.BlockSpec((tm, k), lambda j, i: (i, 0)),
        pl.BlockSpec((None, k, tn), lambda j, i: (layer, 0, j + off)),
    ]
    args = [lhs, w3]
    if bias is not None:
        in_specs.append(pl.BlockSpec((1, tn), lambda j, i: (0, j)))
        args.append(bias)
    return pl.pallas_call(
        functools.partial(_mm_plain_kernel, act_lhs=act_lhs, has_bias=bias is not None),
        grid=(n_out // tn, rows // tm),
        in_specs=in_specs,
        out_specs=pl.BlockSpec((tm, tn), lambda j, i: (i, j)),
        out_shape=jax.ShapeDtypeStruct((rows, n_out), out_dtype),
        scratch_shapes=[pltpu.VMEM((k, tn), BF16)],
        compiler_params=_cparams(2),
        name=name,
    )(*args)


def _mm_resid_kernel(*refs, tm, tn, nl, nr):
    lhs_refs = refs[:nl]
    w_ref = refs[nl]
    res_refs = refs[nl + 1:nl + 1 + nr]
    gate_ref, o_ref, wb = refs[nl + 1 + nr:]
    i = pl.program_id(1)

    @pl.when(i == 0)
    def _():
        wb[...] = w_ref[...].astype(BF16)

    def fn(mods, g, s):
        (gate,) = mods
        acc = jnp.dot(lhs_refs[s % nl][...], wb[...], preferred_element_type=F32)
        out = res_refs[s % nr][...].reshape(g, tm // g, tn) + gate * acc.reshape(g, tm // g, tn)
        o_ref[...] = out.reshape(tm, tn)

    _grouped_apply(i, tm, [gate_ref], fn)


def _mm_resid(lhs, w3, res, mod, gate_blk, *, k, tn, tm, name):
    goff = gate_blk * (D // tn)
    l_specs, l_args = _row_operand(lhs, tm, k, lambda j, i: (i, 0))
    r_specs, r_args = _row_operand(res, tm, tn, lambda j, i: (i, j))
    return pl.pallas_call(
        functools.partial(_mm_resid_kernel, tm=tm, tn=tn, nl=len(l_args), nr=len(r_args)),
        grid=(D // tn, M // tm),
        in_specs=l_specs + [pl.BlockSpec((None, k, tn), lambda j, i: (0, 0, j))] + r_specs + [
            pl.BlockSpec((MOD_ROWS, tn), lambda j, i: (0, goff + j)),
        ],
        out_specs=pl.BlockSpec((tm, tn), lambda j, i: (i, j)),
        out_shape=jax.ShapeDtypeStruct((M, D), F32),
        scratch_shapes=[pltpu.VMEM((k, tn), BF16)],
        compiler_params=_cparams(2),
        name=name,
    )(*l_args, w3, *r_args, mod)


def _mm_swiglu_kernel(lhs_ref, wg_ref, wu_ref, o_ref, wgb, wub):
    @pl.when(pl.program_id(1) == 0)
    def _():
        wgb[...] = wg_ref[...].astype(BF16)
        wub[...] = wu_ref[...].astype(BF16)

    lhs = lhs_ref[...]
    gt = jnp.dot(lhs, wgb[...], preferred_element_type=F32)
    up = jnp.dot(lhs, wub[...], preferred_element_type=F32)
    o_ref[...] = (_silu(gt) * up).astype(o_ref.dtype)


def _mm_swiglu(lhs, w3, hidden, *, tn, tm, name):
    rows, k = lhs.shape
    nb = hidden // tn
    return pl.pallas_call(
        _mm_swiglu_kernel,
        grid=(nb, rows // tm),
        in_specs=[
            pl.BlockSpec((tm, k), lambda j, i: (i, 0)),
            pl.BlockSpec((None, k, tn), lambda j, i: (0, 0, j)),
            pl.BlockSpec((None, k, tn), lambda j, i: (0, 0, j + nb)),
        ],
        out_specs=pl.BlockSpec((tm, tn), lambda j, i: (i, j)),
        out_shape=jax.ShapeDtypeStruct((rows, hidden), BF16),
        scratch_shapes=[pltpu.VMEM((k, tn), BF16), pltpu.VMEM((k, tn), BF16)],
        compiler_params=_cparams(2),
        name=name,
    )(lhs, w3, w3)


def _expert_changed(te_ref, i):
    prev = te_ref[jnp.maximum(i - 1, 0)]
    return jnp.logical_or(i == 0, te_ref[i] != prev)


def _moe_gu_kernel(te_ref, nu_ref, lhs_ref, wg_ref, wu_ref, o_ref, wgb, wub):
    i = pl.program_id(1)

    @pl.when(_expert_changed(te_ref, i))
    def _():
        wgb[...] = wg_ref[...].astype(BF16)
        wub[...] = wu_ref[...].astype(BF16)

    @pl.when(i < nu_ref[0])
    def _():
        lhs = lhs_ref[...]
        gt = jnp.dot(lhs, wgb[...], preferred_element_type=F32)
        up = jnp.dot(lhs, wub[...], preferred_element_type=F32)
        o_ref[...] = (_silu(gt) * up).astype(o_ref.dtype)

    @pl.when(i >= nu_ref[0])
    def _():
        o_ref[...] = jnp.zeros_like(o_ref)


def _moe_gu(te, nu, xs, w_gu, *, tn=1024):
    tm = MOE_TM
    nb = EH // tn
    grid_spec = pltpu.PrefetchScalarGridSpec(
        num_scalar_prefetch=2,
        grid=(nb, MOE_TILES),
        in_specs=[
            pl.BlockSpec((tm, D), lambda j, i, te, nu: (i, 0)),
            pl.BlockSpec((None, D, tn), lambda j, i, te, nu: (te[i], 0, j)),
            pl.BlockSpec((None, D, tn), lambda j, i, te, nu: (te[i], 0, j + nb)),
        ],
        out_specs=pl.BlockSpec((tm, tn), lambda j, i, te, nu: (i, j)),
        scratch_shapes=[pltpu.VMEM((D, tn), BF16), pltpu.VMEM((D, tn), BF16)],
    )
    return pl.pallas_call(
        _moe_gu_kernel,
        grid_spec=grid_spec,
        out_shape=jax.ShapeDtypeStruct((MOE_ROWS, EH), BF16),
        compiler_params=_cparams(2),
        name="moe_gu",
    )(te, nu, xs, w_gu, w_gu)


def _moe_down_kernel(te_ref, nu_ref, lhs_ref, w_ref, o_ref, wb):
    i = pl.program_id(1)

    @pl.when(_expert_changed(te_ref, i))
    def _():
        wb[...] = w_ref[...].astype(BF16)

    @pl.when(i < nu_ref[0])
    def _():
        o_ref[...] = jnp.dot(lhs_ref[...], wb[...], preferred_element_type=F32)

    @pl.when(i >= nu_ref[0])
    def _():
        o_ref[...] = jnp.zeros_like(o_ref)


def _moe_down(te, nu, act, w_down, *, tn=512):
    tm = MOE_TM
    grid_spec = pltpu.PrefetchScalarGridSpec(
        num_scalar_prefetch=2,
        grid=(D // tn, MOE_TILES),
        in_specs=[
            pl.BlockSpec((tm, EH), lambda j, i, te, nu: (i, 0)),
            pl.BlockSpec((None, EH, tn), lambda j, i, te, nu: (te[i], 0, j)),
        ],
        out_specs=pl.BlockSpec((tm, tn), lambda j, i, te, nu: (i, j)),
        scratch_shapes=[pltpu.VMEM((EH, tn), BF16)],
    )
    return pl.pallas_call(
        _moe_down_kernel,
        grid_spec=grid_spec,
        out_shape=jax.ShapeDtypeStruct((MOE_ROWS, D), F32),
        compiler_params=_cparams(2),
        name="moe_down",
    )(te, nu, act, w_down)


def _router_kernel(h_ref, r_ref, gv_ref, gi_ref):
    logits = jnp.dot(h_ref[...], r_ref[...], preferred_element_type=F32, precision=lax.Precision.HIGHEST)
    lane = lax.broadcasted_iota(I32, logits.shape, 1)
    neg = jnp.float32(-jnp.inf)
    l1 = jnp.where(lane < NE, logits, neg)
    m1 = jnp.max(l1, axis=-1, keepdims=True)
    i1 = jnp.min(jnp.where(l1 == m1, lane, 128), axis=-1, keepdims=True)
    l2 = jnp.where(lane == i1, neg, l1)
    m2 = jnp.max(l2, axis=-1, keepdims=True)
    i2 = jnp.min(jnp.where(l2 == m2, lane, 128), axis=-1, keepdims=True)
    e = jnp.exp(m2 - m1)
    g1 = 1.0 / (1.0 + e)
    g2 = e * g1
    gv_ref[...] = jnp.where(lane == 0, g1, jnp.where(lane == 1, g2, 0.0))
    gi_ref[...] = jnp.where(lane == 0, i1, jnp.where(lane == 1, i2, 0))


def _router(hn, router_pad, tm=512):
    return pl.pallas_call(
        _router_kernel,
        grid=(M // tm,),
        in_specs=[pl.BlockSpec((tm, D), lambda i: (i, 0)), pl.BlockSpec((D, 128), lambda i: (0, 0))],
        out_specs=[pl.BlockSpec((tm, 128), lambda i: (i, 0)), pl.BlockSpec((tm, 128), lambda i: (i, 0))],
        out_shape=[jax.ShapeDtypeStruct((M, 128), F32), jax.ShapeDtypeStruct((M, 128), I32)],
        compiler_params=_cparams(1),
        name="router",
    )(hn, router_pad)


def _row_copy(src_hbm, row, buf, r, sem):
    return pltpu.make_async_copy(src_hbm.at[pl.ds(row, 1), :], buf.at[pl.ds(r, 1), :], sem)


def _start_rows(src_hbm, idx_ref, off, dst, sem, n):
    def body(r, carry):
        _row_copy(src_hbm, idx_ref[0, 0, off + r], dst, r, sem).start()
        return carry

    lax.fori_loop(0, n, body, 0, unroll=8)


def _wait_rows(src_hbm, dst, sem, n):
    pltpu.make_async_copy(src_hbm.at[pl.ds(0, n), :], dst, sem).wait()


def _gather_kernel(idx_ref, idx_next_ref, src_hbm, o_ref, buf, sem, *, tm, nt):
    t = pl.program_id(0)
    slot = lax.rem(t, 2)

    @pl.when(t == 0)
    def _():
        _start_rows(src_hbm, idx_ref, 0, buf.at[0], sem.at[0], tm)

    @pl.when(t + 1 < nt)
    def _():
        _start_rows(src_hbm, idx_next_ref, 0, buf.at[1 - slot], sem.at[1 - slot], tm)

    _wait_rows(src_hbm, buf.at[slot], sem.at[slot], tm)
    o_ref[...] = buf[slot].astype(o_ref.dtype)


def _gather_rows(idx3, src):
    tm = MOE_TM
    nt = MOE_TILES
    return pl.pallas_call(
        functools.partial(_gather_kernel, tm=tm, nt=nt),
        grid=(nt,),
        in_specs=[
            pl.BlockSpec((1, 1, tm), lambda t: (t, 0, 0), memory_space=pltpu.SMEM),
            pl.BlockSpec((1, 1, tm), lambda t: (jnp.minimum(t + 1, nt - 1), 0, 0), memory_space=pltpu.SMEM),
            pl.BlockSpec(memory_space=pl.ANY),
        ],
        out_specs=pl.BlockSpec((tm, D), lambda t: (t, 0)),
        out_shape=jax.ShapeDtypeStruct((MOE_ROWS, D), BF16),
        scratch_shapes=[pltpu.VMEM((2, tm, D), F32), pltpu.SemaphoreType.DMA((2,))],
        compiler_params=_cparams(1),
        name="moe_gather",
    )(idx3, idx3, src)


def _combine_kernel(pos_ref, pos_next_ref, ys_hbm, x_ref, gv_ref, gate_ref, fw_ref, op_ref, os_ref, buf, sem,
                    *, tm, nt):
    t = pl.program_id(0)
    slot = lax.rem(t, 2)

    def start(idx_ref, s):
        _start_rows(ys_hbm, idx_ref, 0, buf.at[s, 0], sem.at[s, 0], tm)
        _start_rows(ys_hbm, idx_ref, tm, buf.at[s, 1], sem.at[s, 1], tm)

    @pl.when(t == 0)
    def _():
        start(pos_ref, 0)

    @pl.when(t + 1 < nt)
    def _():
        start(pos_next_ref, 1 - slot)

    _wait_rows(ys_hbm, buf.at[slot, 0], sem.at[slot, 0], tm)
    _wait_rows(ys_hbm, buf.at[slot, 1], sem.at[slot, 1], tm)

    def fn(mods, g, s):
        (gate,) = mods
        gv = gv_ref[...]
        f = gv[:, 0:1] * buf[slot, 0] + gv[:, 1:2] * buf[slot, 1]
        x = x_ref[...].reshape(g, tm // g, D) + gate * f.reshape(g, tm // g, D)
        x = x.reshape(tm, D)
        ms = jnp.mean(x * x, axis=-1, keepdims=True)
        (op_ref, os_ref)[s][...] = x * lax.rsqrt(ms + EPS) * fw_ref[...]

    _grouped_apply(t, tm, [gate_ref], fn)


def _combine(pos3, ys, x, gv, mod, gate_blk, final_w, tm=256):
    nt = M // tm
    n_pt = MP // tm
    return pl.pallas_call(
        functools.partial(_combine_kernel, tm=tm, nt=nt),
        grid=(nt,),
        in_specs=[
            pl.BlockSpec((1, 1, 2 * tm), lambda t: (t, 0, 0), memory_space=pltpu.SMEM),
            pl.BlockSpec((1, 1, 2 * tm), lambda t: (jnp.minimum(t + 1, nt - 1), 0, 0), memory_space=pltpu.SMEM),
            pl.BlockSpec(memory_space=pl.ANY),
            pl.BlockSpec((tm, D), lambda t: (t, 0)),
            pl.BlockSpec((tm, 128), lambda t: (t, 0)),
            pl.BlockSpec((MOD_ROWS, D), lambda t: (0, gate_blk)),
            pl.BlockSpec((1, D), lambda t: (0, 0)),
        ],
        out_specs=[
            pl.BlockSpec((tm, D), lambda t: (jnp.minimum(t, n_pt - 1), 0)),
            pl.BlockSpec((tm, D), lambda t: (jnp.maximum(t - n_pt, 0), 0)),
        ],
        out_shape=[jax.ShapeDtypeStruct((MP, D), F32), jax.ShapeDtypeStruct((MS, D), F32)],
        scratch_shapes=[pltpu.VMEM((2, 2, tm, D), F32), pltpu.SemaphoreType.DMA((2, 2))],
        compiler_params=_cparams(1),
        name="moe_combine",
    )(pos3, pos3, ys, x, gv, mod, final_w)


def _ssd_kernel(z_ref, xbc_ref, dt_ref, cs_ref, h0_ref, cw_ref, cb_ref, dtb_ref, alog_ref, dsk_ref, nw_ref,
                e2_ref, et2_ref, y_ref, csn_ref, hn_ref, cbuf, hst, *, C, nc):
    c = pl.program_id(1)
    hp = 128 // C
    neg = jnp.float32(-jnp.inf)

    @pl.when(c == 0)
    def _():
        cbuf[5:8, :] = cs_ref[...]
        hst[...] = h0_ref[...]

    cbuf[8:8 + C, :] = xbc_ref[...]

    @pl.when(c == nc - 1)
    def _():
        csn_ref[...] = cbuf[C + 5:C + 8, :]

    dt = _softplus(dt_ref[...] + dtb_ref[...])
    a = -jnp.exp(alog_ref[...])
    row = lax.broadcasted_iota(I32, (C, C), 0)
    col = lax.broadcasted_iota(I32, (C, C), 1)
    tril = jnp.where(row >= col, 1.0, 0.0).astype(BF16)
    b = _dot_exact_rhs(tril, dt * a)
    eb = jnp.exp(b)
    b_last = b[C - 1:C, :]
    dl = dt * jnp.exp(b_last - b)
    stack = jnp.concatenate([dt, eb, dl], axis=0)
    s_hi = stack.astype(BF16)
    s_lo = (stack - s_hi.astype(F32)).astype(BF16)
    lhs2 = jnp.concatenate([s_hi, s_lo], axis=1)
    bst = jnp.concatenate([b] + [pltpu.roll(b, 128 - k, axis=1) for k in range(1, hp)], axis=0)
    bt = bst.T
    rdec = jnp.broadcast_to(jnp.exp(bt[:, C - 1:C]), (128, 128))
    r_hi = rdec.astype(BF16)
    r_lo = (rdec - r_hi.astype(F32)).astype(BF16)
    r2 = jnp.concatenate([r_hi, r_lo], axis=0)

    lane = lax.broadcasted_iota(I32, (C, 128), 1)
    trow = lax.broadcasted_iota(I32, (C, 128), 0)
    causal = trow >= (lane & (C - 1))
    lane_seg = lane >> (C.bit_length() - 1)
    xlane_seg = lax.broadcasted_iota(I32, (C, hp * SSD_P), 1) >> (SSD_P.bit_length() - 1)

    def conv(cols):
        acc = cb_ref[:, cols] + cbuf[5:5 + C, cols] * cw_ref[0:1, cols]
        for k in range(1, SSD_K):
            acc = acc + cbuf[5 + k:5 + k + C, cols] * cw_ref[k:k + 1, cols]
        return _silu(acc)

    for g in range(SSD_G):
        xc = slice(g * SSD_GW, (g + 1) * SSD_GW)
        xg = conv(xc)
        bg = conv(slice(SSD_DI + g * SSD_N, SSD_DI + (g + 1) * SSD_N)).astype(BF16)
        cg = conv(slice(SSD_DI + SSD_G * SSD_N + g * SSD_N, SSD_DI + SSD_G * SSD_N + (g + 1) * SSD_N)).astype(BF16)
        ex = jnp.dot(lhs2, e2_ref[:, xc], preferred_element_type=F32)
        dt_e, eb_e, dl_e = ex[0:C], ex[C:2 * C], ex[2 * C:3 * C]
        xdt = xg * dt_e
        cb2 = lax.dot_general(cg, jnp.concatenate([bg] * hp, axis=0), NT_DIMS, preferred_element_type=F32)
        hg = hst[g * SSD_GW:(g + 1) * SSD_GW, :]
        y_inter = lax.dot_general(cg, hg.astype(BF16), NT_DIMS, preferred_element_type=F32)
        ys = []
        for p in range(8 // hp):
            h0 = g * 8 + p * hp
            bcol = jnp.broadcast_to(b[:, h0:h0 + 1], (C, 128))
            for k in range(1, hp):
                bcol = jnp.where(lane_seg >= k, b[:, h0 + k:h0 + k + 1], bcol)
            lm = jnp.exp(jnp.where(causal, bcol - bt[h0:h0 + 1, :], neg))
            m2 = (cb2 * lm).astype(BF16)
            xp = xdt[:, p * hp * SSD_P:(p + 1) * hp * SSD_P]
            rhs = jnp.concatenate([jnp.where(xlane_seg == k, xp, 0.0) for k in range(hp)], axis=0).astype(BF16)
            ys.append(jnp.dot(m2, rhs, preferred_element_type=F32))
        y = jnp.concatenate(ys, axis=1) + y_inter * eb_e + xg * dsk_ref[:, xc]
        y = y * _silu(z_ref[:, xc])
        ms = jnp.mean(y * y, axis=-1, keepdims=True)
        y_ref[:, xc] = (y * lax.rsqrt(ms + EPS) * nw_ref[:, xc]).astype(y_ref.dtype)
        upd = lax.dot_general((xg * dl_e).astype(BF16), bg, TN_DIMS, preferred_element_type=F32)
        dec = jnp.dot(et2_ref[g * SSD_GW:(g + 1) * SSD_GW, :], r2, preferred_element_type=F32)
        hst[g * SSD_GW:(g + 1) * SSD_GW, :] = hg * dec + upd

    cbuf[5:8, :] = cbuf[C + 5:C + 8, :]

    @pl.when(c == nc - 1)
    def _():
        hn_ref[...] = hst[...]


def _ssd_scan(z, xbc, dtr, conv_state, h0, consts, *, nb, seq, C, row0):
    nc = seq // C
    rb0 = row0 // C
    cw, cb, dtb, alog, dsk, nw, e2, et2 = consts
    rows = lambda b, c: (rb0 + b * nc + c, 0)
    fixed = lambda b, c: (0, 0)
    per_b = lambda b, c: (b, 0, 0)
    return pl.pallas_call(
        functools.partial(_ssd_kernel, C=C, nc=nc),
        grid=(nb, nc),
        in_specs=[
            pl.BlockSpec((C, SSD_DI), rows),
            pl.BlockSpec((C, SSD_XBC), rows),
            pl.BlockSpec((C, 128), rows),
            pl.BlockSpec((None, SSD_K - 1, SSD_XBC), per_b),
            pl.BlockSpec((None, SSD_DI, SSD_N), per_b),
            pl.BlockSpec((SSD_K, SSD_XBC), fixed),
            pl.BlockSpec((1, SSD_XBC), fixed),
            pl.BlockSpec((1, 128), fixed),
            pl.BlockSpec((1, 128), fixed),
            pl.BlockSpec((1, SSD_DI), fixed),
            pl.BlockSpec((1, SSD_DI), fixed),
            pl.BlockSpec((256, SSD_DI), fixed),
            pl.BlockSpec((SSD_DI, 256), fixed),
        ],
        out_specs=[
            pl.BlockSpec((C, SSD_DI), lambda b, c: (b * nc + c, 0)),
            pl.BlockSpec((None, SSD_K - 1, SSD_XBC), per_b),
            pl.BlockSpec((None, SSD_DI, SSD_N), per_b),
        ],
        out_shape=[
            jax.ShapeDtypeStruct((nb * seq, SSD_DI), BF16),
            jax.ShapeDtypeStruct((nb, SSD_K - 1, SSD_XBC), F32),
            jax.ShapeDtypeStruct((nb, SSD_DI, SSD_N), F32),
        ],
        scratch_shapes=[pltpu.VMEM((8 + C, SSD_XBC), F32), pltpu.VMEM((SSD_DI, SSD_N), F32)],
        compiler_params=_cparams(2),
        name=f"ssd_scan_c{C}",
    )(z, xbc, dtr, conv_state, h0, cw, cb, dtb, alog, dsk, nw, e2, et2)


def _gla_kernel(q_ref, f_ref, v_ref, g_ref, s0_ref, lb_ref, nw_ref, o_ref, sn_ref, st, b_s, c_s, *, T, nc):
    c = pl.program_id(1)
    neg = jnp.float32(-jnp.inf)

    @pl.when(c == 0)
    def _():
        for h in range(HG_H):
            st[h * 128:(h + 1) * 128, :] = s0_ref[h * 128:(h + 1) * 128, :].T

    row = lax.broadcasted_iota(I32, (T, T), 0)
    col = lax.broadcasted_iota(I32, (T, T), 1)
    sh = HG_BLK.bit_length() - 1
    same_blk = (row >> sh) == (col >> sh)
    bd = jnp.where(jnp.logical_and(same_blk, row >= col), 1.0, 0.0).astype(BF16)

    for h in range(HG_H):
        hl = slice(h * 128, (h + 1) * 128)
        fz = f_ref[:, hl]
        lb = lb_ref[:, hl]
        ls = jnp.minimum(fz, 0.0) - jnp.log1p(jnp.exp(-jnp.abs(fz)))
        a1 = jnp.log(lb)
        a2 = jnp.log1p(-lb) + ls
        lf = jnp.maximum(a1, a2) + jnp.log1p(jnp.exp(-jnp.abs(a1 - a2)))
        b2 = _dot_exact_rhs(bd, lf * LOG2E)
        b_s[:, hl] = b2
        c_s[:, hl] = b2 - (a2 - fz) * LOG2E

    trow = lax.broadcasted_iota(I32, (8, 128), 0)
    ones = jnp.ones((128, 128), BF16)
    hb = HG_BLK // 2

    def blk(j, carry):
        rows = pl.ds(pl.multiple_of(j * HG_BLK, HG_BLK), HG_BLK)
        for h in range(HG_H):
            hl = slice(h * 128, (h + 1) * 128)
            qb = q_ref[rows, hl]
            bb = b_s[rows, hl]
            cb = c_s[rows, hl]
            vb = v_ref[rows, hl]
            sth = st[hl, :]
            o = lax.dot_general((qb * jnp.exp2(bb)).astype(BF16), sth.astype(BF16), NT_DIMS,
                                preferred_element_type=F32)
            q_h, b_h = (qb[0:hb], qb[hb:]), (bb[0:hb], bb[hb:])
            pieces, owner = [], []
            for s in range(HG_BLK):
                cs = cb[s:s + 1, :]
                for half in range(2):
                    if s >= hb * (half + 1):
                        continue
                    e = b_h[half] - cs
                    if s >= hb * half:
                        e = jnp.where(trow >= s - hb * half, e, neg)
                    pieces.append(q_h[half] * jnp.exp2(e))
                    owner.append((s, half))
            att = jnp.dot(jnp.concatenate(pieces, axis=0).astype(BF16), ones, preferred_element_type=F32)
            o_h = [o[0:hb], o[hb:]]
            for n, (s, half) in enumerate(owner):
                o_h[half] = o_h[half] + att[n * hb:(n + 1) * hb] * vb[s:s + 1, :]
            o = jnp.concatenate(o_h, axis=0)
            b_end = bb[HG_BLK - 1:HG_BLK, :]
            khat = jnp.exp2(b_end - cb).astype(BF16)
            upd = lax.dot_general(vb.astype(BF16), khat, TN_DIMS, preferred_element_type=F32)
            st[hl, :] = sth * jnp.exp2(b_end) + upd
            ms = jnp.mean(o * o, axis=-1, keepdims=True)
            o_ref[rows, hl] = (o * lax.rsqrt(ms + EPS) * nw_ref[...] * _silu(g_ref[rows, hl])).astype(o_ref.dtype)
        return carry

    lax.fori_loop(0, T // HG_BLK, blk, 0)

    @pl.when(c == nc - 1)
    def _():
        for h in range(HG_H):
            sn_ref[h * 128:(h + 1) * 128, :] = st[h * 128:(h + 1) * 128, :].T


def _gla_scan(qfig, s0, lb, nw, *, nb, seq, T, row0):
    nc = seq // T
    rb0 = row0 // T
    per_b = lambda b, c: (b, 0, 0)

    def cols(k):
        return pl.BlockSpec((T, D), lambda b, c: (rb0 + b * nc + c, k))

    return pl.pallas_call(
        functools.partial(_gla_kernel, T=T, nc=nc),
        grid=(nb, nc),
        in_specs=[
            cols(0), cols(1), cols(2), cols(3),
            pl.BlockSpec((None, D, HG_DV), per_b),
            pl.BlockSpec((1, D), lambda b, c: (0, 0)),
            pl.BlockSpec((1, HG_DV), lambda b, c: (0, 0)),
        ],
        out_specs=[
            pl.BlockSpec((T, D), lambda b, c: (b * nc + c, 0)),
            pl.BlockSpec((None, D, HG_DV), per_b),
        ],
        out_shape=[
            jax.ShapeDtypeStruct((nb * seq, D), BF16),
            jax.ShapeDtypeStruct((nb, D, HG_DV), F32),
        ],
        scratch_shapes=[pltpu.VMEM((D, HG_DK), F32), pltpu.VMEM((T, D), F32), pltpu.VMEM((T, D), F32)],
        compiler_params=_cparams(2),
        name=f"gla_scan_t{T}",
    )(qfig, qfig, qfig, qfig, s0, lb, nw)


def _route_plan(top_i):
    e_flat = top_i.reshape(-1)
    onehot = (e_flat[:, None] == jnp.arange(NE, dtype=I32)[None, :]).astype(I32)
    csum = jnp.cumsum(onehot, axis=0)
    rank = jnp.sum(onehot * (csum - 1), axis=1)
    cnt = csum[-1]
    ntile = (cnt + MOE_TM - 1) // MOE_TM
    cum_t = jnp.cumsum(ntile)
    tile0 = cum_t - ntile
    pos = jnp.sum(onehot * tile0[None, :], axis=1) * MOE_TM + rank
    n_used = cum_t[-1]
    t_ids = jnp.arange(MOE_TILES, dtype=I32)
    te = jnp.sum((t_ids[:, None] >= cum_t[None, :]).astype(I32), axis=1)
    te_last = jnp.sum((n_used - 1 >= cum_t).astype(I32))
    te = jnp.where(t_ids < n_used, te, te_last).astype(I32)
    tok = jnp.arange(2 * M, dtype=I32) // 2
    src = jnp.zeros((MOE_ROWS,), I32).at[pos].set(tok)
    return te, n_used.reshape(1).astype(I32), src, pos.reshape(M, 2)


def kernel(x_prompt, x_sample, c_prompt, c_sample, state_ssd_conv, state_ssd, state_hgrn, ada_w, ada_b, norm_w,
           ssd_w_in, ssd_conv_w, ssd_conv_b, ssd_dt_bias, ssd_a_log, ssd_d, ssd_norm_w, ssd_w_out, hgrn_w_in,
           hgrn_lb_logits, hgrn_norm_w, hgrn_w_out, ffn_w_gu, ffn_w_down, moe_router, moe_w_gu, moe_w_down,
           final_norm_w):
    x = (x_prompt.reshape(MP, D), x_sample.reshape(MS, D))
    c_all = jnp.concatenate([c_prompt, jnp.zeros((MOD_S0 - NP, D), F32), c_sample], axis=0)
    norm_w4 = norm_w.reshape(4, 1, D)

    mods = [
        _mm_plain(c_all, ada_w, layer=l, col0=0, n_out=6 * D, tn=1024, tm=MOD_ROWS,
                  bias=ada_b[l].reshape(1, 6 * D), act_lhs=True, name="adaln")
        for l in range(2)
    ]

    hn = _prenorm(x, norm_w4, 0, mods[0], 0, 1, BF16)
    z = _mm_plain(hn, ssd_w_in, layer=0, col0=0, n_out=SSD_DI, tn=1024, tm=1024, name="ssd_in_z")
    xbc = _mm_plain(hn, ssd_w_in, layer=0, col0=SSD_DI, n_out=SSD_XBC, tn=1024, tm=1024, name="ssd_in_xbc")
    w_dt = jnp.pad(ssd_w_in[:, :, SSD_DI + SSD_XBC:], ((0, 0), (0, 0), (0, 128 - SSD_HEADS)))
    dtr = _mm_plain(hn, w_dt, layer=0, col0=0, n_out=128, tn=128, tm=1024, name="ssd_in_dt")

    pad64 = lambda v: jnp.pad(v.reshape(1, SSD_HEADS), ((0, 0), (0, 128 - SSD_HEADS)))
    head_of = jnp.arange(SSD_DI, dtype=I32) // SSD_P
    e1 = (jnp.arange(128, dtype=I32)[:, None] == head_of[None, :]).astype(BF16)
    e2 = jnp.concatenate([e1, e1], axis=0)
    et2 = jnp.concatenate([e1.T, e1.T], axis=1)
    consts = (ssd_conv_w[0], ssd_conv_b[0].reshape(1, SSD_XBC), pad64(ssd_dt_bias[0]), pad64(ssd_a_log[0]),
              jnp.repeat(ssd_d[0], SSD_P).reshape(1, SSD_DI), ssd_norm_w[0].reshape(1, SSD_DI), e2, et2)
    yp, p_conv, p_ssd = _ssd_scan(z, xbc, dtr, jnp.zeros((NP, SSD_K - 1, SSD_XBC), F32),
                                  jnp.zeros((NP, SSD_DI, SSD_N), F32), consts, nb=NP, seq=LP, C=64, row0=0)
    ysm, s_conv, s_ssd = _ssd_scan(z, xbc, dtr, state_ssd_conv[0], state_ssd[0].reshape(NS, SSD_DI, SSD_N), consts,
                                   nb=NS, seq=LS, C=32, row0=MP)
    x = _mm_resid((yp, ysm), ssd_w_out, x, mods[0], 2, k=SSD_DI, tn=512, tm=512, name="ssd_out")

    hn = _prenorm(x, norm_w4, 1, mods[0], 3, 4, BF16)
    act = _mm_swiglu(hn, ffn_w_gu, FFN, tn=512, tm=1024, name="ffn_gu")
    x = _mm_resid(act, ffn_w_down, x, mods[0], 5, k=FFN, tn=512, tm=512, name="ffn_down")

    hn = _prenorm(x, norm_w4, 2, mods[1], 0, 1, BF16)
    qfig = _mm_plain(hn, hgrn_w_in, layer=0, col0=0, n_out=4 * D, tn=1024, tm=1024, name="hgrn_in")
    p = jax.nn.softmax(hgrn_lb_logits.astype(F32), axis=0)
    lb = (jnp.cumsum(p, axis=0) - p[0])[1].reshape(1, D)
    nw_h = hgrn_norm_w[0].reshape(1, HG_DV)
    op, p_hgrn = _gla_scan(qfig, jnp.zeros((NP, D, HG_DV), F32), lb, nw_h, nb=NP, seq=LP, T=64, row0=0)
    osm, s_hgrn = _gla_scan(qfig, state_hgrn[0].reshape(NS, D, HG_DV), lb, nw_h, nb=NS, seq=LS, T=32, row0=MP)
    x = _mm_resid((op, osm), hgrn_w_out, x, mods[1], 2, k=D, tn=1024, tm=512, name="hgrn_out")

    hn32 = _prenorm(x, norm_w4, 3, mods[1], 3, 4, F32)
    gv, gi = _router(hn32, jnp.pad(moe_router[0], ((0, 0), (0, 128 - NE))))
    te, nu, src, pos = _route_plan(gi[:, :2])
    xs = _gather_rows(src.reshape(MOE_TILES, 1, MOE_TM), hn32)
    act = _moe_gu(te, nu, xs, moe_w_gu[0])
    ys = _moe_down(te, nu, act, moe_w_down[0])
    tmc = 256
    pos3 = jnp.concatenate([pos[:, 0].reshape(M // tmc, 1, tmc), pos[:, 1].reshape(M // tmc, 1, tmc)], axis=2)
    yo_p, yo_s = _combine(pos3, ys, x, gv, mods[1], 5, final_norm_w.reshape(1, D), tm=tmc)

    return (
        yo_p.reshape(NP, LP, D),
        yo_s.reshape(NS, LS, D),
        p_conv[None],
        p_ssd.reshape(1, NP, SSD_HEADS, SSD_P, SSD_N),
        p_hgrn.reshape(1, NP, HG_H, HG_DK, HG_DV),
        s_conv[None],
        s_ssd.reshape(1, NS, SSD_HEADS, SSD_P, SSD_N),
        s_hgrn.reshape(1, NS, HG_H, HG_DK, HG_DV),
    )
```

```python
import functools

import jax
import jax.numpy as jnp
from jax import lax
from jax.experimental import pallas as pl
from jax.experimental.pallas import tpu as pltpu

F32 = jnp.float32
BF16 = jnp.bfloat16
I32 = jnp.int32

EPS = 1e-6
LOG2E = 1.4426950408889634
D = 2048
NP, LP = 4, 2048
NS, LS = 32, 32
MP, MS = NP * LP, NS * LS
M = MP + MS
MOD_S0 = 8
MOD_ROWS = MOD_S0 + NS

SSD_DI = 4096
SSD_HEADS = 64
SSD_P = 64
SSD_G = 8
SSD_GW = SSD_DI // SSD_G
SSD_N = 128
SSD_XBC = SSD_DI + 2 * SSD_G * SSD_N
SSD_K = 4

HG_H = 16
HG_DK = 128
HG_DV = 128
HG_BLK = 16

FFN = 5632
NE = 8
EH = 7168
MOE_TM = 512
MOE_TILES = (2 * M) // MOE_TM + NE
MOE_ROWS = MOE_TILES * MOE_TM

VMEM_LIMIT = 56 * 1024 * 1024

NT_DIMS = (((1,), (1,)), ((), ()))
TN_DIMS = (((0,), (0,)), ((), ()))


def _cparams(n_axes):
    return pltpu.CompilerParams(dimension_semantics=("arbitrary",) * n_axes,
                                vmem_limit_bytes=VMEM_LIMIT)


def _silu(x):
    h = 0.5 * x
    return h + h * jnp.tanh(h)


def _softplus(x):
    return jnp.maximum(x, 0.0) + jnp.log1p(jnp.exp(-jnp.abs(x)))


def _split3(x):
    hi = x.astype(BF16)
    r1 = x - hi.astype(F32)
    mid = r1.astype(BF16)
    lo = (r1 - mid.astype(F32)).astype(BF16)
    return hi, mid, lo


def _dot_exact_rhs(mat_bf16, x):
    hi, mid, lo = _split3(x)
    acc = jnp.dot(mat_bf16, hi, preferred_element_type=F32)
    acc = acc + jnp.dot(mat_bf16, mid, preferred_element_type=F32)
    return acc + jnp.dot(mat_bf16, lo, preferred_element_type=F32)


def _grouped_apply(i, tm, mod_refs, fn):
    n_pt = MP // tm
    tpb = LP // tm

    @pl.when(i < n_pt)
    def _():
        r = i // tpb
        fn([m[pl.ds(r, 1), :][None] for m in mod_refs], 1, 0)

    @pl.when(i >= n_pt)
    def _():
        g = tm // LS
        start = pl.multiple_of(MOD_S0 + (i - n_pt) * g, 8)
        fn([m[pl.ds(start, g), :][:, None, :] for m in mod_refs], g, 1)


def _row_operand(op, tm, width, ij):
    if not isinstance(op, tuple):
        return [pl.BlockSpec((tm, width), lambda *g: ij(*g))], [op]
    n_pt = MP // tm
    return [
        pl.BlockSpec((tm, width), lambda *g: (jnp.minimum(ij(*g)[0], n_pt - 1), ij(*g)[1])),
        pl.BlockSpec((tm, width), lambda *g: (jnp.maximum(ij(*g)[0] - n_pt, 0), ij(*g)[1])),
    ], list(op)


def _prenorm_kernel(*refs, tm, nx):
    x_refs = refs[:nx]
    nw_ref, sc_ref, sh_ref, o_ref = refs[nx:]
    i = pl.program_id(0)

    def fn(mods, g, s):
        sc, sh = mods
        x = x_refs[s % nx][...]
        ms = jnp.mean(x * x, axis=-1, keepdims=True)
        y = x * lax.rsqrt(ms + EPS) * nw_ref[...]
        y = y.reshape(g, tm // g, D) * (1.0 + sc) + sh
        o_ref[...] = y.reshape(tm, D).astype(o_ref.dtype)

    _grouped_apply(i, tm, [sc_ref, sh_ref], fn)


def _prenorm(x, norm_w4, k, mod, sh_blk, sc_blk, out_dtype, tm=512):
    x_specs, x_args = _row_operand(x, tm, D, lambda i: (i, 0))
    return pl.pallas_call(
        functools.partial(_prenorm_kernel, tm=tm, nx=len(x_args)),
        grid=(M // tm,),
        in_specs=x_specs + [
            pl.BlockSpec((None, 1, D), lambda i: (k, 0, 0)),
            pl.BlockSpec((MOD_ROWS, D), lambda i: (0, sc_blk)),
            pl.BlockSpec((MOD_ROWS, D), lambda i: (0, sh_blk)),
        ],
        out_specs=pl.BlockSpec((tm, D), lambda i: (i, 0)),
        out_shape=jax.ShapeDtypeStruct((M, D), out_dtype),
        compiler_params=_cparams(1),
        name="prenorm",
    )(*x_args, norm_w4, mod, mod)


def _mm_plain_kernel(*refs, act_lhs, has_bias, valid_cols):
    if has_bias:
        lhs_ref, w_ref, b_ref, o_ref, wb = refs
    else:
        lhs_ref, w_ref, o_ref, wb = refs

    @pl.when(pl.program_id(1) == 0)
    def _():
        w = w_ref[...]
        if valid_cols is not None:
            w = jnp.where(lax.broadcasted_iota(I32, w.shape, 1) < valid_cols, w, 0.0)
        wb[...] = w.astype(BF16)

    lhs = lhs_ref[...]
    if act_lhs:
        lhs = _silu(lhs).astype(BF16)
    acc = jnp.dot(lhs, wb[...], preferred_element_type=F32)
    if has_bias:
        acc = acc + b_ref[...]
    o_ref[...] = acc.astype(o_ref.dtype)


def _mm_plain(lhs, w3, *, layer, col0, n_out, tn, tm, out_dtype=F32, bias=None, act_lhs=False, name="mm"):
    rows, k = lhs.shape
    off = col0 // tn
    w_cols = w3.shape[2] - col0
    valid_cols = w_cols if w_cols < n_out else None
    in_specs = [
        pl.BlockSpec((tm, k), lambda j, i: (i, 0)),
        pl.BlockSpec((None, k, tn), lambda j, i: (layer, 0, j + off)),
    ]
    args = [lhs, w3]
    if bias is not None:
        in_specs.append(pl.BlockSpec((1, tn), lambda j, i: (0, j)))
        args.append(bias)
    return pl.pallas_call(
        functools.partial(_mm_plain_kernel, act_lhs=act_lhs, has_bias=bias is not None, valid_cols=valid_cols),
        grid=(n_out // tn, rows // tm),
        in_specs=in_specs,
        out_specs=pl.BlockSpec((tm, tn), lambda j, i: (i, j)),
        out_shape=jax.ShapeDtypeStruct((rows, n_out), out_dtype),
        scratch_shapes=[pltpu.VMEM((k, tn), BF16)],
        compiler_params=_cparams(2),
        name=name,
    )(*args)


def _mm_resid_kernel(*refs, tm, tn, nl, nr):
    lhs_refs = refs[:nl]
    w_ref = refs[nl]
    res_refs = refs[nl + 1:nl + 1 + nr]
    gate_ref, o_ref, wb = refs[nl + 1 + nr:]
    i = pl.program_id(1)

    @pl.when(i == 0)
    def _():
        wb[...] = w_ref[...].astype(BF16)

    def fn(mods, g, s):
        (gate,) = mods
        acc = jnp.dot(lhs_refs[s % nl][...], wb[...], preferred_element_type=F32)
        out = res_refs[s % nr][...].reshape(g, tm // g, tn) + gate * acc.reshape(g, tm // g, tn)
        o_ref[...] = out.reshape(tm, tn)

    _grouped_apply(i, tm, [gate_ref], fn)


def _mm_resid(lhs, w3, res, mod, gate_blk, *, k, tn, tm, name):
    goff = gate_blk * (D // tn)
    l_specs, l_args = _row_operand(lhs, tm, k, lambda j, i: (i, 0))
    r_specs, r_args = _row_operand(res, tm, tn, lambda j, i: (i, j))
    return pl.pallas_call(
        functools.partial(_mm_resid_kernel, tm=tm, tn=tn, nl=len(l_args), nr=len(r_args)),
        grid=(D // tn, M // tm),
        in_specs=l_specs + [pl.BlockSpec((None, k, tn), lambda j, i: (0, 0, j))] + r_specs + [
            pl.BlockSpec((MOD_ROWS, tn), lambda j, i: (0, goff + j)),
        ],
        out_specs=pl.BlockSpec((tm, tn), lambda j, i: (i, j)),
        out_shape=jax.ShapeDtypeStruct((M, D), F32),
        scratch_shapes=[pltpu.VMEM((k, tn), BF16)],
        compiler_params=_cparams(2),
        name=name,
    )(*l_args, w3, *r_args, mod)


def _mm_swiglu_kernel(lhs_ref, wg_ref, wu_ref, o_ref, wgb, wub):
    @pl.when(pl.program_id(1) == 0)
    def _():
        wgb[...] = wg_ref[...].astype(BF16)
        wub[...] = wu_ref[...].astype(BF16)

    lhs = lhs_ref[...]
    gt = jnp.dot(lhs, wgb[...], preferred_element_type=F32)
    up = jnp.dot(lhs, wub[...], preferred_element_type=F32)
    o_ref[...] = (_silu(gt) * up).astype(o_ref.dtype)


def _mm_swiglu(lhs, w3, hidden, *, tn, tm, name):
    rows, k = lhs.shape
    nb = hidden // tn
    return pl.pallas_call(
        _mm_swiglu_kernel,
        grid=(nb, rows // tm),
        in_specs=[
            pl.BlockSpec((tm, k), lambda j, i: (i, 0)),
            pl.BlockSpec((None, k, tn), lambda j, i: (0, 0, j)),
            pl.BlockSpec((None, k, tn), lambda j, i: (0, 0, j + nb)),
        ],
        out_specs=pl.BlockSpec((tm, tn), lambda j, i: (i, j)),
        out_shape=jax.ShapeDtypeStruct((rows, hidden), BF16),
        scratch_shapes=[pltpu.VMEM((k, tn), BF16), pltpu.VMEM((k, tn), BF16)],
        compiler_params=_cparams(2),
        name=name,
    )(lhs, w3, w3)


def _expert_changed(te_ref, i):
    prev = te_ref[jnp.maximum(i - 1, 0)]
    return jnp.logical_or(i == 0, te_ref[i] != prev)


def _moe_gu_kernel(te_ref, nu_ref, lhs_ref, wg_ref, wu_ref, o_ref, wgb, wub):
    i = pl.program_id(1)

    @pl.when(_expert_changed(te_ref, i))
    def _():
        wgb[...] = wg_ref[...].astype(BF16)
        wub[...] = wu_ref[...].astype(BF16)

    @pl.when(i < nu_ref[0])
    def _():
        lhs = lhs_ref[...]
        gt = jnp.dot(lhs, wgb[...], preferred_element_type=F32)
        up = jnp.dot(lhs, wub[...], preferred_element_type=F32)
        o_ref[...] = (_silu(gt) * up).astype(o_ref.dtype)

    @pl.when(i >= nu_ref[0])
    def _():
        o_ref[...] = jnp.zeros_like(o_ref)


def _moe_gu(te, nu, xs, w_gu, *, tn=1024):
    tm = MOE_TM
    nb = EH // tn
    grid_spec = pltpu.PrefetchScalarGridSpec(
        num_scalar_prefetch=2,
        grid=(nb, MOE_TILES),
        in_specs=[
            pl.BlockSpec((tm, D), lambda j, i, te, nu: (i, 0)),
            pl.BlockSpec((None, D, tn), lambda j, i, te, nu: (te[i], 0, j)),
            pl.BlockSpec((None, D, tn), lambda j, i, te, nu: (te[i], 0, j + nb)),
        ],
        out_specs=pl.BlockSpec((tm, tn), lambda j, i, te, nu: (i, j)),
        scratch_shapes=[pltpu.VMEM((D, tn), BF16), pltpu.VMEM((D, tn), BF16)],
    )
    return pl.pallas_call(
        _moe_gu_kernel,
        grid_spec=grid_spec,
        out_shape=jax.ShapeDtypeStruct((MOE_ROWS, EH), BF16),
        compiler_params=_cparams(2),
        name="moe_gu",
    )(te, nu, xs, w_gu, w_gu)


def _moe_down_kernel(te_ref, nu_ref, lhs_ref, w_ref, o_ref, wb):
    i = pl.program_id(1)

    @pl.when(_expert_changed(te_ref, i))
    def _():
        wb[...] = w_ref[...].astype(BF16)

    @pl.when(i < nu_ref[0])
    def _():
        o_ref[...] = jnp.dot(lhs_ref[...], wb[...], preferred_element_type=F32)

    @pl.when(i >= nu_ref[0])
    def _():
        o_ref[...] = jnp.zeros_like(o_ref)


def _moe_down(te, nu, act, w_down, *, tn=512):
    tm = MOE_TM
    grid_spec = pltpu.PrefetchScalarGridSpec(
        num_scalar_prefetch=2,
        grid=(D // tn, MOE_TILES),
        in_specs=[
            pl.BlockSpec((tm, EH), lambda j, i, te, nu: (i, 0)),
            pl.BlockSpec((None, EH, tn), lambda j, i, te, nu: (te[i], 0, j)),
        ],
        out_specs=pl.BlockSpec((tm, tn), lambda j, i, te, nu: (i, j)),
        scratch_shapes=[pltpu.VMEM((EH, tn), BF16)],
    )
    return pl.pallas_call(
        _moe_down_kernel,
        grid_spec=grid_spec,
        out_shape=jax.ShapeDtypeStruct((MOE_ROWS, D), F32),
        compiler_params=_cparams(2),
        name="moe_down",
    )(te, nu, act, w_down)


def _router_kernel(h_ref, r_ref, gv_ref, gi_ref):
    logits = jnp.dot(h_ref[...], r_ref[...], preferred_element_type=F32, precision=lax.Precision.HIGHEST)
    lane = lax.broadcasted_iota(I32, logits.shape, 1)
    neg = jnp.float32(-jnp.inf)
    l1 = jnp.where(lane < NE, logits, neg)
    m1 = jnp.max(l1, axis=-1, keepdims=True)
    i1 = jnp.min(jnp.where(l1 == m1, lane, 128), axis=-1, keepdims=True)
    l2 = jnp.where(lane == i1, neg, l1)
    m2 = jnp.max(l2, axis=-1, keepdims=True)
    i2 = jnp.min(jnp.where(l2 == m2, lane, 128), axis=-1, keepdims=True)
    e = jnp.exp(m2 - m1)
    g1 = 1.0 / (1.0 + e)
    g2 = e * g1
    gv_ref[...] = jnp.where(lane == 0, g1, jnp.where(lane == 1, g2, 0.0))
    gi_ref[...] = jnp.where(lane == 0, i1, jnp.where(lane == 1, i2, 0))


def _router(hn, router_pad, tm=512):
    return pl.pallas_call(
        _router_kernel,
        grid=(M // tm,),
        in_specs=[pl.BlockSpec((tm, D), lambda i: (i, 0)), pl.BlockSpec((D, 128), lambda i: (0, 0))],
        out_specs=[pl.BlockSpec((tm, 128), lambda i: (i, 0)), pl.BlockSpec((tm, 128), lambda i: (i, 0))],
        out_shape=[jax.ShapeDtypeStruct((M, 128), F32), jax.ShapeDtypeStruct((M, 128), I32)],
        compiler_params=_cparams(1),
        name="router",
    )(hn, router_pad)


def _row_copy(src_hbm, row, buf, r, sem):
    return pltpu.make_async_copy(src_hbm.at[pl.ds(row, 1), :], buf.at[pl.ds(r, 1), :], sem)


def _start_rows(src_hbm, idx_ref, off, dst, sem, n):
    def body(p, carry):
        for q in range(2):
            r = 2 * p + q
            _row_copy(src_hbm, idx_ref[0, 0, off + r], dst, r, sem).start(priority=q)
        return carry

    lax.fori_loop(0, n // 2, body, 0, unroll=4)


def _wait_rows(src_hbm, dst, sem, n):
    pltpu.make_async_copy(src_hbm.at[pl.ds(0, n), :], dst, sem).wait()


def _gather_kernel(nu_ref, idx_ref, idx_next_ref, src_hbm, o_ref, buf, sem, *, tm):
    t = pl.program_id(0)
    nu = nu_ref[0]
    slot = lax.rem(t, 2)

    @pl.when(t == 0)
    def _():
        _start_rows(src_hbm, idx_ref, 0, buf.at[0], sem.at[0], tm)

    @pl.when(t + 1 < nu)
    def _():
        _start_rows(src_hbm, idx_next_ref, 0, buf.at[1 - slot], sem.at[1 - slot], tm)

    @pl.when(t < nu)
    def _():
        _wait_rows(src_hbm, buf.at[slot], sem.at[slot], tm)
        o_ref[...] = buf[slot].astype(o_ref.dtype)

    @pl.when(t >= nu)
    def _():
        o_ref[...] = jnp.zeros_like(o_ref)


def _gather_rows(nu, idx3, src):
    tm = MOE_TM
    nt = MOE_TILES
    grid_spec = pltpu.PrefetchScalarGridSpec(
        num_scalar_prefetch=1,
        grid=(nt,),
        in_specs=[
            pl.BlockSpec((1, 1, tm), lambda t, nu: (t, 0, 0), memory_space=pltpu.SMEM),
            pl.BlockSpec((1, 1, tm), lambda t, nu: (jnp.minimum(t + 1, nt - 1), 0, 0), memory_space=pltpu.SMEM),
            pl.BlockSpec(memory_space=pl.ANY),
        ],
        out_specs=pl.BlockSpec((tm, D), lambda t, nu: (t, 0)),
        scratch_shapes=[pltpu.VMEM((2, tm, D), F32), pltpu.SemaphoreType.DMA((2,))],
    )
    return pl.pallas_call(
        functools.partial(_gather_kernel, tm=tm),
        grid_spec=grid_spec,
        out_shape=jax.ShapeDtypeStruct((MOE_ROWS, D), BF16),
        compiler_params=_cparams(1),
        name="moe_gather",
    )(nu, idx3, idx3, src)


def _combine_kernel(pos_ref, pos_next_ref, ys_hbm, x_ref, gv_ref, gate_ref, fw_ref, op_ref, os_ref, buf, sem,
                    *, tm, nt):
    t = pl.program_id(0)
    slot = lax.rem(t, 2)

    def start(idx_ref, s):
        _start_rows(ys_hbm, idx_ref, 0, buf.at[s, 0], sem.at[s, 0], tm)
        _start_rows(ys_hbm, idx_ref, tm, buf.at[s, 1], sem.at[s, 1], tm)

    @pl.when(t == 0)
    def _():
        start(pos_ref, 0)

    @pl.when(t + 1 < nt)
    def _():
        start(pos_next_ref, 1 - slot)

    _wait_rows(ys_hbm, buf.at[slot, 0], sem.at[slot, 0], tm)
    _wait_rows(ys_hbm, buf.at[slot, 1], sem.at[slot, 1], tm)

    def fn(mods, g, s):
        (gate,) = mods
        gv = gv_ref[...]
        f = gv[:, 0:1] * buf[slot, 0] + gv[:, 1:2] * buf[slot, 1]
        x = x_ref[...].reshape(g, tm // g, D) + gate * f.reshape(g, tm // g, D)
        x = x.reshape(tm, D)
        ms = jnp.mean(x * x, axis=-1, keepdims=True)
        (op_ref, os_ref)[s][...] = x * lax.rsqrt(ms + EPS) * fw_ref[...]

    _grouped_apply(t, tm, [gate_ref], fn)


def _combine(pos3, ys, x, gv, mod, gate_blk, final_w, tm=256):
    nt = M // tm
    n_pt = MP // tm
    return pl.pallas_call(
        functools.partial(_combine_kernel, tm=tm, nt=nt),
        grid=(nt,),
        in_specs=[
            pl.BlockSpec((1, 1, 2 * tm), lambda t: (t, 0, 0), memory_space=pltpu.SMEM),
            pl.BlockSpec((1, 1, 2 * tm), lambda t: (jnp.minimum(t + 1, nt - 1), 0, 0), memory_space=pltpu.SMEM),
            pl.BlockSpec(memory_space=pl.ANY),
            pl.BlockSpec((tm, D), lambda t: (t, 0)),
            pl.BlockSpec((tm, 128), lambda t: (t, 0)),
            pl.BlockSpec((MOD_ROWS, D), lambda t: (0, gate_blk)),
            pl.BlockSpec((1, D), lambda t: (0, 0)),
        ],
        out_specs=[
            pl.BlockSpec((tm, D), lambda t: (jnp.minimum(t, n_pt - 1), 0)),
            pl.BlockSpec((tm, D), lambda t: (jnp.maximum(t - n_pt, 0), 0)),
        ],
        out_shape=[jax.ShapeDtypeStruct((MP, D), F32), jax.ShapeDtypeStruct((MS, D), F32)],
        scratch_shapes=[pltpu.VMEM((2, 2, tm, D), F32), pltpu.SemaphoreType.DMA((2, 2))],
        compiler_params=_cparams(1),
        name="moe_combine",
    )(pos3, pos3, ys, x, gv, mod, final_w)


def _ssd_kernel(z_ref, xbc_ref, dt_ref, cs_ref, h0_ref, cw_ref, cb_ref, dtb_ref, alog_ref, dsk_ref, nw_ref,
                e2_ref, et2_ref, y_ref, csn_ref, hn_ref, cbuf, hst, *, C, nc):
    c = pl.program_id(1)
    hp = 128 // C
    neg = jnp.float32(-jnp.inf)

    @pl.when(c == 0)
    def _():
        cbuf[0:8, :] = jnp.zeros((8, SSD_XBC), F32)
        cbuf[5:8, :] = cs_ref[...]
        hst[...] = h0_ref[...]

    cbuf[8:8 + C, :] = xbc_ref[...]

    @pl.when(c == nc - 1)
    def _():
        csn_ref[...] = cbuf[C + 5:C + 8, :]

    dt = _softplus(dt_ref[...] + dtb_ref[...])
    a = -jnp.exp(alog_ref[...]) * LOG2E
    row = lax.broadcasted_iota(I32, (C, C), 0)
    col = lax.broadcasted_iota(I32, (C, C), 1)
    tril = jnp.where(row >= col, 1.0, 0.0).astype(BF16)
    b = _dot_exact_rhs(tril, dt * a)
    eb = jnp.exp2(b)
    b_last = b[C - 1:C, :]
    dl = dt * jnp.exp2(b_last - b)
    stack = jnp.concatenate([dt, eb, dl], axis=0)
    s_hi = stack.astype(BF16)
    s_lo = (stack - s_hi.astype(F32)).astype(BF16)
    lhs2 = jnp.concatenate([s_hi, s_lo], axis=1)
    bst = jnp.concatenate([b] + [pltpu.roll(b, 128 - k, axis=1) for k in range(1, hp)], axis=0)
    bt = bst.T
    rdec = jnp.broadcast_to(jnp.exp2(bt[:, C - 1:C]), (128, 128))
    r_hi = rdec.astype(BF16)
    r_lo = (rdec - r_hi.astype(F32)).astype(BF16)
    r2 = jnp.concatenate([r_hi, r_lo], axis=0)

    lane = lax.broadcasted_iota(I32, (C, 128), 1)
    trow = lax.broadcasted_iota(I32, (C, 128), 0)
    causal = trow >= (lane & (C - 1))
    lane_seg = lane >> (C.bit_length() - 1)
    xlane_seg = lax.broadcasted_iota(I32, (C, hp * SSD_P), 1) >> (SSD_P.bit_length() - 1)

    def conv(cols):
        v = cbuf[0:C + 8, cols]
        acc = cb_ref[:, cols] + v[8:8 + C] * cw_ref[SSD_K - 1:SSD_K, cols]
        for k in range(SSD_K - 1):
            acc = acc + pltpu.roll(v, C + 3 - k, axis=0)[0:C] * cw_ref[k:k + 1, cols]
        return _silu(acc)

    for g in range(SSD_G):
        xc = slice(g * SSD_GW, (g + 1) * SSD_GW)
        xg = conv(xc)
        bg = conv(slice(SSD_DI + g * SSD_N, SSD_DI + (g + 1) * SSD_N)).astype(BF16)
        cg = conv(slice(SSD_DI + SSD_G * SSD_N + g * SSD_N, SSD_DI + SSD_G * SSD_N + (g + 1) * SSD_N)).astype(BF16)
        ex = jnp.dot(lhs2, e2_ref[:, xc], preferred_element_type=F32)
        dt_e, eb_e, dl_e = ex[0:C], ex[C:2 * C], ex[2 * C:3 * C]
        xdt = xg * dt_e
        cb2 = lax.dot_general(cg, jnp.concatenate([bg] * hp, axis=0), NT_DIMS, preferred_element_type=F32)
        hg = hst[g * SSD_GW:(g + 1) * SSD_GW, :]
        y_inter = lax.dot_general(cg, hg.astype(BF16), NT_DIMS, preferred_element_type=F32)
        ys = []
        for p in range(8 // hp):
            h0 = g * 8 + p * hp
            bcol = jnp.broadcast_to(b[:, h0:h0 + 1], (C, 128))
            for k in range(1, hp):
                bcol = jnp.where(lane_seg >= k, b[:, h0 + k:h0 + k + 1], bcol)
            lm = jnp.exp2(jnp.where(causal, bcol - bt[h0:h0 + 1, :], neg))
            m2 = (cb2 * lm).astype(BF16)
            xp = xdt[:, p * hp * SSD_P:(p + 1) * hp * SSD_P]
            rhs = jnp.concatenate([jnp.where(xlane_seg == k, xp, 0.0) for k in range(hp)], axis=0).astype(BF16)
            ys.append(jnp.dot(m2, rhs, preferred_element_type=F32))
        y = jnp.concatenate(ys, axis=1) + y_inter * eb_e + xg * dsk_ref[:, xc]
        y = y * _silu(z_ref[:, xc])
        ms = jnp.mean(y * y, axis=-1, keepdims=True)
        y_ref[:, xc] = (y * lax.rsqrt(ms + EPS) * nw_ref[:, xc]).astype(y_ref.dtype)
        upd = lax.dot_general((xg * dl_e).astype(BF16), bg, TN_DIMS, preferred_element_type=F32)
        dec = jnp.dot(et2_ref[g * SSD_GW:(g + 1) * SSD_GW, :], r2, preferred_element_type=F32)
        hst[g * SSD_GW:(g + 1) * SSD_GW, :] = hg * dec + upd

    cbuf[5:8, :] = cbuf[C + 5:C + 8, :]

    @pl.when(c == nc - 1)
    def _():
        hn_ref[...] = hst[...]


def _ssd_scan(z, xbc, dtr, conv_state, h0, consts, *, nb, seq, C, row0):
    nc = seq // C
    rb0 = row0 // C
    cw, cb, dtb, alog, dsk, nw, e2, et2 = consts
    rows = lambda b, c: (rb0 + b * nc + c, 0)
    fixed = lambda b, c: (0, 0)
    per_b = lambda b, c: (b, 0, 0)
    return pl.pallas_call(
        functools.partial(_ssd_kernel, C=C, nc=nc),
        grid=(nb, nc),
        in_specs=[
            pl.BlockSpec((C, SSD_DI), rows),
            pl.BlockSpec((C, SSD_XBC), rows),
            pl.BlockSpec((C, 128), rows),
            pl.BlockSpec((None, SSD_K - 1, SSD_XBC), per_b),
            pl.BlockSpec((None, SSD_DI, SSD_N), per_b),
            pl.BlockSpec((SSD_K, SSD_XBC), fixed),
            pl.BlockSpec((1, SSD_XBC), fixed),
            pl.BlockSpec((1, 128), fixed),
            pl.BlockSpec((1, 128), fixed),
            pl.BlockSpec((1, SSD_DI), fixed),
            pl.BlockSpec((1, SSD_DI), fixed),
            pl.BlockSpec((256, SSD_DI), fixed),
            pl.BlockSpec((SSD_DI, 256), fixed),
        ],
        out_specs=[
            pl.BlockSpec((C, SSD_DI), lambda b, c: (b * nc + c, 0)),
            pl.BlockSpec((None, SSD_K - 1, SSD_XBC), per_b),
            pl.BlockSpec((None, SSD_DI, SSD_N), per_b),
        ],
        out_shape=[
            jax.ShapeDtypeStruct((nb * seq, SSD_DI), BF16),
            jax.ShapeDtypeStruct((nb, SSD_K - 1, SSD_XBC), F32),
            jax.ShapeDtypeStruct((nb, SSD_DI, SSD_N), F32),
        ],
        scratch_shapes=[pltpu.VMEM((8 + C, SSD_XBC), F32), pltpu.VMEM((SSD_DI, SSD_N), F32)],
        compiler_params=_cparams(2),
        name=f"ssd_scan_c{C}",
    )(z, xbc, dtr, conv_state, h0, cw, cb, dtb, alog, dsk, nw, e2, et2)


def _gla_kernel(q_ref, f_ref, v_ref, g_ref, s0_ref, lb_ref, nw_ref, o_ref, sn_ref, st, b_s, c_s, *, T, nc):
    c = pl.program_id(1)
    neg = jnp.float32(-jnp.inf)

    @pl.when(c == 0)
    def _():
        for h in range(HG_H):
            st[h * 128:(h + 1) * 128, :] = s0_ref[h * 128:(h + 1) * 128, :].T

    row = lax.broadcasted_iota(I32, (T, T), 0)
    col = lax.broadcasted_iota(I32, (T, T), 1)
    sh = HG_BLK.bit_length() - 1
    same_blk = (row >> sh) == (col >> sh)
    bd = jnp.where(jnp.logical_and(same_blk, row >= col), 1.0, 0.0).astype(BF16)

    for h in range(HG_H):
        hl = slice(h * 128, (h + 1) * 128)
        fz = f_ref[:, hl]
        lb = lb_ref[:, hl]
        ls = jnp.minimum(fz, 0.0) - jnp.log1p(jnp.exp(-jnp.abs(fz)))
        a1 = jnp.log(lb)
        a2 = jnp.log1p(-lb) + ls
        lf = jnp.maximum(a1, a2) + jnp.log1p(jnp.exp(-jnp.abs(a1 - a2)))
        b2 = _dot_exact_rhs(bd, lf * LOG2E)
        b_s[:, hl] = b2
        c_s[:, hl] = b2 - (a2 - fz) * LOG2E

    trow = lax.broadcasted_iota(I32, (8, 128), 0)
    ones = jnp.ones((128, 128), BF16)
    hb = HG_BLK // 2

    def blk(j, carry):
        rows = pl.ds(pl.multiple_of(j * HG_BLK, HG_BLK), HG_BLK)
        for h in range(HG_H):
            hl = slice(h * 128, (h + 1) * 128)
            qb = q_ref[rows, hl]
            bb = b_s[rows, hl]
            cb = c_s[rows, hl]
            vb = v_ref[rows, hl]
            sth = st[hl, :]
            o = lax.dot_general((qb * jnp.exp2(bb)).astype(BF16), sth.astype(BF16), NT_DIMS,
                                preferred_element_type=F32)
            q_h, b_h = (qb[0:hb], qb[hb:]), (bb[0:hb], bb[hb:])
            pieces, owner = [], []
            for s in range(HG_BLK):
                cs = cb[s:s + 1, :]
                for half in range(2):
                    if s >= hb * (half + 1):
                        continue
                    e = b_h[half] - cs
                    if s >= hb * half:
                        e = jnp.where(trow >= s - hb * half, e, neg)
                    pieces.append(q_h[half] * jnp.exp2(e))
                    owner.append((s, half))
            att = jnp.dot(jnp.concatenate(pieces, axis=0).astype(BF16), ones, preferred_element_type=F32)
            o_h = [o[0:hb], o[hb:]]
            for n, (s, half) in enumerate(owner):
                o_h[half] = o_h[half] + att[n * hb:(n + 1) * hb] * vb[s:s + 1, :]
            o = jnp.concatenate(o_h, axis=0)
            b_end = bb[HG_BLK - 1:HG_BLK, :]
            khat = jnp.exp2(b_end - cb).astype(BF16)
            upd = lax.dot_general(vb.astype(BF16), khat, TN_DIMS, preferred_element_type=F32)
            st[hl, :] = sth * jnp.exp2(b_end) + upd
            ms = jnp.mean(o * o, axis=-1, keepdims=True)
            o_ref[rows, hl] = (o * lax.rsqrt(ms + EPS) * nw_ref[...] * _silu(g_ref[rows, hl])).astype(o_ref.dtype)
        return carry

    lax.fori_loop(0, T // HG_BLK, blk, 0)

    @pl.when(c == nc - 1)
    def _():
        for h in range(HG_H):
            sn_ref[h * 128:(h + 1) * 128, :] = st[h * 128:(h + 1) * 128, :].T


def _gla_scan(qfig, s0, lb, nw, *, nb, seq, T, row0):
    nc = seq // T
    rb0 = row0 // T
    per_b = lambda b, c: (b, 0, 0)

    def cols(k):
        return pl.BlockSpec((T, D), lambda b, c: (rb0 + b * nc + c, k))

    return pl.pallas_call(
        functools.partial(_gla_kernel, T=T, nc=nc),
        grid=(nb, nc),
        in_specs=[
            cols(0), cols(1), cols(2), cols(3),
            pl.BlockSpec((None, D, HG_DV), per_b),
            pl.BlockSpec((1, D), lambda b, c: (0, 0)),
            pl.BlockSpec((1, HG_DV), lambda b, c: (0, 0)),
        ],
        out_specs=[
            pl.BlockSpec((T, D), lambda b, c: (b * nc + c, 0)),
            pl.BlockSpec((None, D, HG_DV), per_b),
        ],
        out_shape=[
            jax.ShapeDtypeStruct((nb * seq, D), BF16),
            jax.ShapeDtypeStruct((nb, D, HG_DV), F32),
        ],
        scratch_shapes=[pltpu.VMEM((D, HG_DK), F32), pltpu.VMEM((T, D), F32), pltpu.VMEM((T, D), F32)],
        compiler_params=_cparams(2),
        name=f"gla_scan_t{T}",
    )(qfig, qfig, qfig, qfig, s0, lb, nw)


def _route_plan(top_i):
    e_flat = top_i.reshape(-1)
    onehot = (e_flat[:, None] == jnp.arange(NE, dtype=I32)[None, :]).astype(I32)
    csum = jnp.cumsum(onehot, axis=0)
    rank = jnp.sum(onehot * (csum - 1), axis=1)
    cnt = csum[-1]
    ntile = (cnt + MOE_TM - 1) // MOE_TM
    cum_t = jnp.cumsum(ntile)
    tile0 = cum_t - ntile
    pos = jnp.sum(onehot * tile0[None, :], axis=1) * MOE_TM + rank
    n_used = cum_t[-1]
    t_ids = jnp.arange(MOE_TILES, dtype=I32)
    te = jnp.sum((t_ids[:, None] >= cum_t[None, :]).astype(I32), axis=1)
    te_last = jnp.sum((n_used - 1 >= cum_t).astype(I32))
    te = jnp.where(t_ids < n_used, te, te_last).astype(I32)
    tok = jnp.arange(2 * M, dtype=I32) // 2
    src = jnp.zeros((MOE_ROWS,), I32).at[pos].set(tok)
    return te, n_used.reshape(1).astype(I32), src, pos.reshape(M, 2)


def kernel(x_prompt, x_sample, c_prompt, c_sample, state_ssd_conv, state_ssd, state_hgrn, ada_w, ada_b, norm_w,
           ssd_w_in, ssd_conv_w, ssd_conv_b, ssd_dt_bias, ssd_a_log, ssd_d, ssd_norm_w, ssd_w_out, hgrn_w_in,
           hgrn_lb_logits, hgrn_norm_w, hgrn_w_out, ffn_w_gu, ffn_w_down, moe_router, moe_w_gu, moe_w_down,
           final_norm_w):
    x = (x_prompt.reshape(MP, D), x_sample.reshape(MS, D))
    c_all = jnp.concatenate([c_prompt, jnp.zeros((MOD_S0 - NP, D), F32), c_sample], axis=0)
    norm_w4 = norm_w.reshape(4, 1, D)

    mods = [
        _mm_plain(c_all, ada_w, layer=l, col0=0, n_out=6 * D, tn=1024, tm=MOD_ROWS,
                  bias=ada_b[l].reshape(1, 6 * D), act_lhs=True, name="adaln")
        for l in range(2)
    ]

    hn = _prenorm(x, norm_w4, 0, mods[0], 0, 1, BF16)
    z = _mm_plain(hn, ssd_w_in, layer=0, col0=0, n_out=SSD_DI, tn=1024, tm=1024, name="ssd_in_z")
    xbc = _mm_plain(hn, ssd_w_in, layer=0, col0=SSD_DI, n_out=SSD_XBC, tn=1024, tm=1024, name="ssd_in_xbc")
    dtr = _mm_plain(hn, ssd_w_in, layer=0, col0=SSD_DI + SSD_XBC, n_out=128, tn=128, tm=1024, name="ssd_in_dt")

    pad64 = lambda v: jnp.pad(v.reshape(1, SSD_HEADS), ((0, 0), (0, 128 - SSD_HEADS)))
    head_of = jnp.arange(SSD_DI, dtype=I32) // SSD_P
    e1 = (jnp.arange(128, dtype=I32)[:, None] == head_of[None, :]).astype(BF16)
    e2 = jnp.concatenate([e1, e1], axis=0)
    et2 = jnp.concatenate([e1.T, e1.T], axis=1)
    consts = (ssd_conv_w[0], ssd_conv_b[0].reshape(1, SSD_XBC), pad64(ssd_dt_bias[0]), pad64(ssd_a_log[0]),
              jnp.repeat(ssd_d[0], SSD_P).reshape(1, SSD_DI), ssd_norm_w[0].reshape(1, SSD_DI), e2, et2)
    yp, p_conv, p_ssd = _ssd_scan(z, xbc, dtr, jnp.zeros((NP, SSD_K - 1, SSD_XBC), F32),
                                  jnp.zeros((NP, SSD_DI, SSD_N), F32), consts, nb=NP, seq=LP, C=64, row0=0)
    ysm, s_conv, s_ssd = _ssd_scan(z, xbc, dtr, state_ssd_conv[0], state_ssd[0].reshape(NS, SSD_DI, SSD_N), consts,
                                   nb=NS, seq=LS, C=32, row0=MP)
    x = _mm_resid((yp, ysm), ssd_w_out, x, mods[0], 2, k=SSD_DI, tn=512, tm=512, name="ssd_out")

    hn = _prenorm(x, norm_w4, 1, mods[0], 3, 4, BF16)
    act = _mm_swiglu(hn, ffn_w_gu, FFN, tn=512, tm=1024, name="ffn_gu")
    x = _mm_resid(act, ffn_w_down, x, mods[0], 5, k=FFN, tn=512, tm=512, name="ffn_down")

    hn = _prenorm(x, norm_w4, 2, mods[1], 0, 1, BF16)
    qfig = _mm_plain(hn, hgrn_w_in, layer=0, col0=0, n_out=4 * D, tn=1024, tm=1024, name="hgrn_in")
    p = jax.nn.softmax(hgrn_lb_logits.astype(F32), axis=0)
    lb = (jnp.cumsum(p, axis=0) - p[0])[1].reshape(1, D)
    nw_h = hgrn_norm_w[0].reshape(1, HG_DV)
    op, p_hgrn = _gla_scan(qfig, jnp.zeros((NP, D, HG_DV), F32), lb, nw_h, nb=NP, seq=LP, T=64, row0=0)
    osm, s_hgrn = _gla_scan(qfig, state_hgrn[0].reshape(NS, D, HG_DV), lb, nw_h, nb=NS, seq=LS, T=32, row0=MP)
    x = _mm_resid((op, osm), hgrn_w_out, x, mods[1], 2, k=D, tn=1024, tm=512, name="hgrn_out")

    hn32 = _prenorm(x, norm_w4, 3, mods[1], 3, 4, F32)
    gv, gi = _router(hn32, jnp.pad(moe_router[0], ((0, 0), (0, 128 - NE))))
    te, nu, src, pos = _route_plan(gi[:, :2])
    xs = _gather_rows(nu, src.reshape(MOE_TILES, 1, MOE_TM), hn32)
    act = _moe_gu(te, nu, xs, moe_w_gu[0])
    ys = _moe_down(te, nu, act, moe_w_down[0])
    tmc = 256
    pos3 = jnp.concatenate([pos[:, 0].reshape(M // tmc, 1, tmc), pos[:, 1].reshape(M // tmc, 1, tmc)], axis=2)
    yo_p, yo_s = _combine(pos3, ys, x, gv, mods[1], 5, final_norm_w.reshape(1, D), tm=tmc)

    return (
        yo_p.reshape(NP, LP, D),
        yo_s.reshape(NS, LS, D),
        p_conv[None],
        p_ssd.reshape(1, NP, SSD_HEADS, SSD_P, SSD_N),
        p_hgrn.reshape(1, NP, HG_H, HG_DK, HG_DV),
        s_conv[None],
        s_ssd.reshape(1, NS, SSD_HEADS, SSD_P, SSD_N),
        s_hgrn.reshape(1, NS, HG_H, HG_DK, HG_DV),
    )
```

```python
import functools

import jax
import jax.numpy as jnp
from jax import lax
from jax.experimental import pallas as pl
from jax.experimental.pallas import tpu as pltpu

F32 = jnp.float32
BF16 = jnp.bfloat16
I32 = jnp.int32

EPS = 1e-6
LOG2E = 1.4426950408889634
D = 2048
NP, LP = 4, 2048
NS, LS = 32, 32
MP, MS = NP * LP, NS * LS
M = MP + MS
MOD_S0 = 8
MOD_ROWS = MOD_S0 + NS

SSD_DI = 4096
SSD_HEADS = 64
SSD_P = 64
SSD_G = 8
SSD_GW = SSD_DI // SSD_G
SSD_N = 128
SSD_XBC = SSD_DI + 2 * SSD_G * SSD_N
SSD_K = 4

HG_H = 16
HG_DK = 128
HG_DV = 128
HG_BLK = 16

FFN = 5632
NE = 8
EH = 7168
MOE_SORT = 256
MOE_TILES = (2 * M) // MOE_SORT + NE
MOE_STEPS = MOE_TILES // 2
MOE_ROWS = MOE_TILES * MOE_SORT

VMEM_LIMIT = 56 * 1024 * 1024

NT_DIMS = (((1,), (1,)), ((), ()))
TN_DIMS = (((0,), (0,)), ((), ()))


def _cparams(n_axes):
    return pltpu.CompilerParams(dimension_semantics=("arbitrary",) * n_axes,
                                vmem_limit_bytes=VMEM_LIMIT)


def _silu(x):
    h = 0.5 * x
    return h + h * jnp.tanh(h)


def _softplus(x):
    return jnp.maximum(x, 0.0) + jnp.log1p(jnp.exp(-jnp.abs(x)))


def _split3(x):
    hi = x.astype(BF16)
    r1 = x - hi.astype(F32)
    mid = r1.astype(BF16)
    lo = (r1 - mid.astype(F32)).astype(BF16)
    return hi, mid, lo


def _dot_exact_rhs(mat_bf16, x):
    hi, mid, lo = _split3(x)
    acc = jnp.dot(mat_bf16, hi, preferred_element_type=F32)
    acc = acc + jnp.dot(mat_bf16, mid, preferred_element_type=F32)
    return acc + jnp.dot(mat_bf16, lo, preferred_element_type=F32)


def _grouped_apply(i, tm, mod_refs, fn):
    n_pt = MP // tm
    tpb = LP // tm

    @pl.when(i < n_pt)
    def _():
        r = i // tpb
        fn([m[pl.ds(r, 1), :][None] for m in mod_refs], 1, 0)

    @pl.when(i >= n_pt)
    def _():
        g = tm // LS
        start = pl.multiple_of(MOD_S0 + (i - n_pt) * g, 8)
        fn([m[pl.ds(start, g), :][:, None, :] for m in mod_refs], g, 1)


def _row_operand(op, tm, width, ij):
    if not isinstance(op, tuple):
        return [pl.BlockSpec((tm, width), lambda *g: ij(*g))], [op]
    n_pt = MP // tm
    return [
        pl.BlockSpec((tm, width), lambda *g: (jnp.minimum(ij(*g)[0], n_pt - 1), ij(*g)[1])),
        pl.BlockSpec((tm, width), lambda *g: (jnp.maximum(ij(*g)[0] - n_pt, 0), ij(*g)[1])),
    ], list(op)


def _prenorm_kernel(*refs, tm, nx):
    x_refs = refs[:nx]
    nw_ref, sc_ref, sh_ref, o_ref = refs[nx:]
    i = pl.program_id(0)

    def fn(mods, g, s):
        sc, sh = mods
        x = x_refs[s % nx][...]
        ms = jnp.mean(x * x, axis=-1, keepdims=True)
        y = x * lax.rsqrt(ms + EPS) * nw_ref[...]
        y = y.reshape(g, tm // g, D) * (1.0 + sc) + sh
        o_ref[...] = y.reshape(tm, D).astype(o_ref.dtype)

    _grouped_apply(i, tm, [sc_ref, sh_ref], fn)


def _prenorm(x, norm_w4, k, mod, sh_blk, sc_blk, out_dtype, tm=512):
    x_specs, x_args = _row_operand(x, tm, D, lambda i: (i, 0))
    return pl.pallas_call(
        functools.partial(_prenorm_kernel, tm=tm, nx=len(x_args)),
        grid=(M // tm,),
        in_specs=x_specs + [
            pl.BlockSpec((None, 1, D), lambda i: (k, 0, 0)),
            pl.BlockSpec((MOD_ROWS, D), lambda i: (0, sc_blk)),
            pl.BlockSpec((MOD_ROWS, D), lambda i: (0, sh_blk)),
        ],
        out_specs=pl.BlockSpec((tm, D), lambda i: (i, 0)),
        out_shape=jax.ShapeDtypeStruct((M, D), out_dtype),
        compiler_params=_cparams(1),
        name="prenorm",
    )(*x_args, norm_w4, mod, mod)


def _mm_plain_kernel(*refs, act_lhs, has_bias, valid_cols):
    if has_bias:
        lhs_ref, w_ref, b_ref, o_ref, wb = refs
    else:
        lhs_ref, w_ref, o_ref, wb = refs

    @pl.when(pl.program_id(1) == 0)
    def _():
        w = w_ref[...]
        if valid_cols is not None:
            w = jnp.where(lax.broadcasted_iota(I32, w.shape, 1) < valid_cols, w, 0.0)
        wb[...] = w.astype(BF16)

    lhs = lhs_ref[...]
    if act_lhs:
        lhs = _silu(lhs).astype(BF16)
    acc = jnp.dot(lhs, wb[...], preferred_element_type=F32)
    if has_bias:
        acc = acc + b_ref[...]
    o_ref[...] = acc.astype(o_ref.dtype)


def _mm_plain(lhs, w3, *, layer, col0, n_out, tn, tm, out_dtype=F32, bias=None, act_lhs=False, name="mm"):
    rows, k = lhs.shape
    off = col0 // tn
    w_cols = w3.shape[2] - col0
    valid_cols = w_cols if w_cols < n_out else None
    in_specs = [
        pl.BlockSpec((tm, k), lambda j, i: (i, 0)),
        pl.BlockSpec((None, k, tn), lambda j, i: (layer, 0, j + off)),
    ]
    args = [lhs, w3]
    if bias is not None:
        in_specs.append(pl.BlockSpec((1, tn), lambda j, i: (0, j)))
        args.append(bias)
    return pl.pallas_call(
        functools.partial(_mm_plain_kernel, act_lhs=act_lhs, has_bias=bias is not None, valid_cols=valid_cols),
        grid=(n_out // tn, rows // tm),
        in_specs=in_specs,
        out_specs=pl.BlockSpec((tm, tn), lambda j, i: (i, j)),
        out_shape=jax.ShapeDtypeStruct((rows, n_out), out_dtype),
        scratch_shapes=[pltpu.VMEM((k, tn), BF16)],
        compiler_params=_cparams(2),
        name=name,
    )(*args)


def _mm_resid_kernel(*refs, tm, tn, nl, nr):
    lhs_refs = refs[:nl]
    w_ref = refs[nl]
    res_refs = refs[nl + 1:nl + 1 + nr]
    gate_ref, o_ref, wb = refs[nl + 1 + nr:]
    i = pl.program_id(1)

    @pl.when(i == 0)
    def _():
        wb[...] = w_ref[...].astype(BF16)

    def fn(mods, g, s):
        (gate,) = mods
        acc = jnp.dot(lhs_refs[s % nl][...], wb[...], preferred_element_type=F32)
        out = res_refs[s % nr][...].reshape(g, tm // g, tn) + gate * acc.reshape(g, tm // g, tn)
        o_ref[...] = out.reshape(tm, tn)

    _grouped_apply(i, tm, [gate_ref], fn)


def _mm_resid(lhs, w3, res, mod, gate_blk, *, k, tn, tm, name):
    goff = gate_blk * (D // tn)
    l_specs, l_args = _row_operand(lhs, tm, k, lambda j, i: (i, 0))
    r_specs, r_args = _row_operand(res, tm, tn, lambda j, i: (i, j))
    return pl.pallas_call(
        functools.partial(_mm_resid_kernel, tm=tm, tn=tn, nl=len(l_args), nr=len(r_args)),
        grid=(D // tn, M // tm),
        in_specs=l_specs + [pl.BlockSpec((None, k, tn), lambda j, i: (0, 0, j))] + r_specs + [
            pl.BlockSpec((MOD_ROWS, tn), lambda j, i: (0, goff + j)),
        ],
        out_specs=pl.BlockSpec((tm, tn), lambda j, i: (i, j)),
        out_shape=jax.ShapeDtypeStruct((M, D), F32),
        scratch_shapes=[pltpu.VMEM((k, tn), BF16)],
        compiler_params=_cparams(2),
        name=name,
    )(*l_args, w3, *r_args, mod)


def _mm_swiglu_kernel(lhs_ref, wg_ref, wu_ref, o_ref, wgb, wub):
    @pl.when(pl.program_id(1) == 0)
    def _():
        wgb[...] = wg_ref[...].astype(BF16)
        wub[...] = wu_ref[...].astype(BF16)

    lhs = lhs_ref[...]
    gt = jnp.dot(lhs, wgb[...], preferred_element_type=F32)
    up = jnp.dot(lhs, wub[...], preferred_element_type=F32)
    o_ref[...] = (_silu(gt) * up).astype(o_ref.dtype)


def _mm_swiglu(lhs, w3, hidden, *, tn, tm, name):
    rows, k = lhs.shape
    nb = hidden // tn
    return pl.pallas_call(
        _mm_swiglu_kernel,
        grid=(nb, rows // tm),
        in_specs=[
            pl.BlockSpec((tm, k), lambda j, i: (i, 0)),
            pl.BlockSpec((None, k, tn), lambda j, i: (0, 0, j)),
            pl.BlockSpec((None, k, tn), lambda j, i: (0, 0, j + nb)),
        ],
        out_specs=pl.BlockSpec((tm, tn), lambda j, i: (i, j)),
        out_shape=jax.ShapeDtypeStruct((rows, hidden), BF16),
        scratch_shapes=[pltpu.VMEM((k, tn), BF16), pltpu.VMEM((k, tn), BF16)],
        compiler_params=_cparams(2),
        name=name,
    )(lhs, w3, w3)


def _expert_matmul_kernel(te_ref, first_ref, nxt_ref, misc_ref, lhs_ref, w_hbm, o_ref, stage, panel, sem,
                          *, tn, nj, col_offs, epilogue):
    j = pl.program_id(0)
    i = pl.program_id(1)
    nu = misc_ref[0]
    first_e = misc_ref[1]
    half = MOE_SORT

    def weight_copies(e, jj):
        c0 = pl.multiple_of(jj * tn, tn)
        return [pltpu.make_async_copy(w_hbm.at[e, :, pl.ds(off + c0, tn)], stage.at[n], sem.at[n])
                for n, off in enumerate(col_offs)]

    def change(t, very_first):
        e_new = te_ref[t]

        @pl.when(very_first)
        def _():
            for cp in weight_copies(e_new, j):
                cp.start()

        for cp in weight_copies(e_new, j):
            cp.wait()
        for n in range(len(col_offs)):
            panel[n] = stage[n].astype(BF16)
        ne = nxt_ref[e_new]
        same_col = ne >= 0

        @pl.when(jnp.logical_or(same_col, j + 1 < nj))
        def _():
            for cp in weight_copies(jnp.where(same_col, ne, first_e), jnp.where(same_col, j, j + 1)):
                cp.start()

    def rows_dot(r0, rows):
        lhs = lhs_ref[r0:r0 + rows, :]
        accs = [jnp.dot(lhs, panel[n], preferred_element_type=F32) for n in range(len(col_offs))]
        o_ref[r0:r0 + rows, :] = epilogue(*accs).astype(o_ref.dtype)

    t0 = 2 * i
    t1 = t0 + 1
    v0 = t0 < nu
    v1 = t1 < nu
    f0 = first_ref[t0] == 1
    f1 = first_ref[t1] == 1

    @pl.when(f0)
    def _():
        change(t0, jnp.logical_and(j == 0, i == 0))

    @pl.when(jnp.logical_and(v1, jnp.logical_not(f1)))
    def _():
        rows_dot(0, 2 * half)

    @pl.when(f1)
    def _():
        rows_dot(0, half)
        change(t1, False)
        rows_dot(half, half)

    @pl.when(jnp.logical_and(v0, jnp.logical_not(v1)))
    def _():
        rows_dot(0, half)
        o_ref[half:, :] = jnp.zeros((half, tn), o_ref.dtype)

    @pl.when(jnp.logical_not(v0))
    def _():
        o_ref[...] = jnp.zeros_like(o_ref)


def _expert_matmul(plan, lhs, w, *, k, n_out, tn, col_offs, epilogue, out_dtype, name):
    te, first, nxt, misc = plan
    nj = n_out // tn
    nw = len(col_offs)
    grid_spec = pltpu.PrefetchScalarGridSpec(
        num_scalar_prefetch=4,
        grid=(nj, MOE_STEPS),
        in_specs=[
            pl.BlockSpec((2 * MOE_SORT, k), lambda j, i, *_: (i, 0)),
            pl.BlockSpec(memory_space=pl.ANY),
        ],
        out_specs=pl.BlockSpec((2 * MOE_SORT, tn), lambda j, i, *_: (i, j)),
        scratch_shapes=[pltpu.VMEM((nw, k, tn), F32), pltpu.VMEM((nw, k, tn), BF16),
                        pltpu.SemaphoreType.DMA((nw,))],
    )
    return pl.pallas_call(
        functools.partial(_expert_matmul_kernel, tn=tn, nj=nj, col_offs=col_offs, epilogue=epilogue),
        grid_spec=grid_spec,
        out_shape=jax.ShapeDtypeStruct((MOE_ROWS, n_out), out_dtype),
        compiler_params=_cparams(2),
        name=name,
    )(te, first, nxt, misc, lhs, w)


def _moe_gu(plan, xs, w_gu):
    return _expert_matmul(plan, xs, w_gu, k=D, n_out=EH, tn=1024, col_offs=(0, EH),
                          epilogue=lambda gt, up: _silu(gt) * up, out_dtype=BF16, name="moe_gu")


def _moe_down(plan, act, w_down):
    return _expert_matmul(plan, act, w_down, k=EH, n_out=D, tn=512, col_offs=(0,),
                          epilogue=lambda acc: acc, out_dtype=F32, name="moe_down")


def _router_kernel(h_ref, r_ref, gv_ref, gi_ref):
    logits = jnp.dot(h_ref[...], r_ref[...], preferred_element_type=F32, precision=lax.Precision.HIGHEST)
    lane = lax.broadcasted_iota(I32, logits.shape, 1)
    neg = jnp.float32(-jnp.inf)
    l1 = jnp.where(lane < NE, logits, neg)
    m1 = jnp.max(l1, axis=-1, keepdims=True)
    i1 = jnp.min(jnp.where(l1 == m1, lane, 128), axis=-1, keepdims=True)
    l2 = jnp.where(lane == i1, neg, l1)
    m2 = jnp.max(l2, axis=-1, keepdims=True)
    i2 = jnp.min(jnp.where(l2 == m2, lane, 128), axis=-1, keepdims=True)
    e = jnp.exp(m2 - m1)
    g1 = 1.0 / (1.0 + e)
    g2 = e * g1
    gv_ref[...] = jnp.where(lane == 0, g1, jnp.where(lane == 1, g2, 0.0))
    gi_ref[...] = jnp.where(lane == 0, i1, jnp.where(lane == 1, i2, 0))


def _router(hn, router_pad, tm=512):
    return pl.pallas_call(
        _router_kernel,
        grid=(M // tm,),
        in_specs=[pl.BlockSpec((tm, D), lambda i: (i, 0)), pl.BlockSpec((D, 128), lambda i: (0, 0))],
        out_specs=[pl.BlockSpec((tm, 128), lambda i: (i, 0)), pl.BlockSpec((tm, 128), lambda i: (i, 0))],
        out_shape=[jax.ShapeDtypeStruct((M, 128), F32), jax.ShapeDtypeStruct((M, 128), I32)],
        compiler_params=_cparams(1),
        name="router",
    )(hn, router_pad)


def _row_copy(src_hbm, row, buf, r, sem):
    return pltpu.make_async_copy(src_hbm.at[pl.ds(row, 1), :], buf.at[pl.ds(r, 1), :], sem)


def _start_rows(src_hbm, idx_ref, off, dst, sem, n):
    def body(p, carry):
        for q in range(2):
            r = 2 * p + q
            _row_copy(src_hbm, idx_ref[0, 0, off + r], dst, r, sem).start(priority=q)
        return carry

    lax.fori_loop(0, n // 2, body, 0, unroll=4)


def _wait_rows(src_hbm, dst, sem, n):
    pltpu.make_async_copy(src_hbm.at[pl.ds(0, n), :], dst, sem).wait()


def _gather_kernel(nu_ref, idx_ref, idx_next_ref, src_hbm, o_ref, buf, sem, *, tm):
    t = pl.program_id(0)
    nu = nu_ref[0]
    slot = lax.rem(t, 2)

    @pl.when(t == 0)
    def _():
        _start_rows(src_hbm, idx_ref, 0, buf.at[0], sem.at[0], tm)

    @pl.when(t + 1 < nu)
    def _():
        _start_rows(src_hbm, idx_next_ref, 0, buf.at[1 - slot], sem.at[1 - slot], tm)

    @pl.when(t < nu)
    def _():
        _wait_rows(src_hbm, buf.at[slot], sem.at[slot], tm)
        o_ref[...] = buf[slot].astype(o_ref.dtype)

    @pl.when(t >= nu)
    def _():
        o_ref[...] = jnp.zeros_like(o_ref)


def _gather_rows(nu, idx3, src):
    tm = 2 * MOE_SORT
    nt = MOE_STEPS
    grid_spec = pltpu.PrefetchScalarGridSpec(
        num_scalar_prefetch=1,
        grid=(nt,),
        in_specs=[
            pl.BlockSpec((1, 1, tm), lambda t, nu: (t, 0, 0), memory_space=pltpu.SMEM),
            pl.BlockSpec((1, 1, tm), lambda t, nu: (jnp.minimum(t + 1, nt - 1), 0, 0), memory_space=pltpu.SMEM),
            pl.BlockSpec(memory_space=pl.ANY),
        ],
        out_specs=pl.BlockSpec((tm, D), lambda t, nu: (t, 0)),
        scratch_shapes=[pltpu.VMEM((2, tm, D), F32), pltpu.SemaphoreType.DMA((2,))],
    )
    return pl.pallas_call(
        functools.partial(_gather_kernel, tm=tm),
        grid_spec=grid_spec,
        out_shape=jax.ShapeDtypeStruct((MOE_ROWS, D), BF16),
        compiler_params=_cparams(1),
        name="moe_gather",
    )(nu, idx3, idx3, src)


def _combine_kernel(pos_ref, pos_next_ref, ys_hbm, x_ref, gv_ref, gate_ref, fw_ref, op_ref, os_ref, buf, sem,
                    *, tm, nt):
    t = pl.program_id(0)
    slot = lax.rem(t, 2)

    def start(idx_ref, s):
        _start_rows(ys_hbm, idx_ref, 0, buf.at[s, 0], sem.at[s, 0], tm)
        _start_rows(ys_hbm, idx_ref, tm, buf.at[s, 1], sem.at[s, 1], tm)

    @pl.when(t == 0)
    def _():
        start(pos_ref, 0)

    @pl.when(t + 1 < nt)
    def _():
        start(pos_next_ref, 1 - slot)

    _wait_rows(ys_hbm, buf.at[slot, 0], sem.at[slot, 0], tm)
    _wait_rows(ys_hbm, buf.at[slot, 1], sem.at[slot, 1], tm)

    def fn(mods, g, s):
        (gate,) = mods
        gv = gv_ref[...]
        f = gv[:, 0:1] * buf[slot, 0] + gv[:, 1:2] * buf[slot, 1]
        x = x_ref[...].reshape(g, tm // g, D) + gate * f.reshape(g, tm // g, D)
        x = x.reshape(tm, D)
        ms = jnp.mean(x * x, axis=-1, keepdims=True)
        (op_ref, os_ref)[s][...] = x * lax.rsqrt(ms + EPS) * fw_ref[...]

    _grouped_apply(t, tm, [gate_ref], fn)


def _combine(pos3, ys, x, gv, mod, gate_blk, final_w, tm=256):
    nt = M // tm
    n_pt = MP // tm
    return pl.pallas_call(
        functools.partial(_combine_kernel, tm=tm, nt=nt),
        grid=(nt,),
        in_specs=[
            pl.BlockSpec((1, 1, 2 * tm), lambda t: (t, 0, 0), memory_space=pltpu.SMEM),
            pl.BlockSpec((1, 1, 2 * tm), lambda t: (jnp.minimum(t + 1, nt - 1), 0, 0), memory_space=pltpu.SMEM),
            pl.BlockSpec(memory_space=pl.ANY),
            pl.BlockSpec((tm, D), lambda t: (t, 0)),
            pl.BlockSpec((tm, 128), lambda t: (t, 0)),
            pl.BlockSpec((MOD_ROWS, D), lambda t: (0, gate_blk)),
            pl.BlockSpec((1, D), lambda t: (0, 0)),
        ],
        out_specs=[
            pl.BlockSpec((tm, D), lambda t: (jnp.minimum(t, n_pt - 1), 0)),
            pl.BlockSpec((tm, D), lambda t: (jnp.maximum(t - n_pt, 0), 0)),
        ],
        out_shape=[jax.ShapeDtypeStruct((MP, D), F32), jax.ShapeDtypeStruct((MS, D), F32)],
        scratch_shapes=[pltpu.VMEM((2, 2, tm, D), F32), pltpu.SemaphoreType.DMA((2, 2))],
        compiler_params=_cparams(1),
        name="moe_combine",
    )(pos3, pos3, ys, x, gv, mod, final_w)


def _ssd_kernel(z_ref, xbc_ref, dt_ref, cs_ref, h0_ref, cw_ref, cb_ref, dtb_ref, alog_ref, dsk_ref, nw_ref,
                e2_ref, et2_ref, y_ref, csn_ref, hn_ref, cbuf, hst, *, C, nc):
    c = pl.program_id(1)
    hp = 128 // C
    neg = jnp.float32(-jnp.inf)

    @pl.when(c == 0)
    def _():
        cbuf[0:8, :] = jnp.zeros((8, SSD_XBC), F32)
        cbuf[5:8, :] = cs_ref[...]
        hst[...] = h0_ref[...]

    cbuf[8:8 + C, :] = xbc_ref[...]

    @pl.when(c == nc - 1)
    def _():
        csn_ref[...] = cbuf[C + 5:C + 8, :]

    dt = _softplus(dt_ref[...] + dtb_ref[...])
    a = -jnp.exp(alog_ref[...]) * LOG2E
    row = lax.broadcasted_iota(I32, (C, C), 0)
    col = lax.broadcasted_iota(I32, (C, C), 1)
    tril = jnp.where(row >= col, 1.0, 0.0).astype(BF16)
    b = _dot_exact_rhs(tril, dt * a)
    eb = jnp.exp2(b)
    b_last = b[C - 1:C, :]
    dl = dt * jnp.exp2(b_last - b)
    stack = jnp.concatenate([dt, eb, dl], axis=0)
    s_hi = stack.astype(BF16)
    s_lo = (stack - s_hi.astype(F32)).astype(BF16)
    lhs2 = jnp.concatenate([s_hi, s_lo], axis=1)
    bst = jnp.concatenate([b] + [pltpu.roll(b, 128 - k, axis=1) for k in range(1, hp)], axis=0)
    bt = bst.T
    rdec = jnp.broadcast_to(jnp.exp2(bt[:, C - 1:C]), (128, 128))
    r_hi = rdec.astype(BF16)
    r_lo = (rdec - r_hi.astype(F32)).astype(BF16)
    r2 = jnp.concatenate([r_hi, r_lo], axis=0)

    lane = lax.broadcasted_iota(I32, (C, 128), 1)
    trow = lax.broadcasted_iota(I32, (C, 128), 0)
    causal = trow >= (lane & (C - 1))
    lane_seg = lane >> (C.bit_length() - 1)
    xlane_seg = lax.broadcasted_iota(I32, (C, hp * SSD_P), 1) >> (SSD_P.bit_length() - 1)

    def conv(cols):
        v = cbuf[0:C + 8, cols]
        acc = cb_ref[:, cols] + v[8:8 + C] * cw_ref[SSD_K - 1:SSD_K, cols]
        for k in range(SSD_K - 1):
            acc = acc + pltpu.roll(v, C + 3 - k, axis=0)[0:C] * cw_ref[k:k + 1, cols]
        return _silu(acc)

    for g in range(SSD_G):
        xc = slice(g * SSD_GW, (g + 1) * SSD_GW)
        xg = conv(xc)
        bg = conv(slice(SSD_DI + g * SSD_N, SSD_DI + (g + 1) * SSD_N)).astype(BF16)
        cg = conv(slice(SSD_DI + SSD_G * SSD_N + g * SSD_N, SSD_DI + SSD_G * SSD_N + (g + 1) * SSD_N)).astype(BF16)
        ex = jnp.dot(lhs2, e2_ref[:, xc], preferred_element_type=F32)
        dt_e, eb_e, dl_e = ex[0:C], ex[C:2 * C], ex[2 * C:3 * C]
        xdt = xg * dt_e
        cb2 = lax.dot_general(cg, jnp.concatenate([bg] * hp, axis=0), NT_DIMS, preferred_element_type=F32)
        hg = hst[g * SSD_GW:(g + 1) * SSD_GW, :]
        y_inter = lax.dot_general(cg, hg.astype(BF16), NT_DIMS, preferred_element_type=F32)
        ys = []
        for p in range(8 // hp):
            h0 = g * 8 + p * hp
            bcol = jnp.broadcast_to(b[:, h0:h0 + 1], (C, 128))
            for k in range(1, hp):
                bcol = jnp.where(lane_seg >= k, b[:, h0 + k:h0 + k + 1], bcol)
            lm = jnp.exp2(jnp.where(causal, bcol - bt[h0:h0 + 1, :], neg))
            m2 = (cb2 * lm).astype(BF16)
            xp = xdt[:, p * hp * SSD_P:(p + 1) * hp * SSD_P]
            rhs = jnp.concatenate([jnp.where(xlane_seg == k, xp, 0.0) for k in range(hp)], axis=0).astype(BF16)
            ys.append(jnp.dot(m2, rhs, preferred_element_type=F32))
        y = jnp.concatenate(ys, axis=1) + y_inter * eb_e + xg * dsk_ref[:, xc]
        y = y * _silu(z_ref[:, xc])
        ms = jnp.mean(y * y, axis=-1, keepdims=True)
        y_ref[:, xc] = (y * lax.rsqrt(ms + EPS) * nw_ref[:, xc]).astype(y_ref.dtype)
        upd = lax.dot_general((xg * dl_e).astype(BF16), bg, TN_DIMS, preferred_element_type=F32)
        dec = jnp.dot(et2_ref[g * SSD_GW:(g + 1) * SSD_GW, :], r2, preferred_element_type=F32)
        hst[g * SSD_GW:(g + 1) * SSD_GW, :] = hg * dec + upd

    cbuf[5:8, :] = cbuf[C + 5:C + 8, :]

    @pl.when(c == nc - 1)
    def _():
        hn_ref[...] = hst[...]


def _ssd_scan(z, xbc, dtr, conv_state, h0, consts, *, nb, seq, C, row0):
    nc = seq // C
    rb0 = row0 // C
    cw, cb, dtb, alog, dsk, nw, e2, et2 = consts
    rows = lambda b, c: (rb0 + b * nc + c, 0)
    fixed = lambda b, c: (0, 0)
    per_b = lambda b, c: (b, 0, 0)
    return pl.pallas_call(
        functools.partial(_ssd_kernel, C=C, nc=nc),
        grid=(nb, nc),
        in_specs=[
            pl.BlockSpec((C, SSD_DI), rows),
            pl.BlockSpec((C, SSD_XBC), rows),
            pl.BlockSpec((C, 128), rows),
            pl.BlockSpec((None, SSD_K - 1, SSD_XBC), per_b),
            pl.BlockSpec((None, SSD_DI, SSD_N), per_b),
            pl.BlockSpec((SSD_K, SSD_XBC), fixed),
            pl.BlockSpec((1, SSD_XBC), fixed),
            pl.BlockSpec((1, 128), fixed),
            pl.BlockSpec((1, 128), fixed),
            pl.BlockSpec((1, SSD_DI), fixed),
            pl.BlockSpec((1, SSD_DI), fixed),
            pl.BlockSpec((256, SSD_DI), fixed),
            pl.BlockSpec((SSD_DI, 256), fixed),
        ],
        out_specs=[
            pl.BlockSpec((C, SSD_DI), lambda b, c: (b * nc + c, 0)),
            pl.BlockSpec((None, SSD_K - 1, SSD_XBC), per_b),
            pl.BlockSpec((None, SSD_DI, SSD_N), per_b),
        ],
        out_shape=[
            jax.ShapeDtypeStruct((nb * seq, SSD_DI), BF16),
            jax.ShapeDtypeStruct((nb, SSD_K - 1, SSD_XBC), F32),
            jax.ShapeDtypeStruct((nb, SSD_DI, SSD_N), F32),
        ],
        scratch_shapes=[pltpu.VMEM((8 + C, SSD_XBC), F32), pltpu.VMEM((SSD_DI, SSD_N), F32)],
        compiler_params=_cparams(2),
        name=f"ssd_scan_c{C}",
    )(z, xbc, dtr, conv_state, h0, cw, cb, dtb, alog, dsk, nw, e2, et2)


def _gla_kernel(q_ref, f_ref, v_ref, g_ref, s0_ref, lb_ref, nw_ref, o_ref, sn_ref, st, b_s, c_s, *, T, nc):
    c = pl.program_id(1)
    neg = jnp.float32(-jnp.inf)

    @pl.when(c == 0)
    def _():
        for h in range(HG_H):
            st[h * 128:(h + 1) * 128, :] = s0_ref[h * 128:(h + 1) * 128, :].T

    row = lax.broadcasted_iota(I32, (T, T), 0)
    col = lax.broadcasted_iota(I32, (T, T), 1)
    sh = HG_BLK.bit_length() - 1
    same_blk = (row >> sh) == (col >> sh)
    bd = jnp.where(jnp.logical_and(same_blk, row >= col), 1.0, 0.0).astype(BF16)

    for h in range(HG_H):
        hl = slice(h * 128, (h + 1) * 128)
        fz = f_ref[:, hl]
        lb = lb_ref[:, hl]
        ls = jnp.minimum(fz, 0.0) - jnp.log1p(jnp.exp(-jnp.abs(fz)))
        a1 = jnp.log(lb)
        a2 = jnp.log1p(-lb) + ls
        lf = jnp.maximum(a1, a2) + jnp.log1p(jnp.exp(-jnp.abs(a1 - a2)))
        b2 = _dot_exact_rhs(bd, lf * LOG2E)
        b_s[:, hl] = b2
        c_s[:, hl] = b2 - (a2 - fz) * LOG2E

    trow = lax.broadcasted_iota(I32, (8, 128), 0)
    ones = jnp.ones((128, 128), BF16)
    hb = HG_BLK // 2

    def blk(j, carry):
        rows = pl.ds(pl.multiple_of(j * HG_BLK, HG_BLK), HG_BLK)
        for h in range(HG_H):
            hl = slice(h * 128, (h + 1) * 128)
            qb = q_ref[rows, hl]
            bb = b_s[rows, hl]
            cb = c_s[rows, hl]
            vb = v_ref[rows, hl]
            sth = st[hl, :]
            o = lax.dot_general((qb * jnp.exp2(bb)).astype(BF16), sth.astype(BF16), NT_DIMS,
                                preferred_element_type=F32)
            q_h, b_h = (qb[0:hb], qb[hb:]), (bb[0:hb], bb[hb:])
            pieces, owner = [], []
            for s in range(HG_BLK):
                cs = cb[s:s + 1, :]
                for half in range(2):
                    if s >= hb * (half + 1):
                        continue
                    e = b_h[half] - cs
                    if s >= hb * half:
                        e = jnp.where(trow >= s - hb * half, e, neg)
                    pieces.append(q_h[half] * jnp.exp2(e))
                    owner.append((s, half))
            att = jnp.dot(jnp.concatenate(pieces, axis=0).astype(BF16), ones, preferred_element_type=F32)
            o_h = [o[0:hb], o[hb:]]
            for n, (s, half) in enumerate(owner):
                o_h[half] = o_h[half] + att[n * hb:(n + 1) * hb] * vb[s:s + 1, :]
            o = jnp.concatenate(o_h, axis=0)
            b_end = bb[HG_BLK - 1:HG_BLK, :]
            khat = jnp.exp2(b_end - cb).astype(BF16)
            upd = lax.dot_general(vb.astype(BF16), khat, TN_DIMS, preferred_element_type=F32)
            st[hl, :] = sth * jnp.exp2(b_end) + upd
            ms = jnp.mean(o * o, axis=-1, keepdims=True)
            o_ref[rows, hl] = (o * lax.rsqrt(ms + EPS) * nw_ref[...] * _silu(g_ref[rows, hl])).astype(o_ref.dtype)
        return carry

    lax.fori_loop(0, T // HG_BLK, blk, 0)

    @pl.when(c == nc - 1)
    def _():
        for h in range(HG_H):
            sn_ref[h * 128:(h + 1) * 128, :] = st[h * 128:(h + 1) * 128, :].T


def _gla_scan(qfig, s0, lb, nw, *, nb, seq, T, row0):
    nc = seq // T
    rb0 = row0 // T
    per_b = lambda b, c: (b, 0, 0)

    def cols(k):
        return pl.BlockSpec((T, D), lambda b, c: (rb0 + b * nc + c, k))

    return pl.pallas_call(
        functools.partial(_gla_kernel, T=T, nc=nc),
        grid=(nb, nc),
        in_specs=[
            cols(0), cols(1), cols(2), cols(3),
            pl.BlockSpec((None, D, HG_DV), per_b),
            pl.BlockSpec((1, D), lambda b, c: (0, 0)),
            pl.BlockSpec((1, HG_DV), lambda b, c: (0, 0)),
        ],
        out_specs=[
            pl.BlockSpec((T, D), lambda b, c: (b * nc + c, 0)),
            pl.BlockSpec((None, D, HG_DV), per_b),
        ],
        out_shape=[
            jax.ShapeDtypeStruct((nb * seq, D), BF16),
            jax.ShapeDtypeStruct((nb, D, HG_DV), F32),
        ],
        scratch_shapes=[pltpu.VMEM((D, HG_DK), F32), pltpu.VMEM((T, D), F32), pltpu.VMEM((T, D), F32)],
        compiler_params=_cparams(2),
        name=f"gla_scan_t{T}",
    )(qfig, qfig, qfig, qfig, s0, lb, nw)


def _route_plan(top_i):
    e_flat = top_i.reshape(-1)
    onehot = (e_flat[:, None] == jnp.arange(NE, dtype=I32)[None, :]).astype(I32)
    csum = jnp.cumsum(onehot, axis=0)
    rank = jnp.sum(onehot * (csum - 1), axis=1)
    cnt = csum[-1]
    ntile = (cnt + MOE_SORT - 1) // MOE_SORT
    cum_t = jnp.cumsum(ntile)
    tile0 = cum_t - ntile
    pos = jnp.sum(onehot * tile0[None, :], axis=1) * MOE_SORT + rank
    n_used = cum_t[-1]
    t_ids = jnp.arange(MOE_TILES, dtype=I32)
    te = jnp.sum((t_ids[:, None] >= cum_t[None, :]).astype(I32), axis=1)
    te_last = jnp.sum((n_used - 1 >= cum_t).astype(I32))
    te = jnp.where(t_ids < n_used, te, te_last).astype(I32)
    prev = jnp.concatenate([jnp.full((1,), -1, I32), te[:-1]])
    first = jnp.logical_and(t_ids < n_used, te != prev).astype(I32)
    e_ids = jnp.arange(NE, dtype=I32)
    later = jnp.logical_and(e_ids[None, :] > e_ids[:, None], ntile[None, :] > 0)
    nxt = jnp.min(jnp.where(later, e_ids[None, :], NE), axis=1)
    nxt = jnp.where(nxt == NE, -1, nxt).astype(I32)
    misc = jnp.stack([n_used, te[0]]).astype(I32)
    tok = jnp.arange(2 * M, dtype=I32) // 2
    src = jnp.zeros((MOE_ROWS,), I32).at[pos].set(tok)
    n_steps = ((n_used + 1) // 2).reshape(1).astype(I32)
    return (te, first, nxt, misc), n_steps, src, pos.reshape(M, 2)


def kernel(x_prompt, x_sample, c_prompt, c_sample, state_ssd_conv, state_ssd, state_hgrn, ada_w, ada_b, norm_w,
           ssd_w_in, ssd_conv_w, ssd_conv_b, ssd_dt_bias, ssd_a_log, ssd_d, ssd_norm_w, ssd_w_out, hgrn_w_in,
           hgrn_lb_logits, hgrn_norm_w, hgrn_w_out, ffn_w_gu, ffn_w_down, moe_router, moe_w_gu, moe_w_down,
           final_norm_w):
    x = (x_prompt.reshape(MP, D), x_sample.reshape(MS, D))
    c_all = jnp.concatenate([c_prompt, jnp.zeros((MOD_S0 - NP, D), F32), c_sample], axis=0)
    norm_w4 = norm_w.reshape(4, 1, D)

    mods = [
        _mm_plain(c_all, ada_w, layer=l, col0=0, n_out=6 * D, tn=1024, tm=MOD_ROWS,
                  bias=ada_b[l].reshape(1, 6 * D), act_lhs=True, name="adaln")
        for l in range(2)
    ]

    hn = _prenorm(x, norm_w4, 0, mods[0], 0, 1, BF16)
    z = _mm_plain(hn, ssd_w_in, layer=0, col0=0, n_out=SSD_DI, tn=1024, tm=1024, name="ssd_in_z")
    xbc = _mm_plain(hn, ssd_w_in, layer=0, col0=SSD_DI, n_out=SSD_XBC, tn=1024, tm=1024, name="ssd_in_xbc")
    dtr = _mm_plain(hn, ssd_w_in, layer=0, col0=SSD_DI + SSD_XBC, n_out=128, tn=128, tm=1024, name="ssd_in_dt")

    pad64 = lambda v: jnp.pad(v.reshape(1, SSD_HEADS), ((0, 0), (0, 128 - SSD_HEADS)))
    head_of = jnp.arange(SSD_DI, dtype=I32) // SSD_P
    e1 = (jnp.arange(128, dtype=I32)[:, None] == head_of[None, :]).astype(BF16)
    e2 = jnp.concatenate([e1, e1], axis=0)
    et2 = jnp.concatenate([e1.T, e1.T], axis=1)
    consts = (ssd_conv_w[0], ssd_conv_b[0].reshape(1, SSD_XBC), pad64(ssd_dt_bias[0]), pad64(ssd_a_log[0]),
              jnp.repeat(ssd_d[0], SSD_P).reshape(1, SSD_DI), ssd_norm_w[0].reshape(1, SSD_DI), e2, et2)
    yp, p_conv, p_ssd = _ssd_scan(z, xbc, dtr, jnp.zeros((NP, SSD_K - 1, SSD_XBC), F32),
                                  jnp.zeros((NP, SSD_DI, SSD_N), F32), consts, nb=NP, seq=LP, C=64, row0=0)
    ysm, s_conv, s_ssd = _ssd_scan(z, xbc, dtr, state_ssd_conv[0], state_ssd[0].reshape(NS, SSD_DI, SSD_N), consts,
                                   nb=NS, seq=LS, C=32, row0=MP)
    x = _mm_resid((yp, ysm), ssd_w_out, x, mods[0], 2, k=SSD_DI, tn=512, tm=512, name="ssd_out")

    hn = _prenorm(x, norm_w4, 1, mods[0], 3, 4, BF16)
    act = _mm_swiglu(hn, ffn_w_gu, FFN, tn=512, tm=1024, name="ffn_gu")
    x = _mm_resid(act, ffn_w_down, x, mods[0], 5, k=FFN, tn=512, tm=512, name="ffn_down")

    hn = _prenorm(x, norm_w4, 2, mods[1], 0, 1, BF16)
    qfig = _mm_plain(hn, hgrn_w_in, layer=0, col0=0, n_out=4 * D, tn=1024, tm=1024, name="hgrn_in")
    p = jax.nn.softmax(hgrn_lb_logits.astype(F32), axis=0)
    lb = (jnp.cumsum(p, axis=0) - p[0])[1].reshape(1, D)
    nw_h = hgrn_norm_w[0].reshape(1, HG_DV)
    op, p_hgrn = _gla_scan(qfig, jnp.zeros((NP, D, HG_DV), F32), lb, nw_h, nb=NP, seq=LP, T=64, row0=0)
    osm, s_hgrn = _gla_scan(qfig, state_hgrn[0].reshape(NS, D, HG_DV), lb, nw_h, nb=NS, seq=LS, T=32, row0=MP)
    x = _mm_resid((op, osm), hgrn_w_out, x, mods[1], 2, k=D, tn=1024, tm=512, name="hgrn_out")

    hn32 = _prenorm(x, norm_w4, 3, mods[1], 3, 4, F32)
    gv, gi = _router(hn32, jnp.pad(moe_router[0], ((0, 0), (0, 128 - NE))))
    plan, n_steps, src, pos = _route_plan(gi[:, :2])
    xs = _gather_rows(n_steps, src.reshape(MOE_STEPS, 1, 2 * MOE_SORT), hn32)
    act = _moe_gu(plan, xs, moe_w_gu[0])
    ys = _moe_down(plan, act, moe_w_down[0])
    tmc = 256
    pos3 = jnp.concatenate([pos[:, 0].reshape(M // tmc, 1, tmc), pos[:, 1].reshape(M // tmc, 1, tmc)], axis=2)
    yo_p, yo_s = _combine(pos3, ys, x, gv, mods[1], 5, final_norm_w.reshape(1, D), tm=tmc)

    return (
        yo_p.reshape(NP, LP, D),
        yo_s.reshape(NS, LS, D),
        p_conv[None],
        p_ssd.reshape(1, NP, SSD_HEADS, SSD_P, SSD_N),
        p_hgrn.reshape(1, NP, HG_H, HG_DK, HG_DV),
        s_conv[None],
        s_ssd.reshape(1, NS, SSD_HEADS, SSD_P, SSD_N),
        s_hgrn.reshape(1, NS, HG_H, HG_DK, HG_DV),
    )
```

```python
import functools

import jax
import jax.numpy as jnp
from jax import lax
from jax.experimental import pallas as pl
from jax.experimental.pallas import tpu as pltpu

F32 = jnp.float32
BF16 = jnp.bfloat16
I32 = jnp.int32

EPS = 1e-6
LOG2E = 1.4426950408889634
D = 2048
NP, LP = 4, 2048
NS, LS = 32, 32
MP, MS = NP * LP, NS * LS
M = MP + MS
MOD_S0 = 8
MOD_ROWS = MOD_S0 + NS

SSD_DI = 4096
SSD_HEADS = 64
SSD_P = 64
SSD_G = 8
SSD_GW = SSD_DI // SSD_G
SSD_N = 128
SSD_XBC = SSD_DI + 2 * SSD_G * SSD_N
SSD_K = 4

HG_H = 16
HG_DK = 128
HG_DV = 128
HG_BLK = 16

FFN = 5632
NE = 8
EH = 7168
MOE_SORT = 256
MOE_TILES = (2 * M) // MOE_SORT + NE
MOE_STEPS = MOE_TILES // 2
MOE_ROWS = MOE_TILES * MOE_SORT

VMEM_LIMIT = 56 * 1024 * 1024

NT_DIMS = (((1,), (1,)), ((), ()))
TN_DIMS = (((0,), (0,)), ((), ()))


def _cparams(n_axes):
    return pltpu.CompilerParams(dimension_semantics=("arbitrary",) * n_axes,
                                vmem_limit_bytes=VMEM_LIMIT)


def _silu(x):
    h = 0.5 * x
    return h + h * jnp.tanh(h)


def _softplus(x):
    return jnp.maximum(x, 0.0) + jnp.log1p(jnp.exp(-jnp.abs(x)))


def _log1p_unit(u):
    return jnp.log(1.0 + u)


def _split3(x):
    hi = x.astype(BF16)
    r1 = x - hi.astype(F32)
    mid = r1.astype(BF16)
    lo = (r1 - mid.astype(F32)).astype(BF16)
    return hi, mid, lo


def _dot_exact_rhs(mat_bf16, x):
    hi, mid, lo = _split3(x)
    acc = jnp.dot(mat_bf16, hi, preferred_element_type=F32)
    acc = acc + jnp.dot(mat_bf16, mid, preferred_element_type=F32)
    return acc + jnp.dot(mat_bf16, lo, preferred_element_type=F32)


def _grouped_apply(i, tm, mod_refs, fn):
    n_pt = MP // tm
    tpb = LP // tm

    @pl.when(i < n_pt)
    def _():
        r = i // tpb
        fn([m[pl.ds(r, 1), :][None] for m in mod_refs], 1, 0)

    @pl.when(i >= n_pt)
    def _():
        g = tm // LS
        start = pl.multiple_of(MOD_S0 + (i - n_pt) * g, 8)
        fn([m[pl.ds(start, g), :][:, None, :] for m in mod_refs], g, 1)


def _row_operand(op, tm, width, ij):
    if not isinstance(op, tuple):
        return [pl.BlockSpec((tm, width), lambda *g: ij(*g))], [op]
    n_pt = MP // tm
    return [
        pl.BlockSpec((tm, width), lambda *g: (jnp.minimum(ij(*g)[0], n_pt - 1), ij(*g)[1])),
        pl.BlockSpec((tm, width), lambda *g: (jnp.maximum(ij(*g)[0] - n_pt, 0), ij(*g)[1])),
    ], list(op)


def _top2_gates(logits):
    lane = lax.broadcasted_iota(I32, logits.shape, 1)
    neg = jnp.float32(-jnp.inf)
    l1 = jnp.where(lane < NE, logits, neg)
    m1 = jnp.max(l1, axis=-1, keepdims=True)
    i1 = jnp.min(jnp.where(l1 == m1, lane, 128), axis=-1, keepdims=True)
    l2 = jnp.where(lane == i1, neg, l1)
    m2 = jnp.max(l2, axis=-1, keepdims=True)
    i2 = jnp.min(jnp.where(l2 == m2, lane, 128), axis=-1, keepdims=True)
    e = jnp.exp(m2 - m1)
    g1 = 1.0 / (1.0 + e)
    g2 = e * g1
    gv = jnp.where(lane == 0, g1, jnp.where(lane == 1, g2, 0.0))
    gi = jnp.where(lane == 0, i1, jnp.where(lane == 1, i2, 0))
    return gv, gi


def _prenorm_kernel(*refs, tm, nx, route):
    x_refs = refs[:nx]
    if route:
        nw_ref, sc_ref, sh_ref, r_ref, o_ref, gv_ref, gi_ref = refs[nx:]
    else:
        nw_ref, sc_ref, sh_ref, o_ref = refs[nx:]
    i = pl.program_id(0)

    def fn(mods, g, s):
        sc, sh = mods
        x = x_refs[s % nx][...]
        ms = jnp.mean(x * x, axis=-1, keepdims=True)
        y = x * lax.rsqrt(ms + EPS) * nw_ref[...]
        y = (y.reshape(g, tm // g, D) * (1.0 + sc) + sh).reshape(tm, D)
        o_ref[...] = y.astype(o_ref.dtype)
        if route:
            r = r_ref[...]
            y_hi, r_hi = y.astype(BF16), r.astype(BF16)
            y_lo = (y - y_hi.astype(F32)).astype(BF16)
            r_lo = (r - r_hi.astype(F32)).astype(BF16)
            logits = (jnp.dot(y_hi, r_hi, preferred_element_type=F32)
                      + jnp.dot(y_lo, r_hi, preferred_element_type=F32)
                      + jnp.dot(y_hi, r_lo, preferred_element_type=F32))
            gv_ref[...], gi_ref[...] = _top2_gates(logits)

    _grouped_apply(i, tm, [sc_ref, sh_ref], fn)


def _prenorm(x, norm_w4, k, mod, sh_blk, sc_blk, out_dtype, tm=512, router_pad=None):
    x_specs, x_args = _row_operand(x, tm, D, lambda i: (i, 0))
    route = router_pad is not None
    row_out = lambda w: pl.BlockSpec((tm, w), lambda i: (i, 0))
    out_specs = [row_out(D)]
    out_shape = [jax.ShapeDtypeStruct((M, D), out_dtype)]
    extra_specs, extra_args = [], []
    if route:
        extra_specs = [pl.BlockSpec((D, 128), lambda i: (0, 0))]
        extra_args = [router_pad]
        out_specs += [row_out(128), row_out(128)]
        out_shape += [jax.ShapeDtypeStruct((M, 128), F32), jax.ShapeDtypeStruct((M, 128), I32)]
    res = pl.pallas_call(
        functools.partial(_prenorm_kernel, tm=tm, nx=len(x_args), route=route),
        grid=(M // tm,),
        in_specs=x_specs + [
            pl.BlockSpec((None, 1, D), lambda i: (k, 0, 0)),
            pl.BlockSpec((MOD_ROWS, D), lambda i: (0, sc_blk)),
            pl.BlockSpec((MOD_ROWS, D), lambda i: (0, sh_blk)),
        ] + extra_specs,
        out_specs=out_specs,
        out_shape=out_shape,
        compiler_params=_cparams(1),
        name="prenorm_route" if route else "prenorm",
    )(*x_args, norm_w4, mod, mod, *extra_args)
    return res if route else res[0]


def _mm_plain_kernel(*refs, act_lhs, has_bias, valid_cols):
    if has_bias:
        lhs_ref, w_ref, b_ref, o_ref, wb = refs
    else:
        lhs_ref, w_ref, o_ref, wb = refs

    @pl.when(pl.program_id(1) == 0)
    def _():
        w = w_ref[...]
        if valid_cols is not None:
            w = jnp.where(lax.broadcasted_iota(I32, w.shape, 1) < valid_cols, w, 0.0)
        wb[...] = w.astype(BF16)

    lhs = lhs_ref[...]
    if act_lhs:
        lhs = _silu(lhs).astype(BF16)
    acc = jnp.dot(lhs, wb[...], preferred_element_type=F32)
    if has_bias:
        acc = acc + b_ref[...]
    o_ref[...] = acc.astype(o_ref.dtype)


def _mm_plain(lhs, w3, *, layer, col0, n_out, tn, tm, out_dtype=F32, bias=None, act_lhs=False, name="mm"):
    rows, k = lhs.shape
    off = col0 // tn
    w_cols = w3.shape[2] - col0
    valid_cols = w_cols if w_cols < n_out else None
    in_specs = [
        pl.BlockSpec((tm, k), lambda j, i: (i, 0)),
        pl.BlockSpec((None, k, tn), lambda j, i: (layer, 0, j + off)),
    ]
    args = [lhs, w3]
    if bias is not None:
        in_specs.append(pl.BlockSpec((1, tn), lambda j, i: (0, j)))
        args.append(bias)
    return pl.pallas_call(
        functools.partial(_mm_plain_kernel, act_lhs=act_lhs, has_bias=bias is not None, valid_cols=valid_cols),
        grid=(n_out // tn, rows // tm),
        in_specs=in_specs,
        out_specs=pl.BlockSpec((tm, tn), lambda j, i: (i, j)),
        out_shape=jax.ShapeDtypeStruct((rows, n_out), out_dtype),
        scratch_shapes=[pltpu.VMEM((k, tn), BF16)],
        compiler_params=_cparams(2),
        name=name,
    )(*args)


def _mm_resid_kernel(*refs, tm, tn, nl, nr):
    lhs_refs = refs[:nl]
    w_ref = refs[nl]
    res_refs = refs[nl + 1:nl + 1 + nr]
    gate_ref, o_ref, wb = refs[nl + 1 + nr:]
    i = pl.program_id(1)

    @pl.when(i == 0)
    def _():
        wb[...] = w_ref[...].astype(BF16)

    def fn(mods, g, s):
        (gate,) = mods
        acc = jnp.dot(lhs_refs[s % nl][...], wb[...], preferred_element_type=F32)
        out = res_refs[s % nr][...].reshape(g, tm // g, tn) + gate * acc.reshape(g, tm // g, tn)
        o_ref[...] = out.reshape(tm, tn)

    _grouped_apply(i, tm, [gate_ref], fn)


def _mm_resid(lhs, w3, res, mod, gate_blk, *, k, tn, tm, name):
    goff = gate_blk * (D // tn)
    l_specs, l_args = _row_operand(lhs, tm, k, lambda j, i: (i, 0))
    r_specs, r_args = _row_operand(res, tm, tn, lambda j, i: (i, j))
    return pl.pallas_call(
        functools.partial(_mm_resid_kernel, tm=tm, tn=tn, nl=len(l_args), nr=len(r_args)),
        grid=(D // tn, M // tm),
        in_specs=l_specs + [pl.BlockSpec((None, k, tn), lambda j, i: (0, 0, j))] + r_specs + [
            pl.BlockSpec((MOD_ROWS, tn), lambda j, i: (0, goff + j)),
        ],
        out_specs=pl.BlockSpec((tm, tn), lambda j, i: (i, j)),
        out_shape=jax.ShapeDtypeStruct((M, D), F32),
        scratch_shapes=[pltpu.VMEM((k, tn), BF16)],
        compiler_params=_cparams(2),
        name=name,
    )(*l_args, w3, *r_args, mod)


def _mm_swiglu_kernel(lhs_ref, wg_ref, wu_ref, o_ref, wgb, wub):
    @pl.when(pl.program_id(1) == 0)
    def _():
        wgb[...] = wg_ref[...].astype(BF16)
        wub[...] = wu_ref[...].astype(BF16)

    lhs = lhs_ref[...]
    gt = jnp.dot(lhs, wgb[...], preferred_element_type=F32)
    up = jnp.dot(lhs, wub[...], preferred_element_type=F32)
    o_ref[...] = (_silu(gt) * up).astype(o_ref.dtype)


def _mm_swiglu(lhs, w3, hidden, *, tn, tm, name):
    rows, k = lhs.shape
    nb = hidden // tn
    return pl.pallas_call(
        _mm_swiglu_kernel,
        grid=(nb, rows // tm),
        in_specs=[
            pl.BlockSpec((tm, k), lambda j, i: (i, 0)),
            pl.BlockSpec((None, k, tn), lambda j, i: (0, 0, j)),
            pl.BlockSpec((None, k, tn), lambda j, i: (0, 0, j + nb)),
        ],
        out_specs=pl.BlockSpec((tm, tn), lambda j, i: (i, j)),
        out_shape=jax.ShapeDtypeStruct((rows, hidden), BF16),
        scratch_shapes=[pltpu.VMEM((k, tn), BF16), pltpu.VMEM((k, tn), BF16)],
        compiler_params=_cparams(2),
        name=name,
    )(lhs, w3, w3)


def _expert_matmul_kernel(te_ref, first_ref, nxt_ref, misc_ref, lhs_ref, w_hbm, o_ref, stage, panel, sem, st,
                          *, tn, nj, col_offs, epilogue):
    j = pl.program_id(0)
    i = pl.program_id(1)
    nu = misc_ref[0]
    first_e = misc_ref[1]
    half = MOE_SORT
    nw = len(col_offs)

    def weight_copies(e, jj):
        c0 = pl.multiple_of(jj * tn, tn)
        return [pltpu.make_async_copy(w_hbm.at[e, :, pl.ds(off + c0, tn)], stage.at[n], sem.at[n])
                for n, off in enumerate(col_offs)]

    def land_staged():
        for cp in weight_copies(0, 0):
            cp.wait()
        spare = 1 - st[0]
        for n in range(nw):
            panel[spare, n] = stage[n].astype(BF16)
        st[1] = 0

    def change(t, very_first):
        e_new = te_ref[t]

        @pl.when(very_first)
        def _():
            st[0] = 0
            st[1] = 1
            for cp in weight_copies(e_new, j):
                cp.start()

        @pl.when(st[1] == 1)
        def _():
            land_staged()

        st[0] = 1 - st[0]
        ne = nxt_ref[e_new]
        same_col = ne >= 0

        @pl.when(jnp.logical_or(same_col, j + 1 < nj))
        def _():
            st[1] = 1
            for cp in weight_copies(jnp.where(same_col, ne, first_e), jnp.where(same_col, j, j + 1)):
                cp.start()

    def rows_dot(r0, rows):
        cur = st[0]
        lhs = lhs_ref[r0:r0 + rows, :]
        accs = [jnp.dot(lhs, panel[cur, n], preferred_element_type=F32) for n in range(nw)]
        o_ref[r0:r0 + rows, :] = epilogue(*accs).astype(o_ref.dtype)

    t0 = 2 * i
    t1 = t0 + 1
    v0 = t0 < nu
    v1 = t1 < nu
    f0 = first_ref[t0] == 1
    f1 = first_ref[t1] == 1

    @pl.when(f0)
    def _():
        change(t0, jnp.logical_and(j == 0, i == 0))

    whole = jnp.logical_and(v1, jnp.logical_not(f1))
    cast_here = jnp.logical_and(jnp.logical_not(f0), st[1] == 1)

    @pl.when(jnp.logical_and(whole, cast_here))
    def _():
        land_staged()
        rows_dot(0, 2 * half)

    @pl.when(jnp.logical_and(whole, jnp.logical_not(cast_here)))
    def _():
        rows_dot(0, 2 * half)

    @pl.when(f1)
    def _():
        rows_dot(0, half)
        change(t1, False)
        rows_dot(half, half)

    @pl.when(jnp.logical_and(v0, jnp.logical_not(v1)))
    def _():
        rows_dot(0, half)
        o_ref[half:, :] = jnp.zeros((half, tn), o_ref.dtype)

    @pl.when(jnp.logical_not(v0))
    def _():
        o_ref[...] = jnp.zeros_like(o_ref)


def _expert_matmul(plan, lhs, w, *, k, n_out, tn, col_offs, epilogue, out_dtype, name):
    te, first, nxt, misc = plan
    nj = n_out // tn
    nw = len(col_offs)
    grid_spec = pltpu.PrefetchScalarGridSpec(
        num_scalar_prefetch=4,
        grid=(nj, MOE_STEPS),
        in_specs=[
            pl.BlockSpec((2 * MOE_SORT, k), lambda j, i, *_: (i, 0)),
            pl.BlockSpec(memory_space=pl.ANY),
        ],
        out_specs=pl.BlockSpec((2 * MOE_SORT, tn), lambda j, i, *_: (i, j)),
        scratch_shapes=[pltpu.VMEM((nw, k, tn), F32), pltpu.VMEM((2, nw, k, tn), BF16),
                        pltpu.SemaphoreType.DMA((nw,)), pltpu.SMEM((2,), I32)],
    )
    return pl.pallas_call(
        functools.partial(_expert_matmul_kernel, tn=tn, nj=nj, col_offs=col_offs, epilogue=epilogue),
        grid_spec=grid_spec,
        out_shape=jax.ShapeDtypeStruct((MOE_ROWS, n_out), out_dtype),
        compiler_params=_cparams(2),
        name=name,
    )(te, first, nxt, misc, lhs, w)


def _moe_gu(plan, xs, w_gu):
    return _expert_matmul(plan, xs, w_gu, k=D, n_out=EH, tn=1024, col_offs=(0, EH),
                          epilogue=lambda gt, up: _silu(gt) * up, out_dtype=BF16, name="moe_gu")


def _moe_down(plan, act, w_down):
    return _expert_matmul(plan, act, w_down, k=EH, n_out=D, tn=512, col_offs=(0,),
                          epilogue=lambda acc: acc, out_dtype=F32, name="moe_down")


def _row_copy(src_hbm, row, buf, r, sem):
    return pltpu.make_async_copy(src_hbm.at[pl.ds(row, 1), :], buf.at[pl.ds(r, 1), :], sem)


def _start_rows(src_hbm, idx_ref, off, dst, sem, n):
    def body(p, carry):
        for q in range(2):
            r = 2 * p + q
            _row_copy(src_hbm, idx_ref[0, 0, off + r], dst, r, sem).start(priority=q)
        return carry

    lax.fori_loop(0, n // 2, body, 0, unroll=4)


def _wait_rows(src_hbm, dst, sem, n):
    pltpu.make_async_copy(src_hbm.at[pl.ds(0, n), :], dst, sem).wait()


def _gather_kernel(nu_ref, idx_ref, idx_next_ref, src_hbm, o_ref, buf, sem, *, tm):
    t = pl.program_id(0)
    nu = nu_ref[0]
    slot = lax.rem(t, 2)

    @pl.when(t == 0)
    def _():
        _start_rows(src_hbm, idx_ref, 0, buf.at[0], sem.at[0], tm)

    @pl.when(t + 1 < nu)
    def _():
        _start_rows(src_hbm, idx_next_ref, 0, buf.at[1 - slot], sem.at[1 - slot], tm)

    @pl.when(t < nu)
    def _():
        _wait_rows(src_hbm, buf.at[slot], sem.at[slot], tm)
        o_ref[...] = buf[slot].astype(o_ref.dtype)

    @pl.when(t >= nu)
    def _():
        o_ref[...] = jnp.zeros_like(o_ref)


def _gather_rows(nu, idx3, src):
    tm = 2 * MOE_SORT
    nt = MOE_STEPS
    grid_spec = pltpu.PrefetchScalarGridSpec(
        num_scalar_prefetch=1,
        grid=(nt,),
        in_specs=[
            pl.BlockSpec((1, 1, tm), lambda t, nu: (t, 0, 0), memory_space=pltpu.SMEM),
            pl.BlockSpec((1, 1, tm), lambda t, nu: (jnp.minimum(t + 1, nt - 1), 0, 0), memory_space=pltpu.SMEM),
            pl.BlockSpec(memory_space=pl.ANY),
        ],
        out_specs=pl.BlockSpec((tm, D), lambda t, nu: (t, 0)),
        scratch_shapes=[pltpu.VMEM((2, tm, D), F32), pltpu.SemaphoreType.DMA((2,))],
    )
    return pl.pallas_call(
        functools.partial(_gather_kernel, tm=tm),
        grid_spec=grid_spec,
        out_shape=jax.ShapeDtypeStruct((MOE_ROWS, D), BF16),
        compiler_params=_cparams(1),
        name="moe_gather",
    )(nu, idx3, idx3, src)


def _combine_kernel(pos_ref, pos_next_ref, ys_hbm, x_ref, gv_ref, gate_ref, fw_ref, op_ref, os_ref, buf, sem,
                    *, tm, nt):
    t = pl.program_id(0)
    slot = lax.rem(t, 2)

    def start(idx_ref, s):
        _start_rows(ys_hbm, idx_ref, 0, buf.at[s, 0], sem.at[s, 0], tm)
        _start_rows(ys_hbm, idx_ref, tm, buf.at[s, 1], sem.at[s, 1], tm)

    @pl.when(t == 0)
    def _():
        start(pos_ref, 0)

    @pl.when(t + 1 < nt)
    def _():
        start(pos_next_ref, 1 - slot)

    _wait_rows(ys_hbm, buf.at[slot, 0], sem.at[slot, 0], tm)
    _wait_rows(ys_hbm, buf.at[slot, 1], sem.at[slot, 1], tm)

    def fn(mods, g, s):
        (gate,) = mods
        gv = gv_ref[...]
        f = gv[:, 0:1] * buf[slot, 0] + gv[:, 1:2] * buf[slot, 1]
        x = x_ref[...].reshape(g, tm // g, D) + gate * f.reshape(g, tm // g, D)
        x = x.reshape(tm, D)
        ms = jnp.mean(x * x, axis=-1, keepdims=True)
        (op_ref, os_ref)[s][...] = x * lax.rsqrt(ms + EPS) * fw_ref[...]

    _grouped_apply(t, tm, [gate_ref], fn)


def _combine(pos3, ys, x, gv, mod, gate_blk, final_w, tm=256):
    nt = M // tm
    n_pt = MP // tm
    return pl.pallas_call(
        functools.partial(_combine_kernel, tm=tm, nt=nt),
        grid=(nt,),
        in_specs=[
            pl.BlockSpec((1, 1, 2 * tm), lambda t: (t, 0, 0), memory_space=pltpu.SMEM),
            pl.BlockSpec((1, 1, 2 * tm), lambda t: (jnp.minimum(t + 1, nt - 1), 0, 0), memory_space=pltpu.SMEM),
            pl.BlockSpec(memory_space=pl.ANY),
            pl.BlockSpec((tm, D), lambda t: (t, 0)),
            pl.BlockSpec((tm, 128), lambda t: (t, 0)),
            pl.BlockSpec((MOD_ROWS, D), lambda t: (0, gate_blk)),
            pl.BlockSpec((1, D), lambda t: (0, 0)),
        ],
        out_specs=[
            pl.BlockSpec((tm, D), lambda t: (jnp.minimum(t, n_pt - 1), 0)),
            pl.BlockSpec((tm, D), lambda t: (jnp.maximum(t - n_pt, 0), 0)),
        ],
        out_shape=[jax.ShapeDtypeStruct((MP, D), F32), jax.ShapeDtypeStruct((MS, D), F32)],
        scratch_shapes=[pltpu.VMEM((2, 2, tm, D), F32), pltpu.SemaphoreType.DMA((2, 2))],
        compiler_params=_cparams(1),
        name="moe_combine",
    )(pos3, pos3, ys, x, gv, mod, final_w)


def _ssd_kernel(z_ref, xbc_ref, dt_ref, cs_ref, h0_ref, cw_ref, cb_ref, dtb_ref, alog_ref, dsk_ref, nw_ref,
                e2_ref, et2_ref, y_ref, csn_ref, hn_ref, cbuf, hst, *, C, nc):
    c = pl.program_id(1)
    hp = 128 // C
    neg = jnp.float32(-jnp.inf)

    @pl.when(c == 0)
    def _():
        cbuf[0:8, :] = jnp.zeros((8, SSD_XBC), F32)
        cbuf[5:8, :] = cs_ref[...]
        hst[...] = h0_ref[...]

    cbuf[8:8 + C, :] = xbc_ref[...]

    @pl.when(c == nc - 1)
    def _():
        csn_ref[...] = cbuf[C + 5:C + 8, :]

    dt = _softplus(dt_ref[...] + dtb_ref[...])
    a = -jnp.exp(alog_ref[...]) * LOG2E
    row = lax.broadcasted_iota(I32, (C, C), 0)
    col = lax.broadcasted_iota(I32, (C, C), 1)
    tril = jnp.where(row >= col, 1.0, 0.0).astype(BF16)
    b = _dot_exact_rhs(tril, dt * a)
    eb = jnp.exp2(b)
    b_last = b[C - 1:C, :]
    dl = dt * jnp.exp2(b_last - b)
    stack = jnp.concatenate([dt, eb, dl], axis=0)
    s_hi = stack.astype(BF16)
    s_lo = (stack - s_hi.astype(F32)).astype(BF16)
    lhs2 = jnp.concatenate([s_hi, s_lo], axis=1)
    bst = jnp.concatenate([b] + [pltpu.roll(b, 128 - k, axis=1) for k in range(1, hp)], axis=0)
    bt = bst.T
    rdec = jnp.broadcast_to(jnp.exp2(bt[:, C - 1:C]), (128, 128))
    r_hi = rdec.astype(BF16)
    r_lo = (rdec - r_hi.astype(F32)).astype(BF16)
    r2 = jnp.concatenate([r_hi, r_lo], axis=0)

    lane = lax.broadcasted_iota(I32, (C, 128), 1)
    trow = lax.broadcasted_iota(I32, (C, 128), 0)
    causal = trow >= (lane & (C - 1))
    lane_seg = lane >> (C.bit_length() - 1)
    xlane_seg = lax.broadcasted_iota(I32, (C, hp * SSD_P), 1) >> (SSD_P.bit_length() - 1)

    def conv(cols):
        v = cbuf[0:C + 8, cols]
        acc = cb_ref[:, cols] + v[8:8 + C] * cw_ref[SSD_K - 1:SSD_K, cols]
        for k in range(SSD_K - 1):
            acc = acc + pltpu.roll(v, C + 3 - k, axis=0)[0:C] * cw_ref[k:k + 1, cols]
        return _silu(acc)

    for g in range(SSD_G):
        xc = slice(g * SSD_GW, (g + 1) * SSD_GW)
        xg = conv(xc)
        bg = conv(slice(SSD_DI + g * SSD_N, SSD_DI + (g + 1) * SSD_N)).astype(BF16)
        cg = conv(slice(SSD_DI + SSD_G * SSD_N + g * SSD_N, SSD_DI + SSD_G * SSD_N + (g + 1) * SSD_N)).astype(BF16)
        ex = jnp.dot(lhs2, e2_ref[:, xc], preferred_element_type=F32)
        dt_e, eb_e, dl_e = ex[0:C], ex[C:2 * C], ex[2 * C:3 * C]
        xdt = xg * dt_e
        cb2 = lax.dot_general(cg, jnp.concatenate([bg] * hp, axis=0), NT_DIMS, preferred_element_type=F32)
        hg = hst[g * SSD_GW:(g + 1) * SSD_GW, :]
        y_inter = lax.dot_general(cg, hg.astype(BF16), NT_DIMS, preferred_element_type=F32)
        ys = []
        for p in range(8 // hp):
            h0 = g * 8 + p * hp
            bcol = jnp.broadcast_to(b[:, h0:h0 + 1], (C, 128))
            for k in range(1, hp):
                bcol = jnp.where(lane_seg >= k, b[:, h0 + k:h0 + k + 1], bcol)
            lm = jnp.exp2(jnp.where(causal, bcol - bt[h0:h0 + 1, :], neg))
            m2 = (cb2 * lm).astype(BF16)
            xp = xdt[:, p * hp * SSD_P:(p + 1) * hp * SSD_P]
            rhs = jnp.concatenate([jnp.where(xlane_seg == k, xp, 0.0) for k in range(hp)], axis=0).astype(BF16)
            ys.append(jnp.dot(m2, rhs, preferred_element_type=F32))
        y = jnp.concatenate(ys, axis=1) + y_inter * eb_e + xg * dsk_ref[:, xc]
        y = y * _silu(z_ref[:, xc])
        ms = jnp.mean(y * y, axis=-1, keepdims=True)
        y_ref[:, xc] = (y * lax.rsqrt(ms + EPS) * nw_ref[:, xc]).astype(y_ref.dtype)
        upd = lax.dot_general((xg * dl_e).astype(BF16), bg, TN_DIMS, preferred_element_type=F32)
        dec = jnp.dot(et2_ref[g * SSD_GW:(g + 1) * SSD_GW, :], r2, preferred_element_type=F32)
        hst[g * SSD_GW:(g + 1) * SSD_GW, :] = hg * dec + upd

    cbuf[5:8, :] = cbuf[C + 5:C + 8, :]

    @pl.when(c == nc - 1)
    def _():
        hn_ref[...] = hst[...]


def _ssd_scan(z, xbc, dtr, conv_state, h0, consts, *, nb, seq, C, row0):
    nc = seq // C
    rb0 = row0 // C
    cw, cb, dtb, alog, dsk, nw, e2, et2 = consts
    rows = lambda b, c: (rb0 + b * nc + c, 0)
    fixed = lambda b, c: (0, 0)
    per_b = lambda b, c: (b, 0, 0)
    return pl.pallas_call(
        functools.partial(_ssd_kernel, C=C, nc=nc),
        grid=(nb, nc),
        in_specs=[
            pl.BlockSpec((C, SSD_DI), rows),
            pl.BlockSpec((C, SSD_XBC), rows),
            pl.BlockSpec((C, 128), rows),
            pl.BlockSpec((None, SSD_K - 1, SSD_XBC), per_b),
            pl.BlockSpec((None, SSD_DI, SSD_N), per_b),
            pl.BlockSpec((SSD_K, SSD_XBC), fixed),
            pl.BlockSpec((1, SSD_XBC), fixed),
            pl.BlockSpec((1, 128), fixed),
            pl.BlockSpec((1, 128), fixed),
            pl.BlockSpec((1, SSD_DI), fixed),
            pl.BlockSpec((1, SSD_DI), fixed),
            pl.BlockSpec((256, SSD_DI), fixed),
            pl.BlockSpec((SSD_DI, 256), fixed),
        ],
        out_specs=[
            pl.BlockSpec((C, SSD_DI), lambda b, c: (b * nc + c, 0)),
            pl.BlockSpec((None, SSD_K - 1, SSD_XBC), per_b),
            pl.BlockSpec((None, SSD_DI, SSD_N), per_b),
        ],
        out_shape=[
            jax.ShapeDtypeStruct((nb * seq, SSD_DI), BF16),
            jax.ShapeDtypeStruct((nb, SSD_K - 1, SSD_XBC), F32),
            jax.ShapeDtypeStruct((nb, SSD_DI, SSD_N), F32),
        ],
        scratch_shapes=[pltpu.VMEM((8 + C, SSD_XBC), F32), pltpu.VMEM((SSD_DI, SSD_N), F32)],
        compiler_params=_cparams(2),
        name=f"ssd_scan_c{C}",
    )(z, xbc, dtr, conv_state, h0, cw, cb, dtb, alog, dsk, nw, e2, et2)


def _gla_kernel(q_ref, f_ref, v_ref, g_ref, s0_ref, lb_ref, nw_ref, o_ref, sn_ref, st, b_s, c_s, *, T, nc):
    c = pl.program_id(1)
    neg = jnp.float32(-jnp.inf)

    @pl.when(c == 0)
    def _():
        for h in range(HG_H):
            st[h * 128:(h + 1) * 128, :] = s0_ref[h * 128:(h + 1) * 128, :].T

    row = lax.broadcasted_iota(I32, (T, T), 0)
    col = lax.broadcasted_iota(I32, (T, T), 1)
    sh = HG_BLK.bit_length() - 1
    same_blk = (row >> sh) == (col >> sh)
    bd = jnp.where(jnp.logical_and(same_blk, row >= col), 1.0, 0.0).astype(BF16)

    for h in range(HG_H):
        hl = slice(h * 128, (h + 1) * 128)
        fz = f_ref[:, hl]
        lb = lb_ref[:, hl]
        ls = jnp.minimum(fz, 0.0) - _log1p_unit(jnp.exp(-jnp.abs(fz)))
        a1 = jnp.log(lb)
        a2 = jnp.log1p(-lb) + ls
        lf = jnp.maximum(a1, a2) + _log1p_unit(jnp.exp(-jnp.abs(a1 - a2)))
        b2 = _dot_exact_rhs(bd, lf * LOG2E)
        b_s[:, hl] = b2
        c_s[:, hl] = b2 - (a2 - fz) * LOG2E

    trow = lax.broadcasted_iota(I32, (8, 128), 0)
    ones = jnp.ones((128, 128), BF16)
    hb = HG_BLK // 2

    def blk(j, carry):
        rows = pl.ds(pl.multiple_of(j * HG_BLK, HG_BLK), HG_BLK)
        for h in range(HG_H):
            hl = slice(h * 128, (h + 1) * 128)
            qb = q_ref[rows, hl]
            bb = b_s[rows, hl]
            cb = c_s[rows, hl]
            vb = v_ref[rows, hl]
            sth = st[hl, :]
            o = lax.dot_general((qb * jnp.exp2(bb)).astype(BF16), sth.astype(BF16), NT_DIMS,
                                preferred_element_type=F32)
            q_h, b_h = (qb[0:hb], qb[hb:]), (bb[0:hb], bb[hb:])
            pieces, owner = [], []
            for s in range(HG_BLK):
                cs = cb[s:s + 1, :]
                for half in range(2):
                    if s >= hb * (half + 1):
                        continue
                    e = b_h[half] - cs
                    if s >= hb * half:
                        e = jnp.where(trow >= s - hb * half, e, neg)
                    pieces.append(q_h[half] * jnp.exp2(e))
                    owner.append((s, half))
            att = jnp.dot(jnp.concatenate(pieces, axis=0).astype(BF16), ones, preferred_element_type=F32)
            o_h = [o[0:hb], o[hb:]]
            for n, (s, half) in enumerate(owner):
                o_h[half] = o_h[half] + att[n * hb:(n + 1) * hb] * vb[s:s + 1, :]
            o = jnp.concatenate(o_h, axis=0)
            b_end = bb[HG_BLK - 1:HG_BLK, :]
            khat = jnp.exp2(b_end - cb).astype(BF16)
            upd = lax.dot_general(vb.astype(BF16), khat, TN_DIMS, preferred_element_type=F32)
            st[hl, :] = sth * jnp.exp2(b_end) + upd
            ms = jnp.mean(o * o, axis=-1, keepdims=True)
            o_ref[rows, hl] = (o * lax.rsqrt(ms + EPS) * nw_ref[...] * _silu(g_ref[rows, hl])).astype(o_ref.dtype)
        return carry

    lax.fori_loop(0, T // HG_BLK, blk, 0)

    @pl.when(c == nc - 1)
    def _():
        for h in range(HG_H):
            sn_ref[h * 128:(h + 1) * 128, :] = st[h * 128:(h + 1) * 128, :].T


def _gla_scan(qfig, s0, lb, nw, *, nb, seq, T, row0):
    nc = seq // T
    rb0 = row0 // T
    per_b = lambda b, c: (b, 0, 0)

    def cols(k):
        return pl.BlockSpec((T, D), lambda b, c: (rb0 + b * nc + c, k))

    return pl.pallas_call(
        functools.partial(_gla_kernel, T=T, nc=nc),
        grid=(nb, nc),
        in_specs=[
            cols(0), cols(1), cols(2), cols(3),
            pl.BlockSpec((None, D, HG_DV), per_b),
            pl.BlockSpec((1, D), lambda b, c: (0, 0)),
            pl.BlockSpec((1, HG_DV), lambda b, c: (0, 0)),
        ],
        out_specs=[
            pl.BlockSpec((T, D), lambda b, c: (b * nc + c, 0)),
            pl.BlockSpec((None, D, HG_DV), per_b),
        ],
        out_shape=[
            jax.ShapeDtypeStruct((nb * seq, D), BF16),
            jax.ShapeDtypeStruct((nb, D, HG_DV), F32),
        ],
        scratch_shapes=[pltpu.VMEM((D, HG_DK), F32), pltpu.VMEM((T, D), F32), pltpu.VMEM((T, D), F32)],
        compiler_params=_cparams(2),
        name=f"gla_scan_t{T}",
    )(qfig, qfig, qfig, qfig, s0, lb, nw)


def _route_plan(top_i):
    e_flat = top_i.reshape(-1)
    onehot = (e_flat[:, None] == jnp.arange(NE, dtype=I32)[None, :]).astype(I32)
    csum = jnp.cumsum(onehot, axis=0)
    rank = jnp.sum(onehot * (csum - 1), axis=1)
    cnt = csum[-1]
    ntile = (cnt + MOE_SORT - 1) // MOE_SORT
    cum_t = jnp.cumsum(ntile)
    tile0 = cum_t - ntile
    pos = jnp.sum(onehot * tile0[None, :], axis=1) * MOE_SORT + rank
    n_used = cum_t[-1]
    t_ids = jnp.arange(MOE_TILES, dtype=I32)
    te = jnp.sum((t_ids[:, None] >= cum_t[None, :]).astype(I32), axis=1)
    te_last = jnp.sum((n_used - 1 >= cum_t).astype(I32))
    te = jnp.where(t_ids < n_used, te, te_last).astype(I32)
    prev = jnp.concatenate([jnp.full((1,), -1, I32), te[:-1]])
    first = jnp.logical_and(t_ids < n_used, te != prev).astype(I32)
    e_ids = jnp.arange(NE, dtype=I32)
    later = jnp.logical_and(e_ids[None, :] > e_ids[:, None], ntile[None, :] > 0)
    nxt = jnp.min(jnp.where(later, e_ids[None, :], NE), axis=1)
    nxt = jnp.where(nxt == NE, -1, nxt).astype(I32)
    misc = jnp.stack([n_used, te[0]]).astype(I32)
    tok = jnp.arange(2 * M, dtype=I32) // 2
    src = jnp.zeros((MOE_ROWS,), I32).at[pos].set(tok)
    n_steps = ((n_used + 1) // 2).reshape(1).astype(I32)
    return (te, first, nxt, misc), n_steps, src, pos.reshape(M, 2)


def kernel(x_prompt, x_sample, c_prompt, c_sample, state_ssd_conv, state_ssd, state_hgrn, ada_w, ada_b, norm_w,
           ssd_w_in, ssd_conv_w, ssd_conv_b, ssd_dt_bias, ssd_a_log, ssd_d, ssd_norm_w, ssd_w_out, hgrn_w_in,
           hgrn_lb_logits, hgrn_norm_w, hgrn_w_out, ffn_w_gu, ffn_w_down, moe_router, moe_w_gu, moe_w_down,
           final_norm_w):
    x = (x_prompt.reshape(MP, D), x_sample.reshape(MS, D))
    c_all = jnp.concatenate([c_prompt, jnp.zeros((MOD_S0 - NP, D), F32), c_sample], axis=0)
    norm_w4 = norm_w.reshape(4, 1, D)

    mods = [
        _mm_plain(c_all, ada_w, layer=l, col0=0, n_out=6 * D, tn=1024, tm=MOD_ROWS,
                  bias=ada_b[l].reshape(1, 6 * D), act_lhs=True, name="adaln")
        for l in range(2)
    ]

    hn = _prenorm(x, norm_w4, 0, mods[0], 0, 1, BF16)
    z = _mm_plain(hn, ssd_w_in, layer=0, col0=0, n_out=SSD_DI, tn=1024, tm=1024, name="ssd_in_z")
    xbc = _mm_plain(hn, ssd_w_in, layer=0, col0=SSD_DI, n_out=SSD_XBC, tn=1024, tm=1024, name="ssd_in_xbc")
    dtr = _mm_plain(hn, ssd_w_in, layer=0, col0=SSD_DI + SSD_XBC, n_out=128, tn=128, tm=1024, name="ssd_in_dt")

    pad64 = lambda v: jnp.pad(v.reshape(1, SSD_HEADS), ((0, 0), (0, 128 - SSD_HEADS)))
    head_of = jnp.arange(SSD_DI, dtype=I32) // SSD_P
    e1 = (jnp.arange(128, dtype=I32)[:, None] == head_of[None, :]).astype(BF16)
    e2 = jnp.concatenate([e1, e1], axis=0)
    et2 = jnp.concatenate([e1.T, e1.T], axis=1)
    consts = (ssd_conv_w[0], ssd_conv_b[0].reshape(1, SSD_XBC), pad64(ssd_dt_bias[0]), pad64(ssd_a_log[0]),
              jnp.repeat(ssd_d[0], SSD_P).reshape(1, SSD_DI), ssd_norm_w[0].reshape(1, SSD_DI), e2, et2)
    yp, p_conv, p_ssd = _ssd_scan(z, xbc, dtr, jnp.zeros((NP, SSD_K - 1, SSD_XBC), F32),
                                  jnp.zeros((NP, SSD_DI, SSD_N), F32), consts, nb=NP, seq=LP, C=64, row0=0)
    ysm, s_conv, s_ssd = _ssd_scan(z, xbc, dtr, state_ssd_conv[0], state_ssd[0].reshape(NS, SSD_DI, SSD_N), consts,
                                   nb=NS, seq=LS, C=32, row0=MP)
    x = _mm_resid((yp, ysm), ssd_w_out, x, mods[0], 2, k=SSD_DI, tn=512, tm=512, name="ssd_out")

    hn = _prenorm(x, norm_w4, 1, mods[0], 3, 4, BF16)
    act = _mm_swiglu(hn, ffn_w_gu, FFN, tn=512, tm=1024, name="ffn_gu")
    x = _mm_resid(act, ffn_w_down, x, mods[0], 5, k=FFN, tn=512, tm=512, name="ffn_down")

    hn = _prenorm(x, norm_w4, 2, mods[1], 0, 1, BF16)
    qfig = _mm_plain(hn, hgrn_w_in, layer=0, col0=0, n_out=4 * D, tn=1024, tm=1024, name="hgrn_in")
    p = jax.nn.softmax(hgrn_lb_logits.astype(F32), axis=0)
    lb = (jnp.cumsum(p, axis=0) - p[0])[1].reshape(1, D)
    nw_h = hgrn_norm_w[0].reshape(1, HG_DV)
    op, p_hgrn = _gla_scan(qfig, jnp.zeros((NP, D, HG_DV), F32), lb, nw_h, nb=NP, seq=LP, T=64, row0=0)
    osm, s_hgrn = _gla_scan(qfig, state_hgrn[0].reshape(NS, D, HG_DV), lb, nw_h, nb=NS, seq=LS, T=32, row0=MP)
    x = _mm_resid((op, osm), hgrn_w_out, x, mods[1], 2, k=D, tn=1024, tm=512, name="hgrn_out")

    hn32, gv, gi = _prenorm(x, norm_w4, 3, mods[1], 3, 4, F32,
                            router_pad=jnp.pad(moe_router[0], ((0, 0), (0, 128 - NE))))
    plan, n_steps, src, pos = _route_plan(gi[:, :2])
    xs = _gather_rows(n_steps, src.reshape(MOE_STEPS, 1, 2 * MOE_SORT), hn32)
    act = _moe_gu(plan, xs, moe_w_gu[0])
    ys = _moe_down(plan, act, moe_w_down[0])
    tmc = 256
    pos3 = jnp.concatenate([pos[:, 0].reshape(M // tmc, 1, tmc), pos[:, 1].reshape(M // tmc, 1, tmc)], axis=2)
    yo_p, yo_s = _combine(pos3, ys, x, gv, mods[1], 5, final_norm_w.reshape(1, D), tm=tmc)

    return (
        yo_p.reshape(NP, LP, D),
        yo_s.reshape(NS, LS, D),
        p_conv[None],
        p_ssd.reshape(1, NP, SSD_HEADS, SSD_P, SSD_N),
        p_hgrn.reshape(1, NP, HG_H, HG_DK, HG_DV),
        s_conv[None],
        s_ssd.reshape(1, NS, SSD_HEADS, SSD_P, SSD_N),
        s_hgrn.reshape(1, NS, HG_H, HG_DK, HG_DV),
    )
```

```python
import functools

import jax
import jax.numpy as jnp
from jax import lax
from jax.experimental import pallas as pl
from jax.experimental.pallas import tpu as pltpu

F32 = jnp.float32
BF16 = jnp.bfloat16
I32 = jnp.int32

EPS = 1e-6
LOG2E = 1.4426950408889634
D = 2048
NP, LP = 4, 2048
NS, LS = 32, 32
MP, MS = NP * LP, NS * LS
M = MP + MS
MOD_S0 = 8
MOD_ROWS = MOD_S0 + NS

SSD_DI = 4096
SSD_HEADS = 64
SSD_P = 64
SSD_G = 8
SSD_GW = SSD_DI // SSD_G
SSD_N = 128
SSD_XBC = SSD_DI + 2 * SSD_G * SSD_N
SSD_K = 4

HG_H = 16
HG_DK = 128
HG_DV = 128
HG_BLK = 16

FFN = 5632
NE = 8
EH = 7168
MOE_SORT = 256
MOE_TILES = (2 * M) // MOE_SORT + NE
MOE_STEPS = MOE_TILES // 2
MOE_ROWS = MOE_TILES * MOE_SORT

VMEM_LIMIT = 56 * 1024 * 1024

NT_DIMS = (((1,), (1,)), ((), ()))
TN_DIMS = (((0,), (0,)), ((), ()))


def _cparams(n_axes):
    return pltpu.CompilerParams(dimension_semantics=("arbitrary",) * n_axes,
                                vmem_limit_bytes=VMEM_LIMIT)


def _silu(x):
    h = 0.5 * x
    return h + h * jnp.tanh(h)


def _softplus(x):
    return jnp.maximum(x, 0.0) + jnp.log1p(jnp.exp(-jnp.abs(x)))


def _log1p_unit(u):
    return jnp.log(1.0 + u)


def _split3(x):
    hi = x.astype(BF16)
    r1 = x - hi.astype(F32)
    mid = r1.astype(BF16)
    lo = (r1 - mid.astype(F32)).astype(BF16)
    return hi, mid, lo


def _dot_exact_rhs(mat_bf16, x):
    hi, mid, lo = _split3(x)
    acc = jnp.dot(mat_bf16, hi, preferred_element_type=F32)
    acc = acc + jnp.dot(mat_bf16, mid, preferred_element_type=F32)
    return acc + jnp.dot(mat_bf16, lo, preferred_element_type=F32)


def _grouped_apply(i, tm, mod_refs, fn):
    n_pt = MP // tm
    tpb = LP // tm

    @pl.when(i < n_pt)
    def _():
        r = i // tpb
        fn([m[pl.ds(r, 1), :][None] for m in mod_refs], 1, 0)

    @pl.when(i >= n_pt)
    def _():
        g = tm // LS
        start = pl.multiple_of(MOD_S0 + (i - n_pt) * g, 8)
        fn([m[pl.ds(start, g), :][:, None, :] for m in mod_refs], g, 1)


def _row_operand(op, tm, width, ij):
    if not isinstance(op, tuple):
        return [pl.BlockSpec((tm, width), lambda *g: ij(*g))], [op]
    n_pt = MP // tm
    return [
        pl.BlockSpec((tm, width), lambda *g: (jnp.minimum(ij(*g)[0], n_pt - 1), ij(*g)[1])),
        pl.BlockSpec((tm, width), lambda *g: (jnp.maximum(ij(*g)[0] - n_pt, 0), ij(*g)[1])),
    ], list(op)


def _top2_gates(logits):
    lane = lax.broadcasted_iota(I32, logits.shape, 1)
    neg = jnp.float32(-jnp.inf)
    l1 = jnp.where(lane < NE, logits, neg)
    m1 = jnp.max(l1, axis=-1, keepdims=True)
    i1 = jnp.min(jnp.where(l1 == m1, lane, 128), axis=-1, keepdims=True)
    l2 = jnp.where(lane == i1, neg, l1)
    m2 = jnp.max(l2, axis=-1, keepdims=True)
    i2 = jnp.min(jnp.where(l2 == m2, lane, 128), axis=-1, keepdims=True)
    e = jnp.exp(m2 - m1)
    g1 = 1.0 / (1.0 + e)
    g2 = e * g1
    gv = jnp.where(lane == 0, g1, jnp.where(lane == 1, g2, 0.0))
    gi = jnp.where(lane == 0, i1, jnp.where(lane == 1, i2, 0))
    return gv, gi


def _prenorm_kernel(*refs, tm, nx, route):
    x_refs = refs[:nx]
    if route:
        nw_ref, sc_ref, sh_ref, r_ref, o_ref, gv_ref, gi_ref = refs[nx:]
    else:
        nw_ref, sc_ref, sh_ref, o_ref = refs[nx:]
    i = pl.program_id(0)

    def fn(mods, g, s):
        sc, sh = mods
        x = x_refs[s % nx][...]
        ms = jnp.mean(x * x, axis=-1, keepdims=True)
        y = x * lax.rsqrt(ms + EPS) * nw_ref[...]
        y = (y.reshape(g, tm // g, D) * (1.0 + sc) + sh).reshape(tm, D)
        o_ref[...] = y.astype(o_ref.dtype)
        if route:
            r = r_ref[...]
            y_hi, r_hi = y.astype(BF16), r.astype(BF16)
            y_lo = (y - y_hi.astype(F32)).astype(BF16)
            r_lo = (r - r_hi.astype(F32)).astype(BF16)
            logits = (jnp.dot(y_hi, r_hi, preferred_element_type=F32)
                      + jnp.dot(y_lo, r_hi, preferred_element_type=F32)
                      + jnp.dot(y_hi, r_lo, preferred_element_type=F32))
            gv_ref[...], gi_ref[...] = _top2_gates(logits)

    _grouped_apply(i, tm, [sc_ref, sh_ref], fn)


def _prenorm(x, norm_w4, k, mod, sh_blk, sc_blk, out_dtype, tm=512, router_pad=None):
    x_specs, x_args = _row_operand(x, tm, D, lambda i: (i, 0))
    route = router_pad is not None
    row_out = lambda w: pl.BlockSpec((tm, w), lambda i: (i, 0))
    out_specs = [row_out(D)]
    out_shape = [jax.ShapeDtypeStruct((M, D), out_dtype)]
    extra_specs, extra_args = [], []
    if route:
        extra_specs = [pl.BlockSpec((D, 128), lambda i: (0, 0))]
        extra_args = [router_pad]
        out_specs += [row_out(128), row_out(128)]
        out_shape += [jax.ShapeDtypeStruct((M, 128), F32), jax.ShapeDtypeStruct((M, 128), I32)]
    res = pl.pallas_call(
        functools.partial(_prenorm_kernel, tm=tm, nx=len(x_args), route=route),
        grid=(M // tm,),
        in_specs=x_specs + [
            pl.BlockSpec((None, 1, D), lambda i: (k, 0, 0)),
            pl.BlockSpec((MOD_ROWS, D), lambda i: (0, sc_blk)),
            pl.BlockSpec((MOD_ROWS, D), lambda i: (0, sh_blk)),
        ] + extra_specs,
        out_specs=out_specs,
        out_shape=out_shape,
        compiler_params=_cparams(1),
        name="prenorm_route" if route else "prenorm",
    )(*x_args, norm_w4, mod, mod, *extra_args)
    return res if route else res[0]


def _mm_plain_kernel(*refs, act_lhs, has_bias, valid_cols, w_t):
    if has_bias:
        lhs_ref, w_ref, b_ref, o_ref, wb = refs
    else:
        lhs_ref, w_ref, o_ref, wb = refs

    @pl.when(pl.program_id(1) == 0)
    def _():
        w = w_ref[...]
        if valid_cols is not None:
            w = jnp.where(lax.broadcasted_iota(I32, w.shape, 0 if w_t else 1) < valid_cols, w, 0.0)
        wb[...] = w.astype(BF16)

    lhs = lhs_ref[...]
    if act_lhs:
        lhs = _silu(lhs).astype(BF16)
    if w_t:
        acc = lax.dot_general(lhs, wb[...], NT_DIMS, preferred_element_type=F32)
    else:
        acc = jnp.dot(lhs, wb[...], preferred_element_type=F32)
    if has_bias:
        acc = acc + b_ref[...]
    o_ref[...] = acc.astype(o_ref.dtype)


def _mm_plain(lhs, w3, *, layer, col0, n_out, tn, tm, out_dtype=F32, bias=None, act_lhs=False, w_t=False,
              name="mm"):
    rows, k = lhs.shape
    off = col0 // tn
    w_cols = w3.shape[1 if w_t else 2] - col0
    valid_cols = w_cols if w_cols < n_out else None
    if w_t:
        w_spec = pl.BlockSpec((None, tn, k), lambda j, i: (layer, j + off, 0))
    else:
        w_spec = pl.BlockSpec((None, k, tn), lambda j, i: (layer, 0, j + off))
    in_specs = [pl.BlockSpec((tm, k), lambda j, i: (i, 0)), w_spec]
    args = [lhs, w3]
    if bias is not None:
        in_specs.append(pl.BlockSpec((1, tn), lambda j, i: (0, j)))
        args.append(bias)
    return pl.pallas_call(
        functools.partial(_mm_plain_kernel, act_lhs=act_lhs, has_bias=bias is not None, valid_cols=valid_cols,
                          w_t=w_t),
        grid=(n_out // tn, rows // tm),
        in_specs=in_specs,
        out_specs=pl.BlockSpec((tm, tn), lambda j, i: (i, j)),
        out_shape=jax.ShapeDtypeStruct((rows, n_out), out_dtype),
        scratch_shapes=[pltpu.VMEM((tn, k) if w_t else (k, tn), BF16)],
        compiler_params=_cparams(2),
        name=name,
    )(*args)


def _mm_resid_kernel(*refs, tm, tn, nl, nr):
    lhs_refs = refs[:nl]
    w_ref = refs[nl]
    res_refs = refs[nl + 1:nl + 1 + nr]
    gate_ref, o_ref, wb = refs[nl + 1 + nr:]
    i = pl.program_id(1)

    @pl.when(i == 0)
    def _():
        wb[...] = w_ref[...].astype(BF16)

    def fn(mods, g, s):
        (gate,) = mods
        acc = jnp.dot(lhs_refs[s % nl][...], wb[...], preferred_element_type=F32)
        out = res_refs[s % nr][...].reshape(g, tm // g, tn) + gate * acc.reshape(g, tm // g, tn)
        o_ref[...] = out.reshape(tm, tn)

    _grouped_apply(i, tm, [gate_ref], fn)


def _mm_resid(lhs, w3, res, mod, gate_blk, *, k, tn, tm, name):
    goff = gate_blk * (D // tn)
    l_specs, l_args = _row_operand(lhs, tm, k, lambda j, i: (i, 0))
    r_specs, r_args = _row_operand(res, tm, tn, lambda j, i: (i, j))
    return pl.pallas_call(
        functools.partial(_mm_resid_kernel, tm=tm, tn=tn, nl=len(l_args), nr=len(r_args)),
        grid=(D // tn, M // tm),
        in_specs=l_specs + [pl.BlockSpec((None, k, tn), lambda j, i: (0, 0, j))] + r_specs + [
            pl.BlockSpec((MOD_ROWS, tn), lambda j, i: (0, goff + j)),
        ],
        out_specs=pl.BlockSpec((tm, tn), lambda j, i: (i, j)),
        out_shape=jax.ShapeDtypeStruct((M, D), F32),
        scratch_shapes=[pltpu.VMEM((k, tn), BF16)],
        compiler_params=_cparams(2),
        name=name,
    )(*l_args, w3, *r_args, mod)


def _mm_swiglu_kernel(lhs_ref, wg_ref, wu_ref, o_ref, wgb, wub):
    @pl.when(pl.program_id(1) == 0)
    def _():
        wgb[...] = wg_ref[...].astype(BF16)
        wub[...] = wu_ref[...].astype(BF16)

    lhs = lhs_ref[...]
    gt = jnp.dot(lhs, wgb[...], preferred_element_type=F32)
    up = jnp.dot(lhs, wub[...], preferred_element_type=F32)
    o_ref[...] = (_silu(gt) * up).astype(o_ref.dtype)


def _mm_swiglu(lhs, w3, hidden, *, tn, tm, name):
    rows, k = lhs.shape
    nb = hidden // tn
    return pl.pallas_call(
        _mm_swiglu_kernel,
        grid=(nb, rows // tm),
        in_specs=[
            pl.BlockSpec((tm, k), lambda j, i: (i, 0)),
            pl.BlockSpec((None, k, tn), lambda j, i: (0, 0, j)),
            pl.BlockSpec((None, k, tn), lambda j, i: (0, 0, j + nb)),
        ],
        out_specs=pl.BlockSpec((tm, tn), lambda j, i: (i, j)),
        out_shape=jax.ShapeDtypeStruct((rows, hidden), BF16),
        scratch_shapes=[pltpu.VMEM((k, tn), BF16), pltpu.VMEM((k, tn), BF16)],
        compiler_params=_cparams(2),
        name=name,
    )(lhs, w3, w3)


def _expert_matmul_kernel(te_ref, first_ref, nxt_ref, misc_ref, lhs_ref, w_hbm, o_ref, stage, panel, sem,
                          *, tn, nj, col_offs, epilogue):
    j = pl.program_id(0)
    i = pl.program_id(1)
    nu = misc_ref[0]
    first_e = misc_ref[1]
    half = MOE_SORT
    nw = len(col_offs)

    def weight_copies(e, jj):
        c0 = pl.multiple_of(jj * tn, 128)
        return [pltpu.make_async_copy(w_hbm.at[e, :, pl.ds(off + c0, tn)], stage.at[n], sem.at[n])
                for n, off in enumerate(col_offs)]

    def change(t, very_first):
        e_new = te_ref[t]

        @pl.when(very_first)
        def _():
            for cp in weight_copies(e_new, j):
                cp.start()

        for cp in weight_copies(e_new, j):
            cp.wait()
        for n in range(nw):
            panel[n] = stage[n].astype(BF16)
        ne = nxt_ref[e_new]
        same_col = ne >= 0

        @pl.when(jnp.logical_or(same_col, j + 1 < nj))
        def _():
            for cp in weight_copies(jnp.where(same_col, ne, first_e), jnp.where(same_col, j, j + 1)):
                cp.start()

    def rows_dot(r0, rows):
        lhs = lhs_ref[r0:r0 + rows, :]
        accs = [jnp.dot(lhs, panel[n], preferred_element_type=F32) for n in range(nw)]
        o_ref[r0:r0 + rows, :] = epilogue(*accs).astype(o_ref.dtype)

    t0 = 2 * i
    t1 = t0 + 1
    v0 = t0 < nu
    v1 = t1 < nu
    f0 = first_ref[t0] == 1
    f1 = first_ref[t1] == 1

    @pl.when(f0)
    def _():
        change(t0, jnp.logical_and(j == 0, i == 0))

    @pl.when(jnp.logical_and(v1, jnp.logical_not(f1)))
    def _():
        rows_dot(0, 2 * half)

    @pl.when(f1)
    def _():
        rows_dot(0, half)
        change(t1, False)
        rows_dot(half, half)

    @pl.when(jnp.logical_and(v0, jnp.logical_not(v1)))
    def _():
        rows_dot(0, half)
        o_ref[half:, :] = jnp.zeros((half, tn), o_ref.dtype)

    @pl.when(jnp.logical_not(v0))
    def _():
        o_ref[...] = jnp.zeros_like(o_ref)


def _expert_matmul(plan, lhs, w, *, k, n_out, tn, col_offs, epilogue, out_dtype, name):
    te, first, nxt, misc = plan
    nj = n_out // tn
    nw = len(col_offs)
    grid_spec = pltpu.PrefetchScalarGridSpec(
        num_scalar_prefetch=4,
        grid=(nj, MOE_STEPS),
        in_specs=[
            pl.BlockSpec((2 * MOE_SORT, k), lambda j, i, *_: (i, 0)),
            pl.BlockSpec(memory_space=pl.ANY),
        ],
        out_specs=pl.BlockSpec((2 * MOE_SORT, tn), lambda j, i, *_: (i, j)),
        scratch_shapes=[pltpu.VMEM((nw, k, tn), F32), pltpu.VMEM((nw, k, tn), BF16),
                        pltpu.SemaphoreType.DMA((nw,))],
    )
    return pl.pallas_call(
        functools.partial(_expert_matmul_kernel, tn=tn, nj=nj, col_offs=col_offs, epilogue=epilogue),
        grid_spec=grid_spec,
        out_shape=jax.ShapeDtypeStruct((MOE_ROWS, n_out), out_dtype),
        compiler_params=_cparams(2),
        name=name,
    )(te, first, nxt, misc, lhs, w)


def _moe_gu(plan, xs, w_gu):
    return _expert_matmul(plan, xs, w_gu, k=D, n_out=EH, tn=1024, col_offs=(0, EH),
                          epilogue=lambda gt, up: _silu(gt) * up, out_dtype=BF16, name="moe_gu")


def _moe_down(plan, act, w_down):
    return _expert_matmul(plan, act, w_down, k=EH, n_out=D, tn=512, col_offs=(0,),
                          epilogue=lambda acc: acc, out_dtype=F32, name="moe_down")


def _row_copy(src_hbm, row, buf, r, sem):
    return pltpu.make_async_copy(src_hbm.at[pl.ds(row, 1), :], buf.at[pl.ds(r, 1), :], sem)


def _start_rows(src_hbm, idx_ref, off, dst, sem, n):
    def body(p, carry):
        for q in range(2):
            r = 2 * p + q
            _row_copy(src_hbm, idx_ref[0, 0, off + r], dst, r, sem).start(priority=q)
        return carry

    lax.fori_loop(0, n // 2, body, 0, unroll=4)


def _wait_rows(src_hbm, dst, sem, n):
    pltpu.make_async_copy(src_hbm.at[pl.ds(0, n), :], dst, sem).wait()


def _gather_kernel(nu_ref, idx_ref, idx_next_ref, src_hbm, o_ref, buf, sem, *, tm):
    t = pl.program_id(0)
    nu = nu_ref[0]
    slot = lax.rem(t, 2)

    @pl.when(t == 0)
    def _():
        _start_rows(src_hbm, idx_ref, 0, buf.at[0], sem.at[0], tm)

    @pl.when(t + 1 < nu)
    def _():
        _start_rows(src_hbm, idx_next_ref, 0, buf.at[1 - slot], sem.at[1 - slot], tm)

    @pl.when(t < nu)
    def _():
        _wait_rows(src_hbm, buf.at[slot], sem.at[slot], tm)
        o_ref[...] = buf[slot].astype(o_ref.dtype)

    @pl.when(t >= nu)
    def _():
        o_ref[...] = jnp.zeros_like(o_ref)


def _gather_rows(nu, idx3, src):
    tm = 2 * MOE_SORT
    nt = MOE_STEPS
    grid_spec = pltpu.PrefetchScalarGridSpec(
        num_scalar_prefetch=1,
        grid=(nt,),
        in_specs=[
            pl.BlockSpec((1, 1, tm), lambda t, nu: (t, 0, 0), memory_space=pltpu.SMEM),
            pl.BlockSpec((1, 1, tm), lambda t, nu: (jnp.minimum(t + 1, nt - 1), 0, 0), memory_space=pltpu.SMEM),
            pl.BlockSpec(memory_space=pl.ANY),
        ],
        out_specs=pl.BlockSpec((tm, D), lambda t, nu: (t, 0)),
        scratch_shapes=[pltpu.VMEM((2, tm, D), F32), pltpu.SemaphoreType.DMA((2,))],
    )
    return pl.pallas_call(
        functools.partial(_gather_kernel, tm=tm),
        grid_spec=grid_spec,
        out_shape=jax.ShapeDtypeStruct((MOE_ROWS, D), BF16),
        compiler_params=_cparams(1),
        name="moe_gather",
    )(nu, idx3, idx3, src)


def _combine_kernel(pos_ref, pos_next_ref, ys_hbm, x_ref, gv_ref, gate_ref, fw_ref, op_ref, os_ref, buf, sem,
                    *, tm, nt):
    t = pl.program_id(0)
    slot = lax.rem(t, 2)

    def start(idx_ref, s):
        _start_rows(ys_hbm, idx_ref, 0, buf.at[s, 0], sem.at[s, 0], tm)
        _start_rows(ys_hbm, idx_ref, tm, buf.at[s, 1], sem.at[s, 1], tm)

    @pl.when(t == 0)
    def _():
        start(pos_ref, 0)

    @pl.when(t + 1 < nt)
    def _():
        start(pos_next_ref, 1 - slot)

    _wait_rows(ys_hbm, buf.at[slot, 0], sem.at[slot, 0], tm)
    _wait_rows(ys_hbm, buf.at[slot, 1], sem.at[slot, 1], tm)

    def fn(mods, g, s):
        (gate,) = mods
        gv = gv_ref[...]
        f = gv[:, 0:1] * buf[slot, 0] + gv[:, 1:2] * buf[slot, 1]
        x = x_ref[...].reshape(g, tm // g, D) + gate * f.reshape(g, tm // g, D)
        x = x.reshape(tm, D)
        ms = jnp.mean(x * x, axis=-1, keepdims=True)
        (op_ref, os_ref)[s][...] = x * lax.rsqrt(ms + EPS) * fw_ref[...]

    _grouped_apply(t, tm, [gate_ref], fn)


def _combine(pos3, ys, x, gv, mod, gate_blk, final_w, tm=256):
    nt = M // tm
    n_pt = MP // tm
    return pl.pallas_call(
        functools.partial(_combine_kernel, tm=tm, nt=nt),
        grid=(nt,),
        in_specs=[
            pl.BlockSpec((1, 1, 2 * tm), lambda t: (t, 0, 0), memory_space=pltpu.SMEM),
            pl.BlockSpec((1, 1, 2 * tm), lambda t: (jnp.minimum(t + 1, nt - 1), 0, 0), memory_space=pltpu.SMEM),
            pl.BlockSpec(memory_space=pl.ANY),
            pl.BlockSpec((tm, D), lambda t: (t, 0)),
            pl.BlockSpec((tm, 128), lambda t: (t, 0)),
            pl.BlockSpec((MOD_ROWS, D), lambda t: (0, gate_blk)),
            pl.BlockSpec((1, D), lambda t: (0, 0)),
        ],
        out_specs=[
            pl.BlockSpec((tm, D), lambda t: (jnp.minimum(t, n_pt - 1), 0)),
            pl.BlockSpec((tm, D), lambda t: (jnp.maximum(t - n_pt, 0), 0)),
        ],
        out_shape=[jax.ShapeDtypeStruct((MP, D), F32), jax.ShapeDtypeStruct((MS, D), F32)],
        scratch_shapes=[pltpu.VMEM((2, 2, tm, D), F32), pltpu.SemaphoreType.DMA((2, 2))],
        compiler_params=_cparams(1),
        name="moe_combine",
    )(pos3, pos3, ys, x, gv, mod, final_w)


def _ssd_kernel(z_ref, xbc_ref, dt_ref, cs_ref, h0_ref, cw_ref, cb_ref, dtb_ref, alog_ref, dsk_ref, nw_ref,
                e2_ref, et2_ref, y_ref, csn_ref, hn_ref, cbuf, hst, *, C, nc):
    c = pl.program_id(1)
    hp = 128 // C
    neg = jnp.float32(-jnp.inf)

    @pl.when(c == 0)
    def _():
        cbuf[0:8, :] = jnp.zeros((8, SSD_XBC), F32)
        cbuf[5:8, :] = cs_ref[...]
        hst[...] = h0_ref[...]

    cbuf[8:8 + C, :] = xbc_ref[...]

    @pl.when(c == nc - 1)
    def _():
        csn_ref[...] = cbuf[C + 5:C + 8, :]

    dt = _softplus(dt_ref[...] + dtb_ref[...])
    a = -jnp.exp(alog_ref[...]) * LOG2E
    row = lax.broadcasted_iota(I32, (C, C), 0)
    col = lax.broadcasted_iota(I32, (C, C), 1)
    tril = jnp.where(row >= col, 1.0, 0.0).astype(BF16)
    b = _dot_exact_rhs(tril, dt * a)
    eb = jnp.exp2(b)
    b_last = b[C - 1:C, :]
    dl = dt * jnp.exp2(b_last - b)
    stack = jnp.concatenate([dt, eb, dl], axis=0)
    s_hi = stack.astype(BF16)
    s_lo = (stack - s_hi.astype(F32)).astype(BF16)
    lhs2 = jnp.concatenate([s_hi, s_lo], axis=1)
    bst = jnp.concatenate([b] + [pltpu.roll(b, 128 - k, axis=1) for k in range(1, hp)], axis=0)
    bt = bst.T
    rdec = jnp.broadcast_to(jnp.exp2(bt[:, C - 1:C]), (128, 128))
    r_hi = rdec.astype(BF16)
    r_lo = (rdec - r_hi.astype(F32)).astype(BF16)
    r2 = jnp.concatenate([r_hi, r_lo], axis=0)

    lane = lax.broadcasted_iota(I32, (C, 128), 1)
    trow = lax.broadcasted_iota(I32, (C, 128), 0)
    causal = trow >= (lane & (C - 1))
    lane_seg = lane >> (C.bit_length() - 1)
    xlane_seg = lax.broadcasted_iota(I32, (C, hp * SSD_P), 1) >> (SSD_P.bit_length() - 1)

    def conv(cols):
        v = cbuf[0:C + 8, cols]
        acc = cb_ref[:, cols] + v[8:8 + C] * cw_ref[SSD_K - 1:SSD_K, cols]
        for k in range(SSD_K - 1):
            acc = acc + pltpu.roll(v, C + 3 - k, axis=0)[0:C] * cw_ref[k:k + 1, cols]
        return _silu(acc)

    for g in range(SSD_G):
        xc = slice(g * SSD_GW, (g + 1) * SSD_GW)
        xg = conv(xc)
        bg = conv(slice(SSD_DI + g * SSD_N, SSD_DI + (g + 1) * SSD_N)).astype(BF16)
        cg = conv(slice(SSD_DI + SSD_G * SSD_N + g * SSD_N, SSD_DI + SSD_G * SSD_N + (g + 1) * SSD_N)).astype(BF16)
        ex = jnp.dot(lhs2, e2_ref[:, xc], preferred_element_type=F32)
        dt_e, eb_e, dl_e = ex[0:C], ex[C:2 * C], ex[2 * C:3 * C]
        xdt = xg * dt_e
        cb2 = lax.dot_general(cg, jnp.concatenate([bg] * hp, axis=0), NT_DIMS, preferred_element_type=F32)
        hg = hst[g * SSD_GW:(g + 1) * SSD_GW, :]
        y_inter = lax.dot_general(cg, hg.astype(BF16), NT_DIMS, preferred_element_type=F32)
        ys = []
        for p in range(8 // hp):
            h0 = g * 8 + p * hp
            bcol = jnp.broadcast_to(b[:, h0:h0 + 1], (C, 128))
            for k in range(1, hp):
                bcol = jnp.where(lane_seg >= k, b[:, h0 + k:h0 + k + 1], bcol)
            lm = jnp.exp2(jnp.where(causal, bcol - bt[h0:h0 + 1, :], neg))
            m2 = (cb2 * lm).astype(BF16)
            xp = xdt[:, p * hp * SSD_P:(p + 1) * hp * SSD_P]
            rhs = jnp.concatenate([jnp.where(xlane_seg == k, xp, 0.0) for k in range(hp)], axis=0).astype(BF16)
            ys.append(jnp.dot(m2, rhs, preferred_element_type=F32))
        y = jnp.concatenate(ys, axis=1) + y_inter * eb_e + xg * dsk_ref[:, xc]
        y = y * _silu(z_ref[:, xc])
        ms = jnp.mean(y * y, axis=-1, keepdims=True)
        y_ref[:, xc] = (y * lax.rsqrt(ms + EPS) * nw_ref[:, xc]).astype(y_ref.dtype)
        upd = lax.dot_general((xg * dl_e).astype(BF16), bg, TN_DIMS, preferred_element_type=F32)
        dec = jnp.dot(et2_ref[g * SSD_GW:(g + 1) * SSD_GW, :], r2, preferred_element_type=F32)
        hst[g * SSD_GW:(g + 1) * SSD_GW, :] = hg * dec + upd

    cbuf[5:8, :] = cbuf[C + 5:C + 8, :]

    @pl.when(c == nc - 1)
    def _():
        hn_ref[...] = hst[...]


def _ssd_scan(z, xbc, dtr, conv_state, h0, consts, *, nb, seq, C, row0):
    nc = seq // C
    rb0 = row0 // C
    cw, cb, dtb, alog, dsk, nw, e2, et2 = consts
    rows = lambda b, c: (rb0 + b * nc + c, 0)
    fixed = lambda b, c: (0, 0)
    per_b = lambda b, c: (b, 0, 0)
    return pl.pallas_call(
        functools.partial(_ssd_kernel, C=C, nc=nc),
        grid=(nb, nc),
        in_specs=[
            pl.BlockSpec((C, SSD_DI), rows),
            pl.BlockSpec((C, SSD_XBC), rows),
            pl.BlockSpec((C, 128), rows),
            pl.BlockSpec((None, SSD_K - 1, SSD_XBC), per_b),
            pl.BlockSpec((None, SSD_DI, SSD_N), per_b),
            pl.BlockSpec((SSD_K, SSD_XBC), fixed),
            pl.BlockSpec((1, SSD_XBC), fixed),
            pl.BlockSpec((1, 128), fixed),
            pl.BlockSpec((1, 128), fixed),
            pl.BlockSpec((1, SSD_DI), fixed),
            pl.BlockSpec((1, SSD_DI), fixed),
            pl.BlockSpec((256, SSD_DI), fixed),
            pl.BlockSpec((SSD_DI, 256), fixed),
        ],
        out_specs=[
            pl.BlockSpec((C, SSD_DI), lambda b, c: (b * nc + c, 0)),
            pl.BlockSpec((None, SSD_K - 1, SSD_XBC), per_b),
            pl.BlockSpec((None, SSD_DI, SSD_N), per_b),
        ],
        out_shape=[
            jax.ShapeDtypeStruct((nb * seq, SSD_DI), BF16),
            jax.ShapeDtypeStruct((nb, SSD_K - 1, SSD_XBC), F32),
            jax.ShapeDtypeStruct((nb, SSD_DI, SSD_N), F32),
        ],
        scratch_shapes=[pltpu.VMEM((8 + C, SSD_XBC), F32), pltpu.VMEM((SSD_DI, SSD_N), F32)],
        compiler_params=_cparams(2),
        name=f"ssd_scan_c{C}",
    )(z, xbc, dtr, conv_state, h0, cw, cb, dtb, alog, dsk, nw, e2, et2)


def _gla_kernel(q_ref, f_ref, v_ref, g_ref, s0_ref, lb_ref, nw_ref, o_ref, sn_ref, st, b_s, c_s, *, T, nc):
    c = pl.program_id(1)
    neg = jnp.float32(-jnp.inf)

    @pl.when(c == 0)
    def _():
        for h in range(HG_H):
            st[h * 128:(h + 1) * 128, :] = s0_ref[h * 128:(h + 1) * 128, :].T

    row = lax.broadcasted_iota(I32, (T, T), 0)
    col = lax.broadcasted_iota(I32, (T, T), 1)
    sh = HG_BLK.bit_length() - 1
    same_blk = (row >> sh) == (col >> sh)
    bd = jnp.where(jnp.logical_and(same_blk, row >= col), 1.0, 0.0).astype(BF16)

    for h in range(HG_H):
        hl = slice(h * 128, (h + 1) * 128)
        fz = f_ref[:, hl]
        lb = lb_ref[:, hl]
        ls = jnp.minimum(fz, 0.0) - _log1p_unit(jnp.exp(-jnp.abs(fz)))
        a1 = jnp.log(lb)
        a2 = jnp.log1p(-lb) + ls
        lf = jnp.maximum(a1, a2) + _log1p_unit(jnp.exp(-jnp.abs(a1 - a2)))
        b2 = _dot_exact_rhs(bd, lf * LOG2E)
        b_s[:, hl] = b2
        c_s[:, hl] = b2 - (a2 - fz) * LOG2E

    trow = lax.broadcasted_iota(I32, (8, 128), 0)
    ones = jnp.ones((128, 128), BF16)
    hb = HG_BLK // 2

    def blk(j, carry):
        rows = pl.ds(pl.multiple_of(j * HG_BLK, HG_BLK), HG_BLK)
        for h in range(HG_H):
            hl = slice(h * 128, (h + 1) * 128)
            qb = q_ref[rows, hl]
            bb = b_s[rows, hl]
            cb = c_s[rows, hl]
            vb = v_ref[rows, hl]
            sth = st[hl, :]
            o = lax.dot_general((qb * jnp.exp2(bb)).astype(BF16), sth.astype(BF16), NT_DIMS,
                                preferred_element_type=F32)
            q_h, b_h = (qb[0:hb], qb[hb:]), (bb[0:hb], bb[hb:])
            pieces, owner = [], []
            for s in range(HG_BLK):
                cs = cb[s:s + 1, :]
                for half in range(2):
                    if s >= hb * (half + 1):
                        continue
                    e = b_h[half] - cs
                    if s >= hb * half:
                        e = jnp.where(trow >= s - hb * half, e, neg)
                    pieces.append(q_h[half] * jnp.exp2(e))
                    owner.append((s, half))
            att = jnp.dot(jnp.concatenate(pieces, axis=0).astype(BF16), ones, preferred_element_type=F32)
            o_h = [o[0:hb], o[hb:]]
            for n, (s, half) in enumerate(owner):
                o_h[half] = o_h[half] + att[n * hb:(n + 1) * hb] * vb[s:s + 1, :]
            o = jnp.concatenate(o_h, axis=0)
            b_end = bb[HG_BLK - 1:HG_BLK, :]
            khat = jnp.exp2(b_end - cb).astype(BF16)
            upd = lax.dot_general(vb.astype(BF16), khat, TN_DIMS, preferred_element_type=F32)
            st[hl, :] = sth * jnp.exp2(b_end) + upd
            ms = jnp.mean(o * o, axis=-1, keepdims=True)
            o_ref[rows, hl] = (o * lax.rsqrt(ms + EPS) * nw_ref[...] * _silu(g_ref[rows, hl])).astype(o_ref.dtype)
        return carry

    lax.fori_loop(0, T // HG_BLK, blk, 0)

    @pl.when(c == nc - 1)
    def _():
        for h in range(HG_H):
            sn_ref[h * 128:(h + 1) * 128, :] = st[h * 128:(h + 1) * 128, :].T


def _gla_scan(qfig, s0, lb, nw, *, nb, seq, T, row0):
    nc = seq // T
    rb0 = row0 // T
    per_b = lambda b, c: (b, 0, 0)

    def cols(k):
        return pl.BlockSpec((T, D), lambda b, c: (rb0 + b * nc + c, k))

    return pl.pallas_call(
        functools.partial(_gla_kernel, T=T, nc=nc),
        grid=(nb, nc),
        in_specs=[
            cols(0), cols(1), cols(2), cols(3),
            pl.BlockSpec((None, D, HG_DV), per_b),
            pl.BlockSpec((1, D), lambda b, c: (0, 0)),
            pl.BlockSpec((1, HG_DV), lambda b, c: (0, 0)),
        ],
        out_specs=[
            pl.BlockSpec((T, D), lambda b, c: (b * nc + c, 0)),
            pl.BlockSpec((None, D, HG_DV), per_b),
        ],
        out_shape=[
            jax.ShapeDtypeStruct((nb * seq, D), BF16),
            jax.ShapeDtypeStruct((nb, D, HG_DV), F32),
        ],
        scratch_shapes=[pltpu.VMEM((D, HG_DK), F32), pltpu.VMEM((T, D), F32), pltpu.VMEM((T, D), F32)],
        compiler_params=_cparams(2),
        name=f"gla_scan_t{T}",
    )(qfig, qfig, qfig, qfig, s0, lb, nw)


def _route_plan(top_i):
    e_flat = top_i.reshape(-1)
    onehot = (e_flat[:, None] == jnp.arange(NE, dtype=I32)[None, :]).astype(I32)
    csum = jnp.cumsum(onehot, axis=0)
    rank = jnp.sum(onehot * (csum - 1), axis=1)
    cnt = csum[-1]
    ntile = (cnt + MOE_SORT - 1) // MOE_SORT
    cum_t = jnp.cumsum(ntile)
    tile0 = cum_t - ntile
    pos = jnp.sum(onehot * tile0[None, :], axis=1) * MOE_SORT + rank
    n_used = cum_t[-1]
    t_ids = jnp.arange(MOE_TILES, dtype=I32)
    te = jnp.sum((t_ids[:, None] >= cum_t[None, :]).astype(I32), axis=1)
    te_last = jnp.sum((n_used - 1 >= cum_t).astype(I32))
    te = jnp.where(t_ids < n_used, te, te_last).astype(I32)
    prev = jnp.concatenate([jnp.full((1,), -1, I32), te[:-1]])
    first = jnp.logical_and(t_ids < n_used, te != prev).astype(I32)
    e_ids = jnp.arange(NE, dtype=I32)
    later = jnp.logical_and(e_ids[None, :] > e_ids[:, None], ntile[None, :] > 0)
    nxt = jnp.min(jnp.where(later, e_ids[None, :], NE), axis=1)
    nxt = jnp.where(nxt == NE, -1, nxt).astype(I32)
    misc = jnp.stack([n_used, te[0]]).astype(I32)
    tok = jnp.arange(2 * M, dtype=I32) // 2
    src = jnp.zeros((MOE_ROWS,), I32).at[pos].set(tok)
    n_steps = ((n_used + 1) // 2).reshape(1).astype(I32)
    return (te, first, nxt, misc), n_steps, src, pos.reshape(M, 2)


def kernel(x_prompt, x_sample, c_prompt, c_sample, state_ssd_conv, state_ssd, state_hgrn, ada_w, ada_b, norm_w,
           ssd_w_in, ssd_conv_w, ssd_conv_b, ssd_dt_bias, ssd_a_log, ssd_d, ssd_norm_w, ssd_w_out, hgrn_w_in,
           hgrn_lb_logits, hgrn_norm_w, hgrn_w_out, ffn_w_gu, ffn_w_down, moe_router, moe_w_gu, moe_w_down,
           final_norm_w):
    x = (x_prompt.reshape(MP, D), x_sample.reshape(MS, D))
    c_all = jnp.concatenate([c_prompt, jnp.zeros((MOD_S0 - NP, D), F32), c_sample], axis=0)
    norm_w4 = norm_w.reshape(4, 1, D)

    mods = [
        _mm_plain(c_all, ada_w, layer=l, col0=0, n_out=6 * D, tn=1024, tm=MOD_ROWS,
                  bias=ada_b[l].reshape(1, 6 * D), act_lhs=True, name="adaln")
        for l in range(2)
    ]

    hn = _prenorm(x, norm_w4, 0, mods[0], 0, 1, BF16)
    w_in_t = jnp.swapaxes(ssd_w_in, 1, 2)
    z = _mm_plain(hn, w_in_t, layer=0, col0=0, n_out=SSD_DI, tn=1024, tm=1024, w_t=True, name="ssd_in_z")
    xbc = _mm_plain(hn, w_in_t, layer=0, col0=SSD_DI, n_out=SSD_XBC, tn=1024, tm=1024, w_t=True,
                    name="ssd_in_xbc")
    dtr = _mm_plain(hn, w_in_t, layer=0, col0=SSD_DI + SSD_XBC, n_out=128, tn=128, tm=1024, w_t=True,
                    name="ssd_in_dt")

    pad64 = lambda v: jnp.pad(v.reshape(1, SSD_HEADS), ((0, 0), (0, 128 - SSD_HEADS)))
    head_of = jnp.arange(SSD_DI, dtype=I32) // SSD_P
    e1 = (jnp.arange(128, dtype=I32)[:, None] == head_of[None, :]).astype(BF16)
    e2 = jnp.concatenate([e1, e1], axis=0)
    et2 = jnp.concatenate([e1.T, e1.T], axis=1)
    consts = (ssd_conv_w[0], ssd_conv_b[0].reshape(1, SSD_XBC), pad64(ssd_dt_bias[0]), pad64(ssd_a_log[0]),
              jnp.repeat(ssd_d[0], SSD_P).reshape(1, SSD_DI), ssd_norm_w[0].reshape(1, SSD_DI), e2, et2)
    yp, p_conv, p_ssd = _ssd_scan(z, xbc, dtr, jnp.zeros((NP, SSD_K - 1, SSD_XBC), F32),
                                  jnp.zeros((NP, SSD_DI, SSD_N), F32), consts, nb=NP, seq=LP, C=64, row0=0)
    ysm, s_conv, s_ssd = _ssd_scan(z, xbc, dtr, state_ssd_conv[0], state_ssd[0].reshape(NS, SSD_DI, SSD_N), consts,
                                   nb=NS, seq=LS, C=32, row0=MP)
    x = _mm_resid((yp, ysm), ssd_w_out, x, mods[0], 2, k=SSD_DI, tn=512, tm=512, name="ssd_out")

    hn = _prenorm(x, norm_w4, 1, mods[0], 3, 4, BF16)
    act = _mm_swiglu(hn, ffn_w_gu, FFN, tn=512, tm=1024, name="ffn_gu")
    x = _mm_resid(act, ffn_w_down, x, mods[0], 5, k=FFN, tn=512, tm=512, name="ffn_down")

    hn = _prenorm(x, norm_w4, 2, mods[1], 0, 1, BF16)
    qfig = _mm_plain(hn, hgrn_w_in, layer=0, col0=0, n_out=4 * D, tn=1024, tm=1024, name="hgrn_in")
    p = jax.nn.softmax(hgrn_lb_logits.astype(F32), axis=0)
    lb = (jnp.cumsum(p, axis=0) - p[0])[1].reshape(1, D)
    nw_h = hgrn_norm_w[0].reshape(1, HG_DV)
    op, p_hgrn = _gla_scan(qfig, jnp.zeros((NP, D, HG_DV), F32), lb, nw_h, nb=NP, seq=LP, T=64, row0=0)
    osm, s_hgrn = _gla_scan(qfig, state_hgrn[0].reshape(NS, D, HG_DV), lb, nw_h, nb=NS, seq=LS, T=32, row0=MP)
    x = _mm_resid((op, osm), hgrn_w_out, x, mods[1], 2, k=D, tn=1024, tm=512, name="hgrn_out")

    hn32, gv, gi = _prenorm(x, norm_w4, 3, mods[1], 3, 4, F32,
                            router_pad=jnp.pad(moe_router[0], ((0, 0), (0, 128 - NE))))
    plan, n_steps, src, pos = _route_plan(gi[:, :2])
    xs = _gather_rows(n_steps, src.reshape(MOE_STEPS, 1, 2 * MOE_SORT), hn32)
    act = _moe_gu(plan, xs, moe_w_gu[0])
    ys = _moe_down(plan, act, moe_w_down[0])
    tmc = 256
    pos3 = jnp.concatenate([pos[:, 0].reshape(M // tmc, 1, tmc), pos[:, 1].reshape(M // tmc, 1, tmc)], axis=2)
    yo_p, yo_s = _combine(pos3, ys, x, gv, mods[1], 5, final_norm_w.reshape(1, D), tm=tmc)

    return (
        yo_p.reshape(NP, LP, D),
        yo_s.reshape(NS, LS, D),
        p_conv[None],
        p_ssd.reshape(1, NP, SSD_HEADS, SSD_P, SSD_N),
        p_hgrn.reshape(1, NP, HG_H, HG_DK, HG_DV),
        s_conv[None],
        s_ssd.reshape(1, NS, SSD_HEADS, SSD_P, SSD_N),
        s_hgrn.reshape(1, NS, HG_H, HG_DK, HG_DV),
    )
```

```python
import functools

import jax
import jax.numpy as jnp
from jax import lax
from jax.experimental import pallas as pl
from jax.experimental.pallas import tpu as pltpu

F32 = jnp.float32
BF16 = jnp.bfloat16
I32 = jnp.int32

EPS = 1e-6
LOG2E = 1.4426950408889634
D = 2048
NP, LP = 4, 2048
NS, LS = 32, 32
MP, MS = NP * LP, NS * LS
M = MP + MS
MOD_S0 = 8
MOD_ROWS = MOD_S0 + NS

SSD_DI = 4096
SSD_HEADS = 64
SSD_P = 64
SSD_G = 8
SSD_GW = SSD_DI // SSD_G
SSD_N = 128
SSD_XBC = SSD_DI + 2 * SSD_G * SSD_N
SSD_K = 4

HG_H = 16
HG_DK = 128
HG_DV = 128
HG_BLK = 16

FFN = 5632
NE = 8
EH = 7168
MOE_SORT = 256
MOE_TILES = (2 * M) // MOE_SORT + NE
MOE_STEPS = MOE_TILES // 2
MOE_ROWS = MOE_TILES * MOE_SORT
CAST_ROWS = 64

VMEM_LIMIT = 60 * 1024 * 1024

NT_DIMS = (((1,), (1,)), ((), ()))
TN_DIMS = (((0,), (0,)), ((), ()))


def _cparams(n_axes):
    return pltpu.CompilerParams(dimension_semantics=("arbitrary",) * n_axes,
                                vmem_limit_bytes=VMEM_LIMIT)


def _silu(x):
    h = 0.5 * x
    return h + h * jnp.tanh(h)


def _softplus(x):
    return jnp.maximum(x, 0.0) + jnp.log1p(jnp.exp(-jnp.abs(x)))


def _log1p_unit(u):
    return jnp.log(1.0 + u)


def _split3(x):
    hi = x.astype(BF16)
    r1 = x - hi.astype(F32)
    mid = r1.astype(BF16)
    lo = (r1 - mid.astype(F32)).astype(BF16)
    return hi, mid, lo


def _dot_exact_rhs(mat_bf16, x):
    hi, mid, lo = _split3(x)
    acc = jnp.dot(mat_bf16, hi, preferred_element_type=F32)
    acc = acc + jnp.dot(mat_bf16, mid, preferred_element_type=F32)
    return acc + jnp.dot(mat_bf16, lo, preferred_element_type=F32)


def _grouped_apply(i, tm, mod_refs, fn):
    n_pt = MP // tm
    tpb = LP // tm

    @pl.when(i < n_pt)
    def _():
        r = i // tpb
        fn([m[pl.ds(r, 1), :][None] for m in mod_refs], 1, 0)

    @pl.when(i >= n_pt)
    def _():
        g = tm // LS
        start = pl.multiple_of(MOD_S0 + (i - n_pt) * g, 8)
        fn([m[pl.ds(start, g), :][:, None, :] for m in mod_refs], g, 1)


def _row_operand(op, tm, width, ij):
    if not isinstance(op, tuple):
        return [pl.BlockSpec((tm, width), lambda *g: ij(*g))], [op]
    n_pt = MP // tm
    return [
        pl.BlockSpec((tm, width), lambda *g: (jnp.minimum(ij(*g)[0], n_pt - 1), ij(*g)[1])),
        pl.BlockSpec((tm, width), lambda *g: (jnp.maximum(ij(*g)[0] - n_pt, 0), ij(*g)[1])),
    ], list(op)


def _top2_gates(logits):
    lane = lax.broadcasted_iota(I32, logits.shape, 1)
    neg = jnp.float32(-jnp.inf)
    l1 = jnp.where(lane < NE, logits, neg)
    m1 = jnp.max(l1, axis=-1, keepdims=True)
    i1 = jnp.min(jnp.where(l1 == m1, lane, 128), axis=-1, keepdims=True)
    l2 = jnp.where(lane == i1, neg, l1)
    m2 = jnp.max(l2, axis=-1, keepdims=True)
    i2 = jnp.min(jnp.where(l2 == m2, lane, 128), axis=-1, keepdims=True)
    e = jnp.exp(m2 - m1)
    g1 = 1.0 / (1.0 + e)
    g2 = e * g1
    gv = jnp.where(lane == 0, g1, jnp.where(lane == 1, g2, 0.0))
    gi = jnp.where(lane == 0, i1, jnp.where(lane == 1, i2, 0))
    return gv, gi


def _prenorm_kernel(*refs, tm, nx, route):
    x_refs = refs[:nx]
    if route:
        nw_ref, sc_ref, sh_ref, r_ref, o_ref, gv_ref, gi_ref = refs[nx:]
    else:
        nw_ref, sc_ref, sh_ref, o_ref = refs[nx:]
    i = pl.program_id(0)

    def fn(mods, g, s):
        sc, sh = mods
        x = x_refs[s % nx][...]
        ms = jnp.mean(x * x, axis=-1, keepdims=True)
        y = x * lax.rsqrt(ms + EPS) * nw_ref[...]
        y = (y.reshape(g, tm // g, D) * (1.0 + sc) + sh).reshape(tm, D)
        o_ref[...] = y.astype(o_ref.dtype)
        if route:
            r = r_ref[...]
            y_hi, r_hi = y.astype(BF16), r.astype(BF16)
            y_lo = (y - y_hi.astype(F32)).astype(BF16)
            r_lo = (r - r_hi.astype(F32)).astype(BF16)
            logits = (jnp.dot(y_hi, r_hi, preferred_element_type=F32)
                      + jnp.dot(y_lo, r_hi, preferred_element_type=F32)
                      + jnp.dot(y_hi, r_lo, preferred_element_type=F32))
            gv_ref[...], gi_ref[...] = _top2_gates(logits)

    _grouped_apply(i, tm, [sc_ref, sh_ref], fn)


def _prenorm(x, norm_w4, k, mod, sh_blk, sc_blk, out_dtype, tm=512, router_pad=None):
    x_specs, x_args = _row_operand(x, tm, D, lambda i: (i, 0))
    route = router_pad is not None
    row_out = lambda w: pl.BlockSpec((tm, w), lambda i: (i, 0))
    out_specs = [row_out(D)]
    out_shape = [jax.ShapeDtypeStruct((M, D), out_dtype)]
    extra_specs, extra_args = [], []
    if route:
        extra_specs = [pl.BlockSpec((D, 128), lambda i: (0, 0))]
        extra_args = [router_pad]
        out_specs += [row_out(128), row_out(128)]
        out_shape += [jax.ShapeDtypeStruct((M, 128), F32), jax.ShapeDtypeStruct((M, 128), I32)]
    res = pl.pallas_call(
        functools.partial(_prenorm_kernel, tm=tm, nx=len(x_args), route=route),
        grid=(M // tm,),
        in_specs=x_specs + [
            pl.BlockSpec((None, 1, D), lambda i: (k, 0, 0)),
            pl.BlockSpec((MOD_ROWS, D), lambda i: (0, sc_blk)),
            pl.BlockSpec((MOD_ROWS, D), lambda i: (0, sh_blk)),
        ] + extra_specs,
        out_specs=out_specs,
        out_shape=out_shape,
        compiler_params=_cparams(1),
        name="prenorm_route" if route else "prenorm",
    )(*x_args, norm_w4, mod, mod, *extra_args)
    return res if route else res[0]


def _mm_plain_kernel(*refs, act_lhs, has_bias, valid_cols, w_t):
    if has_bias:
        lhs_ref, w_ref, b_ref, o_ref, wb = refs
    else:
        lhs_ref, w_ref, o_ref, wb = refs

    @pl.when(pl.program_id(1) == 0)
    def _():
        w = w_ref[...]
        if valid_cols is not None:
            w = jnp.where(lax.broadcasted_iota(I32, w.shape, 0 if w_t else 1) < valid_cols, w, 0.0)
        wb[...] = w.astype(BF16)

    lhs = lhs_ref[...]
    if act_lhs:
        lhs = _silu(lhs).astype(BF16)
    if w_t:
        acc = lax.dot_general(lhs, wb[...], NT_DIMS, preferred_element_type=F32)
    else:
        acc = jnp.dot(lhs, wb[...], preferred_element_type=F32)
    if has_bias:
        acc = acc + b_ref[...]
    o_ref[...] = acc.astype(o_ref.dtype)


def _mm_plain(lhs, w3, *, layer, col0, n_out, tn, tm, out_dtype=F32, bias=None, act_lhs=False, w_t=False,
              name="mm"):
    rows, k = lhs.shape
    off = col0 // tn
    w_cols = w3.shape[1 if w_t else 2] - col0
    valid_cols = w_cols if w_cols < n_out else None
    if w_t:
        w_spec = pl.BlockSpec((None, tn, k), lambda j, i: (layer, j + off, 0))
    else:
        w_spec = pl.BlockSpec((None, k, tn), lambda j, i: (layer, 0, j + off))
    in_specs = [pl.BlockSpec((tm, k), lambda j, i: (i, 0)), w_spec]
    args = [lhs, w3]
    if bias is not None:
        in_specs.append(pl.BlockSpec((1, tn), lambda j, i: (0, j)))
        args.append(bias)
    return pl.pallas_call(
        functools.partial(_mm_plain_kernel, act_lhs=act_lhs, has_bias=bias is not None, valid_cols=valid_cols,
                          w_t=w_t),
        grid=(n_out // tn, rows // tm),
        in_specs=in_specs,
        out_specs=pl.BlockSpec((tm, tn), lambda j, i: (i, j)),
        out_shape=jax.ShapeDtypeStruct((rows, n_out), out_dtype),
        scratch_shapes=[pltpu.VMEM((tn, k) if w_t else (k, tn), BF16)],
        compiler_params=_cparams(2),
        name=name,
    )(*args)


def _mm_resid_kernel(*refs, tm, tn, nl, nr):
    lhs_refs = refs[:nl]
    w_ref = refs[nl]
    res_refs = refs[nl + 1:nl + 1 + nr]
    gate_ref, o_ref, wb = refs[nl + 1 + nr:]
    i = pl.program_id(1)

    @pl.when(i == 0)
    def _():
        wb[...] = w_ref[...].astype(BF16)

    def fn(mods, g, s):
        (gate,) = mods
        acc = jnp.dot(lhs_refs[s % nl][...], wb[...], preferred_element_type=F32)
        out = res_refs[s % nr][...].reshape(g, tm // g, tn) + gate * acc.reshape(g, tm // g, tn)
        o_ref[...] = out.reshape(tm, tn)

    _grouped_apply(i, tm, [gate_ref], fn)


def _mm_resid(lhs, w3, res, mod, gate_blk, *, k, tn, tm, name):
    goff = gate_blk * (D // tn)
    l_specs, l_args = _row_operand(lhs, tm, k, lambda j, i: (i, 0))
    r_specs, r_args = _row_operand(res, tm, tn, lambda j, i: (i, j))
    return pl.pallas_call(
        functools.partial(_mm_resid_kernel, tm=tm, tn=tn, nl=len(l_args), nr=len(r_args)),
        grid=(D // tn, M // tm),
        in_specs=l_specs + [pl.BlockSpec((None, k, tn), lambda j, i: (0, 0, j),
                                         pipeline_mode=pl.Buffered(1))] + r_specs + [
            pl.BlockSpec((MOD_ROWS, tn), lambda j, i: (0, goff + j)),
        ],
        out_specs=pl.BlockSpec((tm, tn), lambda j, i: (i, j)),
        out_shape=jax.ShapeDtypeStruct((M, D), F32),
        scratch_shapes=[pltpu.VMEM((k, tn), BF16)],
        compiler_params=_cparams(2),
        name=name,
    )(*l_args, w3, *r_args, mod)


def _mm_swiglu_kernel(lhs_ref, wg_ref, wu_ref, o_ref, wgb, wub):
    @pl.when(pl.program_id(1) == 0)
    def _():
        wgb[...] = wg_ref[...].astype(BF16)
        wub[...] = wu_ref[...].astype(BF16)

    lhs = lhs_ref[...]
    gt = jnp.dot(lhs, wgb[...], preferred_element_type=F32)
    up = jnp.dot(lhs, wub[...], preferred_element_type=F32)
    o_ref[...] = (_silu(gt) * up).astype(o_ref.dtype)


def _mm_swiglu(lhs, w3, hidden, *, tn, tm, name):
    rows, k = lhs.shape
    nb = hidden // tn
    return pl.pallas_call(
        _mm_swiglu_kernel,
        grid=(nb, rows // tm),
        in_specs=[
            pl.BlockSpec((tm, k), lambda j, i: (i, 0)),
            pl.BlockSpec((None, k, tn), lambda j, i: (0, 0, j)),
            pl.BlockSpec((None, k, tn), lambda j, i: (0, 0, j + nb)),
        ],
        out_specs=pl.BlockSpec((tm, tn), lambda j, i: (i, j)),
        out_shape=jax.ShapeDtypeStruct((rows, hidden), BF16),
        scratch_shapes=[pltpu.VMEM((k, tn), BF16), pltpu.VMEM((k, tn), BF16)],
        compiler_params=_cparams(2),
        name=name,
    )(lhs, w3, w3)


def _expert_matmul_kernel(te_ref, first_ref, nxt_ref, misc_ref, lhs_ref, w_hbm, o_ref, stage, panel, sem,
                          *, tn, nj, col_offs, epilogue):
    j = pl.program_id(0)
    i = pl.program_id(1)
    nu = misc_ref[0]
    first_e = misc_ref[1]
    half = MOE_SORT
    nw = len(col_offs)

    def weight_copies(e, jj):
        c0 = pl.multiple_of(jj * tn, 128)
        return [pltpu.make_async_copy(w_hbm.at[e, :, pl.ds(off + c0, tn)], stage.at[n], sem.at[n])
                for n, off in enumerate(col_offs)]

    def change(t, very_first):
        e_new = te_ref[t]

        @pl.when(very_first)
        def _():
            for cp in weight_copies(e_new, j):
                cp.start()

        for cp in weight_copies(e_new, j):
            cp.wait()

        def cast_rows(c, carry):
            r = pl.multiple_of(c * CAST_ROWS, CAST_ROWS)
            for n in range(nw):
                panel[n, pl.ds(r, CAST_ROWS), :] = stage[n, pl.ds(r, CAST_ROWS), :].astype(BF16)
            return carry

        lax.fori_loop(0, stage.shape[1] // CAST_ROWS, cast_rows, 0)
        ne = nxt_ref[e_new]
        same_col = ne >= 0

        @pl.when(jnp.logical_or(same_col, j + 1 < nj))
        def _():
            for cp in weight_copies(jnp.where(same_col, ne, first_e), jnp.where(same_col, j, j + 1)):
                cp.start()

    def rows_dot(r0, rows):
        lhs = lhs_ref[r0:r0 + rows, :]
        accs = [jnp.dot(lhs, panel[n], preferred_element_type=F32) for n in range(nw)]
        o_ref[r0:r0 + rows, :] = epilogue(*accs).astype(o_ref.dtype)

    t0 = 2 * i
    t1 = t0 + 1
    v0 = t0 < nu
    v1 = t1 < nu
    f0 = first_ref[t0] == 1
    f1 = first_ref[t1] == 1

    @pl.when(f0)
    def _():
        change(t0, jnp.logical_and(j == 0, i == 0))

    @pl.when(jnp.logical_and(v1, jnp.logical_not(f1)))
    def _():
        rows_dot(0, 2 * half)

    @pl.when(f1)
    def _():
        rows_dot(0, half)
        change(t1, False)
        rows_dot(half, half)

    @pl.when(jnp.logical_and(v0, jnp.logical_not(v1)))
    def _():
        rows_dot(0, half)
        o_ref[half:, :] = jnp.zeros((half, tn), o_ref.dtype)

    @pl.when(jnp.logical_not(v0))
    def _():
        o_ref[...] = jnp.zeros_like(o_ref)


def _expert_matmul(plan, lhs, w, *, k, n_out, tn, col_offs, epilogue, out_dtype, name):
    te, first, nxt, misc = plan
    nj = n_out // tn
    nw = len(col_offs)
    grid_spec = pltpu.PrefetchScalarGridSpec(
        num_scalar_prefetch=4,
        grid=(nj, MOE_STEPS),
        in_specs=[
            pl.BlockSpec((2 * MOE_SORT, k), lambda j, i, *_: (i, 0)),
            pl.BlockSpec(memory_space=pl.ANY),
        ],
        out_specs=pl.BlockSpec((2 * MOE_SORT, tn), lambda j, i, *_: (i, j)),
        scratch_shapes=[pltpu.VMEM((nw, k, tn), F32), pltpu.VMEM((nw, k, tn), BF16),
                        pltpu.SemaphoreType.DMA((nw,))],
    )
    return pl.pallas_call(
        functools.partial(_expert_matmul_kernel, tn=tn, nj=nj, col_offs=col_offs, epilogue=epilogue),
        grid_spec=grid_spec,
        out_shape=jax.ShapeDtypeStruct((MOE_ROWS, n_out), out_dtype),
        compiler_params=_cparams(2),
        name=name,
    )(te, first, nxt, misc, lhs, w)


def _moe_gu(plan, xs, w_gu):
    return _expert_matmul(plan, xs, w_gu, k=D, n_out=EH, tn=1024, col_offs=(0, EH),
                          epilogue=lambda gt, up: _silu(gt) * up, out_dtype=BF16, name="moe_gu")


def _moe_down(plan, act, w_down):
    return _expert_matmul(plan, act, w_down, k=EH, n_out=D, tn=512, col_offs=(0,),
                          epilogue=lambda acc: acc, out_dtype=F32, name="moe_down")


def _row_copy(src_hbm, row, buf, r, sem):
    return pltpu.make_async_copy(src_hbm.at[pl.ds(row, 1), :], buf.at[pl.ds(r, 1), :], sem)


def _start_rows(src_hbm, idx_ref, off, dst, sem, n):
    def body(p, carry):
        for q in range(2):
            r = 2 * p + q
            _row_copy(src_hbm, idx_ref[0, 0, off + r], dst, r, sem).start(priority=q)
        return carry

    lax.fori_loop(0, n // 2, body, 0, unroll=4)


def _wait_rows(src_hbm, dst, sem, n):
    pltpu.make_async_copy(src_hbm.at[pl.ds(0, n), :], dst, sem).wait()


def _gather_kernel(nu_ref, idx_ref, idx_next_ref, src_hbm, o_ref, buf, sem, *, tm):
    t = pl.program_id(0)
    nu = nu_ref[0]
    slot = lax.rem(t, 2)

    @pl.when(t == 0)
    def _():
        _start_rows(src_hbm, idx_ref, 0, buf.at[0], sem.at[0], tm)

    @pl.when(t + 1 < nu)
    def _():
        _start_rows(src_hbm, idx_next_ref, 0, buf.at[1 - slot], sem.at[1 - slot], tm)

    @pl.when(t < nu)
    def _():
        _wait_rows(src_hbm, buf.at[slot], sem.at[slot], tm)
        o_ref[...] = buf[slot].astype(o_ref.dtype)

    @pl.when(t >= nu)
    def _():
        o_ref[...] = jnp.zeros_like(o_ref)


def _gather_rows(nu, idx3, src):
    tm = 2 * MOE_SORT
    nt = MOE_STEPS
    grid_spec = pltpu.PrefetchScalarGridSpec(
        num_scalar_prefetch=1,
        grid=(nt,),
        in_specs=[
            pl.BlockSpec((1, 1, tm), lambda t, nu: (t, 0, 0), memory_space=pltpu.SMEM),
            pl.BlockSpec((1, 1, tm), lambda t, nu: (jnp.minimum(t + 1, nt - 1), 0, 0), memory_space=pltpu.SMEM),
            pl.BlockSpec(memory_space=pl.ANY),
        ],
        out_specs=pl.BlockSpec((tm, D), lambda t, nu: (t, 0)),
        scratch_shapes=[pltpu.VMEM((2, tm, D), F32), pltpu.SemaphoreType.DMA((2,))],
    )
    return pl.pallas_call(
        functools.partial(_gather_kernel, tm=tm),
        grid_spec=grid_spec,
        out_shape=jax.ShapeDtypeStruct((MOE_ROWS, D), BF16),
        compiler_params=_cparams(1),
        name="moe_gather",
    )(nu, idx3, idx3, src)


def _combine_kernel(pos_ref, pos_next_ref, ys_hbm, x_ref, gv_ref, gate_ref, fw_ref, op_ref, os_ref, buf, sem,
                    *, tm, nt):
    t = pl.program_id(0)
    slot = lax.rem(t, 2)

    def start(idx_ref, s):
        _start_rows(ys_hbm, idx_ref, 0, buf.at[s, 0], sem.at[s, 0], tm)
        _start_rows(ys_hbm, idx_ref, tm, buf.at[s, 1], sem.at[s, 1], tm)

    @pl.when(t == 0)
    def _():
        start(pos_ref, 0)

    @pl.when(t + 1 < nt)
    def _():
        start(pos_next_ref, 1 - slot)

    _wait_rows(ys_hbm, buf.at[slot, 0], sem.at[slot, 0], tm)
    _wait_rows(ys_hbm, buf.at[slot, 1], sem.at[slot, 1], tm)

    def fn(mods, g, s):
        (gate,) = mods
        gv = gv_ref[...]
        f = gv[:, 0:1] * buf[slot, 0] + gv[:, 1:2] * buf[slot, 1]
        x = x_ref[...].reshape(g, tm // g, D) + gate * f.reshape(g, tm // g, D)
        x = x.reshape(tm, D)
        ms = jnp.mean(x * x, axis=-1, keepdims=True)
        (op_ref, os_ref)[s][...] = x * lax.rsqrt(ms + EPS) * fw_ref[...]

    _grouped_apply(t, tm, [gate_ref], fn)


def _combine(pos3, ys, x, gv, mod, gate_blk, final_w, tm=256):
    nt = M // tm
    n_pt = MP // tm
    return pl.pallas_call(
        functools.partial(_combine_kernel, tm=tm, nt=nt),
        grid=(nt,),
        in_specs=[
            pl.BlockSpec((1, 1, 2 * tm), lambda t: (t, 0, 0), memory_space=pltpu.SMEM),
            pl.BlockSpec((1, 1, 2 * tm), lambda t: (jnp.minimum(t + 1, nt - 1), 0, 0), memory_space=pltpu.SMEM),
            pl.BlockSpec(memory_space=pl.ANY),
            pl.BlockSpec((tm, D), lambda t: (t, 0)),
            pl.BlockSpec((tm, 128), lambda t: (t, 0)),
            pl.BlockSpec((MOD_ROWS, D), lambda t: (0, gate_blk)),
            pl.BlockSpec((1, D), lambda t: (0, 0)),
        ],
        out_specs=[
            pl.BlockSpec((tm, D), lambda t: (jnp.minimum(t, n_pt - 1), 0)),
            pl.BlockSpec((tm, D), lambda t: (jnp.maximum(t - n_pt, 0), 0)),
        ],
        out_shape=[jax.ShapeDtypeStruct((MP, D), F32), jax.ShapeDtypeStruct((MS, D), F32)],
        scratch_shapes=[pltpu.VMEM((2, 2, tm, D), F32), pltpu.SemaphoreType.DMA((2, 2))],
        compiler_params=_cparams(1),
        name="moe_combine",
    )(pos3, pos3, ys, x, gv, mod, final_w)


def _ssd_kernel(z_ref, xbc_ref, dt_ref, cs_ref, h0_ref, cw_ref, cb_ref, dtb_ref, alog_ref, dsk_ref, nw_ref,
                e2_ref, et2_ref, y_ref, csn_ref, hn_ref, cbuf, hst, *, C, nc):
    c = pl.program_id(1)
    hp = 128 // C
    neg = jnp.float32(-jnp.inf)

    @pl.when(c == 0)
    def _():
        cbuf[0:8, :] = jnp.zeros((8, SSD_XBC), F32)
        cbuf[5:8, :] = cs_ref[...]
        hst[...] = h0_ref[...]

    cbuf[8:8 + C, :] = xbc_ref[...]

    @pl.when(c == nc - 1)
    def _():
        csn_ref[...] = cbuf[C + 5:C + 8, :]

    dt = _softplus(dt_ref[...] + dtb_ref[...])
    a = -jnp.exp(alog_ref[...]) * LOG2E
    row = lax.broadcasted_iota(I32, (C, C), 0)
    col = lax.broadcasted_iota(I32, (C, C), 1)
    tril = jnp.where(row >= col, 1.0, 0.0).astype(BF16)
    b = _dot_exact_rhs(tril, dt * a)
    eb = jnp.exp2(b)
    b_last = b[C - 1:C, :]
    dl = dt * jnp.exp2(b_last - b)
    stack = jnp.concatenate([dt, eb, dl], axis=0)
    s_hi = stack.astype(BF16)
    s_lo = (stack - s_hi.astype(F32)).astype(BF16)
    lhs2 = jnp.concatenate([s_hi, s_lo], axis=1)
    bst = jnp.concatenate([b] + [pltpu.roll(b, 128 - k, axis=1) for k in range(1, hp)], axis=0)
    bt = bst.T
    rdec = jnp.broadcast_to(jnp.exp2(bt[:, C - 1:C]), (128, 128))
    r_hi = rdec.astype(BF16)
    r_lo = (rdec - r_hi.astype(F32)).astype(BF16)
    r2 = jnp.concatenate([r_hi, r_lo], axis=0)

    lane = lax.broadcasted_iota(I32, (C, 128), 1)
    trow = lax.broadcasted_iota(I32, (C, 128), 0)
    causal = trow >= (lane & (C - 1))
    lane_seg = lane >> (C.bit_length() - 1)
    xlane_seg = lax.broadcasted_iota(I32, (C, hp * SSD_P), 1) >> (SSD_P.bit_length() - 1)

    def conv(cols):
        v = cbuf[0:C + 8, cols]
        acc = cb_ref[:, cols] + v[8:8 + C] * cw_ref[SSD_K - 1:SSD_K, cols]
        for k in range(SSD_K - 1):
            acc = acc + pltpu.roll(v, C + 3 - k, axis=0)[0:C] * cw_ref[k:k + 1, cols]
        return _silu(acc)

    for g in range(SSD_G):
        xc = slice(g * SSD_GW, (g + 1) * SSD_GW)
        xg = conv(xc)
        bg = conv(slice(SSD_DI + g * SSD_N, SSD_DI + (g + 1) * SSD_N)).astype(BF16)
        cg = conv(slice(SSD_DI + SSD_G * SSD_N + g * SSD_N, SSD_DI + SSD_G * SSD_N + (g + 1) * SSD_N)).astype(BF16)
        ex = jnp.dot(lhs2, e2_ref[:, xc], preferred_element_type=F32)
        dt_e, eb_e, dl_e = ex[0:C], ex[C:2 * C], ex[2 * C:3 * C]
        xdt = xg * dt_e
        cb2 = lax.dot_general(cg, jnp.concatenate([bg] * hp, axis=0), NT_DIMS, preferred_element_type=F32)
        hg = hst[g * SSD_GW:(g + 1) * SSD_GW, :]
        y_inter = lax.dot_general(cg, hg.astype(BF16), NT_DIMS, preferred_element_type=F32)
        ys = []
        for p in range(8 // hp):
            h0 = g * 8 + p * hp
            bcol = jnp.broadcast_to(b[:, h0:h0 + 1], (C, 128))
            for k in range(1, hp):
                bcol = jnp.where(lane_seg >= k, b[:, h0 + k:h0 + k + 1], bcol)
            lm = jnp.exp2(jnp.where(causal, bcol - bt[h0:h0 + 1, :], neg))
            m2 = (cb2 * lm).astype(BF16)
            xp = xdt[:, p * hp * SSD_P:(p + 1) * hp * SSD_P]
            rhs = jnp.concatenate([jnp.where(xlane_seg == k, xp, 0.0) for k in range(hp)], axis=0).astype(BF16)
            ys.append(jnp.dot(m2, rhs, preferred_element_type=F32))
        y = jnp.concatenate(ys, axis=1) + y_inter * eb_e + xg * dsk_ref[:, xc]
        y = y * _silu(z_ref[:, xc])
        ms = jnp.mean(y * y, axis=-1, keepdims=True)
        y_ref[:, xc] = (y * lax.rsqrt(ms + EPS) * nw_ref[:, xc]).astype(y_ref.dtype)
        upd = lax.dot_general((xg * dl_e).astype(BF16), bg, TN_DIMS, preferred_element_type=F32)
        dec = jnp.dot(et2_ref[g * SSD_GW:(g + 1) * SSD_GW, :], r2, preferred_element_type=F32)
        hst[g * SSD_GW:(g + 1) * SSD_GW, :] = hg * dec + upd

    cbuf[5:8, :] = cbuf[C + 5:C + 8, :]

    @pl.when(c == nc - 1)
    def _():
        hn_ref[...] = hst[...]


def _ssd_scan(z, xbc, dtr, conv_state, h0, consts, *, nb, seq, C, row0):
    nc = seq // C
    rb0 = row0 // C
    cw, cb, dtb, alog, dsk, nw, e2, et2 = consts
    rows = lambda b, c: (rb0 + b * nc + c, 0)
    fixed = lambda b, c: (0, 0)
    per_b = lambda b, c: (b, 0, 0)
    return pl.pallas_call(
        functools.partial(_ssd_kernel, C=C, nc=nc),
        grid=(nb, nc),
        in_specs=[
            pl.BlockSpec((C, SSD_DI), rows),
            pl.BlockSpec((C, SSD_XBC), rows),
            pl.BlockSpec((C, 128), rows),
            pl.BlockSpec((None, SSD_K - 1, SSD_XBC), per_b),
            pl.BlockSpec((None, SSD_DI, SSD_N), per_b),
            pl.BlockSpec((SSD_K, SSD_XBC), fixed),
            pl.BlockSpec((1, SSD_XBC), fixed),
            pl.BlockSpec((1, 128), fixed),
            pl.BlockSpec((1, 128), fixed),
            pl.BlockSpec((1, SSD_DI), fixed),
            pl.BlockSpec((1, SSD_DI), fixed),
            pl.BlockSpec((256, SSD_DI), fixed),
            pl.BlockSpec((SSD_DI, 256), fixed),
        ],
        out_specs=[
            pl.BlockSpec((C, SSD_DI), lambda b, c: (b * nc + c, 0)),
            pl.BlockSpec((None, SSD_K - 1, SSD_XBC), per_b),
            pl.BlockSpec((None, SSD_DI, SSD_N), per_b),
        ],
        out_shape=[
            jax.ShapeDtypeStruct((nb * seq, SSD_DI), BF16),
            jax.ShapeDtypeStruct((nb, SSD_K - 1, SSD_XBC), F32),
            jax.ShapeDtypeStruct((nb, SSD_DI, SSD_N), F32),
        ],
        scratch_shapes=[pltpu.VMEM((8 + C, SSD_XBC), F32), pltpu.VMEM((SSD_DI, SSD_N), F32)],
        compiler_params=_cparams(2),
        name=f"ssd_scan_c{C}",
    )(z, xbc, dtr, conv_state, h0, cw, cb, dtb, alog, dsk, nw, e2, et2)


def _gla_kernel(q_ref, f_ref, v_ref, g_ref, s0_ref, lb_ref, nw_ref, o_ref, sn_ref, st, b_s, c_s, *, T, nc):
    c = pl.program_id(1)
    neg = jnp.float32(-jnp.inf)

    @pl.when(c == 0)
    def _():
        for h in range(HG_H):
            st[h * 128:(h + 1) * 128, :] = s0_ref[h * 128:(h + 1) * 128, :].T

    row = lax.broadcasted_iota(I32, (T, T), 0)
    col = lax.broadcasted_iota(I32, (T, T), 1)
    sh = HG_BLK.bit_length() - 1
    same_blk = (row >> sh) == (col >> sh)
    bd = jnp.where(jnp.logical_and(same_blk, row >= col), 1.0, 0.0).astype(BF16)

    for h in range(HG_H):
        hl = slice(h * 128, (h + 1) * 128)
        fz = f_ref[:, hl]
        lb = lb_ref[:, hl]
        ls = jnp.minimum(fz, 0.0) - _log1p_unit(jnp.exp(-jnp.abs(fz)))
        a1 = jnp.log(lb)
        a2 = jnp.log1p(-lb) + ls
        lf = jnp.maximum(a1, a2) + _log1p_unit(jnp.exp(-jnp.abs(a1 - a2)))
        b2 = _dot_exact_rhs(bd, lf * LOG2E)
        b_s[:, hl] = b2
        c_s[:, hl] = b2 - (a2 - fz) * LOG2E

    trow = lax.broadcasted_iota(I32, (8, 128), 0)
    ones = jnp.ones((128, 128), BF16)
    hb = HG_BLK // 2

    def blk(j, carry):
        rows = pl.ds(pl.multiple_of(j * HG_BLK, HG_BLK), HG_BLK)
        for h in range(HG_H):
            hl = slice(h * 128, (h + 1) * 128)
            qb = q_ref[rows, hl]
            bb = b_s[rows, hl]
            cb = c_s[rows, hl]
            vb = v_ref[rows, hl]
            sth = st[hl, :]
            o = lax.dot_general((qb * jnp.exp2(bb)).astype(BF16), sth.astype(BF16), NT_DIMS,
                                preferred_element_type=F32)
            q_h, b_h = (qb[0:hb], qb[hb:]), (bb[0:hb], bb[hb:])
            pieces, owner = [], []
            for s in range(HG_BLK):
                cs = cb[s:s + 1, :]
                for half in range(2):
                    if s >= hb * (half + 1):
                        continue
                    e = b_h[half] - cs
                    if s >= hb * half:
                        e = jnp.where(trow >= s - hb * half, e, neg)
                    pieces.append(q_h[half] * jnp.exp2(e))
                    owner.append((s, half))
            att = jnp.dot(jnp.concatenate(pieces, axis=0).astype(BF16), ones, preferred_element_type=F32)
            o_h = [o[0:hb], o[hb:]]
            for n, (s, half) in enumerate(owner):
                o_h[half] = o_h[half] + att[n * hb:(n + 1) * hb] * vb[s:s + 1, :]
            o = jnp.concatenate(o_h, axis=0)
            b_end = bb[HG_BLK - 1:HG_BLK, :]
            khat = jnp.exp2(b_end - cb).astype(BF16)
            upd = lax.dot_general(vb.astype(BF16), khat, TN_DIMS, preferred_element_type=F32)
            st[hl, :] = sth * jnp.exp2(b_end) + upd
            ms = jnp.mean(o * o, axis=-1, keepdims=True)
            o_ref[rows, hl] = (o * lax.rsqrt(ms + EPS) * nw_ref[...] * _silu(g_ref[rows, hl])).astype(o_ref.dtype)
        return carry

    lax.fori_loop(0, T // HG_BLK, blk, 0)

    @pl.when(c == nc - 1)
    def _():
        for h in range(HG_H):
            sn_ref[h * 128:(h + 1) * 128, :] = st[h * 128:(h + 1) * 128, :].T


def _gla_scan(qfig, s0, lb, nw, *, nb, seq, T, row0):
    nc = seq // T
    rb0 = row0 // T
    per_b = lambda b, c: (b, 0, 0)

    def cols(k):
        return pl.BlockSpec((T, D), lambda b, c: (rb0 + b * nc + c, k))

    return pl.pallas_call(
        functools.partial(_gla_kernel, T=T, nc=nc),
        grid=(nb, nc),
        in_specs=[
            cols(0), cols(1), cols(2), cols(3),
            pl.BlockSpec((None, D, HG_DV), per_b),
            pl.BlockSpec((1, D), lambda b, c: (0, 0)),
            pl.BlockSpec((1, HG_DV), lambda b, c: (0, 0)),
        ],
        out_specs=[
            pl.BlockSpec((T, D), lambda b, c: (b * nc + c, 0)),
            pl.BlockSpec((None, D, HG_DV), per_b),
        ],
        out_shape=[
            jax.ShapeDtypeStruct((nb * seq, D), BF16),
            jax.ShapeDtypeStruct((nb, D, HG_DV), F32),
        ],
        scratch_shapes=[pltpu.VMEM((D, HG_DK), F32), pltpu.VMEM((T, D), F32), pltpu.VMEM((T, D), F32)],
        compiler_params=_cparams(2),
        name=f"gla_scan_t{T}",
    )(qfig, qfig, qfig, qfig, s0, lb, nw)


def _route_plan(top_i):
    e_flat = top_i.reshape(-1)
    onehot = (e_flat[:, None] == jnp.arange(NE, dtype=I32)[None, :]).astype(I32)
    csum = jnp.cumsum(onehot, axis=0)
    rank = jnp.sum(onehot * (csum - 1), axis=1)
    cnt = csum[-1]
    ntile = (cnt + MOE_SORT - 1) // MOE_SORT
    cum_t = jnp.cumsum(ntile)
    tile0 = cum_t - ntile
    pos = jnp.sum(onehot * tile0[None, :], axis=1) * MOE_SORT + rank
    n_used = cum_t[-1]
    t_ids = jnp.arange(MOE_TILES, dtype=I32)
    te = jnp.sum((t_ids[:, None] >= cum_t[None, :]).astype(I32), axis=1)
    te_last = jnp.sum((n_used - 1 >= cum_t).astype(I32))
    te = jnp.where(t_ids < n_used, te, te_last).astype(I32)
    prev = jnp.concatenate([jnp.full((1,), -1, I32), te[:-1]])
    first = jnp.logical_and(t_ids < n_used, te != prev).astype(I32)
    e_ids = jnp.arange(NE, dtype=I32)
    later = jnp.logical_and(e_ids[None, :] > e_ids[:, None], ntile[None, :] > 0)
    nxt = jnp.min(jnp.where(later, e_ids[None, :], NE), axis=1)
    nxt = jnp.where(nxt == NE, -1, nxt).astype(I32)
    misc = jnp.stack([n_used, te[0]]).astype(I32)
    tok = jnp.arange(2 * M, dtype=I32) // 2
    src = jnp.zeros((MOE_ROWS,), I32).at[pos].set(tok)
    n_steps = ((n_used + 1) // 2).reshape(1).astype(I32)
    return (te, first, nxt, misc), n_steps, src, pos.reshape(M, 2)


def kernel(x_prompt, x_sample, c_prompt, c_sample, state_ssd_conv, state_ssd, state_hgrn, ada_w, ada_b, norm_w,
           ssd_w_in, ssd_conv_w, ssd_conv_b, ssd_dt_bias, ssd_a_log, ssd_d, ssd_norm_w, ssd_w_out, hgrn_w_in,
           hgrn_lb_logits, hgrn_norm_w, hgrn_w_out, ffn_w_gu, ffn_w_down, moe_router, moe_w_gu, moe_w_down,
           final_norm_w):
    x = (x_prompt.reshape(MP, D), x_sample.reshape(MS, D))
    c_all = jnp.concatenate([c_prompt, jnp.zeros((MOD_S0 - NP, D), F32), c_sample], axis=0)
    norm_w4 = norm_w.reshape(4, 1, D)

    mods = [
        _mm_plain(c_all, ada_w, layer=l, col0=0, n_out=6 * D, tn=1024, tm=MOD_ROWS,
                  bias=ada_b[l].reshape(1, 6 * D), act_lhs=True, name="adaln")
        for l in range(2)
    ]

    hn = _prenorm(x, norm_w4, 0, mods[0], 0, 1, BF16)
    w_in_t = jnp.swapaxes(ssd_w_in, 1, 2)
    z = _mm_plain(hn, w_in_t, layer=0, col0=0, n_out=SSD_DI, tn=1024, tm=1024, w_t=True, name="ssd_in_z")
    xbc = _mm_plain(hn, w_in_t, layer=0, col0=SSD_DI, n_out=SSD_XBC, tn=1024, tm=1024, w_t=True,
                    name="ssd_in_xbc")
    dtr = _mm_plain(hn, w_in_t, layer=0, col0=SSD_DI + SSD_XBC, n_out=128, tn=128, tm=1024, w_t=True,
                    name="ssd_in_dt")

    pad64 = lambda v: jnp.pad(v.reshape(1, SSD_HEADS), ((0, 0), (0, 128 - SSD_HEADS)))
    head_of = jnp.arange(SSD_DI, dtype=I32) // SSD_P
    e1 = (jnp.arange(128, dtype=I32)[:, None] == head_of[None, :]).astype(BF16)
    e2 = jnp.concatenate([e1, e1], axis=0)
    et2 = jnp.concatenate([e1.T, e1.T], axis=1)
    consts = (ssd_conv_w[0], ssd_conv_b[0].reshape(1, SSD_XBC), pad64(ssd_dt_bias[0]), pad64(ssd_a_log[0]),
              jnp.repeat(ssd_d[0], SSD_P).reshape(1, SSD_DI), ssd_norm_w[0].reshape(1, SSD_DI), e2, et2)
    yp, p_conv, p_ssd = _ssd_scan(z, xbc, dtr, jnp.zeros((NP, SSD_K - 1, SSD_XBC), F32),
                                  jnp.zeros((NP, SSD_DI, SSD_N), F32), consts, nb=NP, seq=LP, C=64, row0=0)
    ysm, s_conv, s_ssd = _ssd_scan(z, xbc, dtr, state_ssd_conv[0], state_ssd[0].reshape(NS, SSD_DI, SSD_N), consts,
                                   nb=NS, seq=LS, C=32, row0=MP)
    x = _mm_resid((yp, ysm), ssd_w_out, x, mods[0], 2, k=SSD_DI, tn=1024, tm=512, name="ssd_out")

    hn = _prenorm(x, norm_w4, 1, mods[0], 3, 4, BF16)
    act = _mm_swiglu(hn, ffn_w_gu, FFN, tn=512, tm=1024, name="ffn_gu")
    x = _mm_resid(act, ffn_w_down, x, mods[0], 5, k=FFN, tn=1024, tm=512, name="ffn_down")

    hn = _prenorm(x, norm_w4, 2, mods[1], 0, 1, BF16)
    qfig = _mm_plain(hn, hgrn_w_in, layer=0, col0=0, n_out=4 * D, tn=1024, tm=1024, name="hgrn_in")
    p = jax.nn.softmax(hgrn_lb_logits.astype(F32), axis=0)
    lb = (jnp.cumsum(p, axis=0) - p[0])[1].reshape(1, D)
    nw_h = hgrn_norm_w[0].reshape(1, HG_DV)
    op, p_hgrn = _gla_scan(qfig, jnp.zeros((NP, D, HG_DV), F32), lb, nw_h, nb=NP, seq=LP, T=64, row0=0)
    osm, s_hgrn = _gla_scan(qfig, state_hgrn[0].reshape(NS, D, HG_DV), lb, nw_h, nb=NS, seq=LS, T=32, row0=MP)
    x = _mm_resid((op, osm), hgrn_w_out, x, mods[1], 2, k=D, tn=1024, tm=512, name="hgrn_out")

    hn32, gv, gi = _prenorm(x, norm_w4, 3, mods[1], 3, 4, F32,
                            router_pad=jnp.pad(moe_router[0], ((0, 0), (0, 128 - NE))))
    plan, n_steps, src, pos = _route_plan(gi[:, :2])
    xs = _gather_rows(n_steps, src.reshape(MOE_STEPS, 1, 2 * MOE_SORT), hn32)
    act = _moe_gu(plan, xs, moe_w_gu[0])
    ys = _moe_down(plan, act, moe_w_down[0])
    tmc = 256
    pos3 = jnp.concatenate([pos[:, 0].reshape(M // tmc, 1, tmc), pos[:, 1].reshape(M // tmc, 1, tmc)], axis=2)
    yo_p, yo_s = _combine(pos3, ys, x, gv, mods[1], 5, final_norm_w.reshape(1, D), tm=tmc)

    return (
        yo_p.reshape(NP, LP, D),
        yo_s.reshape(NS, LS, D),
        p_conv[None],
        p_ssd.reshape(1, NP, SSD_HEADS, SSD_P, SSD_N),
        p_hgrn.reshape(1, NP, HG_H, HG_DK, HG_DV),
        s_conv[None],
        s_ssd.reshape(1, NS, SSD_HEADS, SSD_P, SSD_N),
        s_hgrn.reshape(1, NS, HG_H, HG_DK, HG_DV),
    )
```

```python
import functools

import jax
import jax.numpy as jnp
from jax import lax
from jax.experimental import pallas as pl
from jax.experimental.pallas import tpu as pltpu

F32 = jnp.float32
BF16 = jnp.bfloat16
I32 = jnp.int32

EPS = 1e-6
LOG2E = 1.4426950408889634
D = 2048
NP, LP = 4, 2048
NS, LS = 32, 32
MP, MS = NP * LP, NS * LS
M = MP + MS
MOD_S0 = 8
MOD_ROWS = MOD_S0 + NS

SSD_DI = 4096
SSD_HEADS = 64
SSD_P = 64
SSD_G = 8
SSD_GW = SSD_DI // SSD_G
SSD_N = 128
SSD_XBC = SSD_DI + 2 * SSD_G * SSD_N
SSD_K = 4

HG_H = 16
HG_DK = 128
HG_DV = 128
HG_BLK = 16

FFN = 5632
NE = 8
EH = 7168
MOE_SORT = 256
MOE_TILES = (2 * M) // MOE_SORT + NE
MOE_STEPS = MOE_TILES // 2
MOE_ROWS = MOE_TILES * MOE_SORT
CAST_ROWS = 64

VMEM_LIMIT = 60 * 1024 * 1024

NT_DIMS = (((1,), (1,)), ((), ()))
TN_DIMS = (((0,), (0,)), ((), ()))


def _cparams(n_axes):
    return pltpu.CompilerParams(dimension_semantics=("arbitrary",) * n_axes,
                                vmem_limit_bytes=VMEM_LIMIT)


def _silu(x):
    h = 0.5 * x
    return h + h * jnp.tanh(h)


def _softplus(x):
    return jnp.maximum(x, 0.0) + jnp.log1p(jnp.exp(-jnp.abs(x)))


def _log1p_unit(u):
    return jnp.log(1.0 + u)


def _split3(x):
    hi = x.astype(BF16)
    r1 = x - hi.astype(F32)
    mid = r1.astype(BF16)
    lo = (r1 - mid.astype(F32)).astype(BF16)
    return hi, mid, lo


def _dot_exact_rhs(mat_bf16, x):
    hi, mid, lo = _split3(x)
    acc = jnp.dot(mat_bf16, hi, preferred_element_type=F32)
    acc = acc + jnp.dot(mat_bf16, mid, preferred_element_type=F32)
    return acc + jnp.dot(mat_bf16, lo, preferred_element_type=F32)


def _grouped_apply(i, tm, mod_refs, fn):
    n_pt = MP // tm
    tpb = LP // tm

    @pl.when(i < n_pt)
    def _():
        r = i // tpb
        fn([m[pl.ds(r, 1), :][None] for m in mod_refs], 1, 0)

    @pl.when(i >= n_pt)
    def _():
        g = tm // LS
        start = pl.multiple_of(MOD_S0 + (i - n_pt) * g, 8)
        fn([m[pl.ds(start, g), :][:, None, :] for m in mod_refs], g, 1)


def _row_operand(op, tm, width, ij):
    if not isinstance(op, tuple):
        return [pl.BlockSpec((tm, width), lambda *g: ij(*g))], [op]
    n_pt = MP // tm
    return [
        pl.BlockSpec((tm, width), lambda *g: (jnp.minimum(ij(*g)[0], n_pt - 1), ij(*g)[1])),
        pl.BlockSpec((tm, width), lambda *g: (jnp.maximum(ij(*g)[0] - n_pt, 0), ij(*g)[1])),
    ], list(op)


def _top2_gates(logits):
    lane = lax.broadcasted_iota(I32, logits.shape, 1)
    neg = jnp.float32(-jnp.inf)
    l1 = jnp.where(lane < NE, logits, neg)
    m1 = jnp.max(l1, axis=-1, keepdims=True)
    i1 = jnp.min(jnp.where(l1 == m1, lane, 128), axis=-1, keepdims=True)
    l2 = jnp.where(lane == i1, neg, l1)
    m2 = jnp.max(l2, axis=-1, keepdims=True)
    i2 = jnp.min(jnp.where(l2 == m2, lane, 128), axis=-1, keepdims=True)
    e = jnp.exp(m2 - m1)
    g1 = 1.0 / (1.0 + e)
    g2 = e * g1
    gv = jnp.where(lane == 0, g1, jnp.where(lane == 1, g2, 0.0))
    gi = jnp.where(lane == 0, i1, jnp.where(lane == 1, i2, 0))
    return gv, gi


def _prenorm_kernel(*refs, tm, nx, route):
    x_refs = refs[:nx]
    if route:
        nw_ref, sc_ref, sh_ref, r_ref, o_ref, gv_ref, gi_ref = refs[nx:]
    else:
        nw_ref, sc_ref, sh_ref, o_ref = refs[nx:]
    i = pl.program_id(0)

    def fn(mods, g, s):
        sc, sh = mods
        x = x_refs[s % nx][...]
        ms = jnp.mean(x * x, axis=-1, keepdims=True)
        y = x * lax.rsqrt(ms + EPS) * nw_ref[...]
        y = (y.reshape(g, tm // g, D) * (1.0 + sc) + sh).reshape(tm, D)
        o_ref[...] = y.astype(o_ref.dtype)
        if route:
            r = r_ref[...]
            y_hi, r_hi = y.astype(BF16), r.astype(BF16)
            y_lo = (y - y_hi.astype(F32)).astype(BF16)
            r_lo = (r - r_hi.astype(F32)).astype(BF16)
            logits = (jnp.dot(y_hi, r_hi, preferred_element_type=F32)
                      + jnp.dot(y_lo, r_hi, preferred_element_type=F32)
                      + jnp.dot(y_hi, r_lo, preferred_element_type=F32))
            gv_ref[...], gi_ref[...] = _top2_gates(logits)

    _grouped_apply(i, tm, [sc_ref, sh_ref], fn)


def _prenorm(x, norm_w4, k, mod, sh_blk, sc_blk, out_dtype, tm=512, router_pad=None):
    x_specs, x_args = _row_operand(x, tm, D, lambda i: (i, 0))
    route = router_pad is not None
    row_out = lambda w: pl.BlockSpec((tm, w), lambda i: (i, 0))
    out_specs = [row_out(D)]
    out_shape = [jax.ShapeDtypeStruct((M, D), out_dtype)]
    extra_specs, extra_args = [], []
    if route:
        extra_specs = [pl.BlockSpec((D, 128), lambda i: (0, 0))]
        extra_args = [router_pad]
        out_specs += [row_out(128), row_out(128)]
        out_shape += [jax.ShapeDtypeStruct((M, 128), F32), jax.ShapeDtypeStruct((M, 128), I32)]
    res = pl.pallas_call(
        functools.partial(_prenorm_kernel, tm=tm, nx=len(x_args), route=route),
        grid=(M // tm,),
        in_specs=x_specs + [
            pl.BlockSpec((None, 1, D), lambda i: (k, 0, 0)),
            pl.BlockSpec((MOD_ROWS, D), lambda i: (0, sc_blk)),
            pl.BlockSpec((MOD_ROWS, D), lambda i: (0, sh_blk)),
        ] + extra_specs,
        out_specs=out_specs,
        out_shape=out_shape,
        compiler_params=_cparams(1),
        name="prenorm_route" if route else "prenorm",
    )(*x_args, norm_w4, mod, mod, *extra_args)
    return res if route else res[0]


def _mm_plain_kernel(*refs, act_lhs, has_bias, valid_cols, w_t):
    if has_bias:
        lhs_ref, w_ref, b_ref, o_ref, wb = refs
    else:
        lhs_ref, w_ref, o_ref, wb = refs

    @pl.when(pl.program_id(1) == 0)
    def _():
        w = w_ref[...]
        if valid_cols is not None:
            w = jnp.where(lax.broadcasted_iota(I32, w.shape, 0 if w_t else 1) < valid_cols, w, 0.0)
        wb[...] = w.astype(BF16)

    lhs = lhs_ref[...]
    if act_lhs:
        lhs = _silu(lhs).astype(BF16)
    if w_t:
        acc = lax.dot_general(lhs, wb[...], NT_DIMS, preferred_element_type=F32)
    else:
        acc = jnp.dot(lhs, wb[...], preferred_element_type=F32)
    if has_bias:
        acc = acc + b_ref[...]
    o_ref[...] = acc.astype(o_ref.dtype)


def _mm_plain(lhs, w3, *, layer, col0, n_out, tn, tm, out_dtype=F32, bias=None, act_lhs=False, w_t=False,
              name="mm"):
    rows, k = lhs.shape
    off = col0 // tn
    w_cols = w3.shape[1 if w_t else 2] - col0
    valid_cols = w_cols if w_cols < n_out else None
    if w_t:
        w_spec = pl.BlockSpec((None, tn, k), lambda j, i: (layer, j + off, 0))
    else:
        w_spec = pl.BlockSpec((None, k, tn), lambda j, i: (layer, 0, j + off))
    in_specs = [pl.BlockSpec((tm, k), lambda j, i: (i, 0)), w_spec]
    args = [lhs, w3]
    if bias is not None:
        in_specs.append(pl.BlockSpec((1, tn), lambda j, i: (0, j)))
        args.append(bias)
    return pl.pallas_call(
        functools.partial(_mm_plain_kernel, act_lhs=act_lhs, has_bias=bias is not None, valid_cols=valid_cols,
                          w_t=w_t),
        grid=(n_out // tn, rows // tm),
        in_specs=in_specs,
        out_specs=pl.BlockSpec((tm, tn), lambda j, i: (i, j)),
        out_shape=jax.ShapeDtypeStruct((rows, n_out), out_dtype),
        scratch_shapes=[pltpu.VMEM((tn, k) if w_t else (k, tn), BF16)],
        compiler_params=_cparams(2),
        name=name,
    )(*args)


def _mm_resid_kernel(*refs, tm, tn, nl, nr):
    lhs_refs = refs[:nl]
    w_ref = refs[nl]
    res_refs = refs[nl + 1:nl + 1 + nr]
    gate_ref, o_ref, wb = refs[nl + 1 + nr:]
    i = pl.program_id(1)

    @pl.when(i == 0)
    def _():
        wb[...] = w_ref[...].astype(BF16)

    def fn(mods, g, s):
        (gate,) = mods
        acc = jnp.dot(lhs_refs[s % nl][...], wb[...], preferred_element_type=F32)
        out = res_refs[s % nr][...].reshape(g, tm // g, tn) + gate * acc.reshape(g, tm // g, tn)
        o_ref[...] = out.reshape(tm, tn)

    _grouped_apply(i, tm, [gate_ref], fn)


def _mm_resid(lhs, w3, res, mod, gate_blk, *, k, tn, tm, name):
    goff = gate_blk * (D // tn)
    l_specs, l_args = _row_operand(lhs, tm, k, lambda j, i: (i, 0))
    r_specs, r_args = _row_operand(res, tm, tn, lambda j, i: (i, j))
    return pl.pallas_call(
        functools.partial(_mm_resid_kernel, tm=tm, tn=tn, nl=len(l_args), nr=len(r_args)),
        grid=(D // tn, M // tm),
        in_specs=l_specs + [pl.BlockSpec((None, k, tn), lambda j, i: (0, 0, j),
                                         pipeline_mode=pl.Buffered(1))] + r_specs + [
            pl.BlockSpec((MOD_ROWS, tn), lambda j, i: (0, goff + j)),
        ],
        out_specs=pl.BlockSpec((tm, tn), lambda j, i: (i, j)),
        out_shape=jax.ShapeDtypeStruct((M, D), F32),
        scratch_shapes=[pltpu.VMEM((k, tn), BF16)],
        compiler_params=_cparams(2),
        name=name,
    )(*l_args, w3, *r_args, mod)


def _mm_swiglu_kernel(lhs_ref, wg_ref, wu_ref, o_ref, wgb, wub):
    @pl.when(pl.program_id(1) == 0)
    def _():
        wgb[...] = wg_ref[...].astype(BF16)
        wub[...] = wu_ref[...].astype(BF16)

    lhs = lhs_ref[...]
    gt = jnp.dot(lhs, wgb[...], preferred_element_type=F32)
    up = jnp.dot(lhs, wub[...], preferred_element_type=F32)
    o_ref[...] = (_silu(gt) * up).astype(o_ref.dtype)


def _mm_swiglu(lhs, w3, hidden, *, tn, tm, name):
    rows, k = lhs.shape
    nb = hidden // tn
    return pl.pallas_call(
        _mm_swiglu_kernel,
        grid=(nb, rows // tm),
        in_specs=[
            pl.BlockSpec((tm, k), lambda j, i: (i, 0)),
            pl.BlockSpec((None, k, tn), lambda j, i: (0, 0, j)),
            pl.BlockSpec((None, k, tn), lambda j, i: (0, 0, j + nb)),
        ],
        out_specs=pl.BlockSpec((tm, tn), lambda j, i: (i, j)),
        out_shape=jax.ShapeDtypeStruct((rows, hidden), BF16),
        scratch_shapes=[pltpu.VMEM((k, tn), BF16), pltpu.VMEM((k, tn), BF16)],
        compiler_params=_cparams(2),
        name=name,
    )(lhs, w3, w3)


def _expert_matmul_kernel(*refs, tn, nj, col_offs, epilogue, k0, has_prev):
    te_ref, first_ref, nxt_ref, misc_ref, lhs_ref, w_hbm = refs[:6]
    prev_ref = refs[6] if has_prev else None
    o_ref, stage, panel, sem = refs[6 + has_prev:]
    j = pl.program_id(0)
    i = pl.program_id(1)
    nu = misc_ref[0]
    first_e = misc_ref[1]
    half = MOE_SORT
    nw = len(col_offs)
    kk = stage.shape[1]

    def weight_copies(e, jj):
        c0 = pl.multiple_of(jj * tn, 128)
        return [pltpu.make_async_copy(w_hbm.at[e, pl.ds(k0, kk), pl.ds(off + c0, tn)], stage.at[n], sem.at[n])
                for n, off in enumerate(col_offs)]

    def change(t, very_first):
        e_new = te_ref[t]

        @pl.when(very_first)
        def _():
            for cp in weight_copies(e_new, j):
                cp.start()

        for cp in weight_copies(e_new, j):
            cp.wait()

        def cast_rows(c, carry):
            r = pl.multiple_of(c * CAST_ROWS, CAST_ROWS)
            for n in range(nw):
                panel[n, pl.ds(r, CAST_ROWS), :] = stage[n, pl.ds(r, CAST_ROWS), :].astype(BF16)
            return carry

        lax.fori_loop(0, stage.shape[1] // CAST_ROWS, cast_rows, 0)
        ne = nxt_ref[e_new]
        same_col = ne >= 0

        @pl.when(jnp.logical_or(same_col, j + 1 < nj))
        def _():
            for cp in weight_copies(jnp.where(same_col, ne, first_e), jnp.where(same_col, j, j + 1)):
                cp.start()

    def rows_dot(r0, rows):
        lhs = lhs_ref[r0:r0 + rows, :]
        accs = [jnp.dot(lhs, panel[n], preferred_element_type=F32) for n in range(nw)]
        res = epilogue(*accs)
        if has_prev:
            res = prev_ref[r0:r0 + rows, :] + res
        o_ref[r0:r0 + rows, :] = res.astype(o_ref.dtype)

    t0 = 2 * i
    t1 = t0 + 1
    v0 = t0 < nu
    v1 = t1 < nu
    f0 = first_ref[t0] == 1
    f1 = first_ref[t1] == 1

    @pl.when(f0)
    def _():
        change(t0, jnp.logical_and(j == 0, i == 0))

    @pl.when(jnp.logical_and(v1, jnp.logical_not(f1)))
    def _():
        rows_dot(0, 2 * half)

    @pl.when(f1)
    def _():
        rows_dot(0, half)
        change(t1, False)
        rows_dot(half, half)

    @pl.when(jnp.logical_and(v0, jnp.logical_not(v1)))
    def _():
        rows_dot(0, half)
        o_ref[half:, :] = jnp.zeros((half, tn), o_ref.dtype)

    @pl.when(jnp.logical_not(v0))
    def _():
        o_ref[...] = jnp.zeros_like(o_ref)


def _expert_matmul(plan, lhs, w, *, k, n_out, tn, col_offs, epilogue, out_dtype, name, k0=0, prev=None):
    te, first, nxt, misc = plan
    nj = n_out // tn
    nw = len(col_offs)
    kb = k0 // k
    row_tile = lambda w_: pl.BlockSpec((2 * MOE_SORT, w_), lambda j, i, *_: (i, j))
    in_specs = [
        pl.BlockSpec((2 * MOE_SORT, k), lambda j, i, *_: (i, kb)),
        pl.BlockSpec(memory_space=pl.ANY),
    ]
    args = [lhs, w]
    if prev is not None:
        in_specs.append(row_tile(tn))
        args.append(prev)
    grid_spec = pltpu.PrefetchScalarGridSpec(
        num_scalar_prefetch=4,
        grid=(nj, MOE_STEPS),
        in_specs=in_specs,
        out_specs=row_tile(tn),
        scratch_shapes=[pltpu.VMEM((nw, k, tn), F32), pltpu.VMEM((nw, k, tn), BF16),
                        pltpu.SemaphoreType.DMA((nw,))],
    )
    return pl.pallas_call(
        functools.partial(_expert_matmul_kernel, tn=tn, nj=nj, col_offs=col_offs, epilogue=epilogue, k0=k0,
                          has_prev=prev is not None),
        grid_spec=grid_spec,
        out_shape=jax.ShapeDtypeStruct((MOE_ROWS, n_out), out_dtype),
        compiler_params=_cparams(2),
        name=name,
    )(te, first, nxt, misc, *args)


def _moe_gu(plan, xs, w_gu):
    return _expert_matmul(plan, xs, w_gu, k=D, n_out=EH, tn=1024, col_offs=(0, EH),
                          epilogue=lambda gt, up: _silu(gt) * up, out_dtype=BF16, name="moe_gu")


def _moe_down(plan, act, w_down):
    kh = EH // 2
    common = dict(k=kh, n_out=D, tn=1024, col_offs=(0,), epilogue=lambda acc: acc, out_dtype=F32)
    part = _expert_matmul(plan, act, w_down, k0=0, name="moe_down_a", **common)
    return _expert_matmul(plan, act, w_down, k0=kh, prev=part, name="moe_down_b", **common)


def _row_copy(src_hbm, row, buf, r, sem):
    return pltpu.make_async_copy(src_hbm.at[pl.ds(row, 1), :], buf.at[pl.ds(r, 1), :], sem)


def _start_rows(src_hbm, idx_ref, off, dst, sem, n):
    def body(p, carry):
        for q in range(2):
            r = 2 * p + q
            _row_copy(src_hbm, idx_ref[0, 0, off + r], dst, r, sem).start(priority=q)
        return carry

    lax.fori_loop(0, n // 2, body, 0, unroll=4)


def _wait_rows(src_hbm, dst, sem, n):
    pltpu.make_async_copy(src_hbm.at[pl.ds(0, n), :], dst, sem).wait()


def _gather_kernel(nu_ref, idx_ref, idx_next_ref, src_hbm, o_ref, buf, sem, *, tm):
    t = pl.program_id(0)
    nu = nu_ref[0]
    slot = lax.rem(t, 2)

    @pl.when(t == 0)
    def _():
        _start_rows(src_hbm, idx_ref, 0, buf.at[0], sem.at[0], tm)

    @pl.when(t + 1 < nu)
    def _():
        _start_rows(src_hbm, idx_next_ref, 0, buf.at[1 - slot], sem.at[1 - slot], tm)

    @pl.when(t < nu)
    def _():
        _wait_rows(src_hbm, buf.at[slot], sem.at[slot], tm)
        o_ref[...] = buf[slot].astype(o_ref.dtype)

    @pl.when(t >= nu)
    def _():
        o_ref[...] = jnp.zeros_like(o_ref)


def _gather_rows(nu, idx3, src):
    tm = 2 * MOE_SORT
    nt = MOE_STEPS
    grid_spec = pltpu.PrefetchScalarGridSpec(
        num_scalar_prefetch=1,
        grid=(nt,),
        in_specs=[
            pl.BlockSpec((1, 1, tm), lambda t, nu: (t, 0, 0), memory_space=pltpu.SMEM),
            pl.BlockSpec((1, 1, tm), lambda t, nu: (jnp.minimum(t + 1, nt - 1), 0, 0), memory_space=pltpu.SMEM),
            pl.BlockSpec(memory_space=pl.ANY),
        ],
        out_specs=pl.BlockSpec((tm, D), lambda t, nu: (t, 0)),
        scratch_shapes=[pltpu.VMEM((2, tm, D), F32), pltpu.SemaphoreType.DMA((2,))],
    )
    return pl.pallas_call(
        functools.partial(_gather_kernel, tm=tm),
        grid_spec=grid_spec,
        out_shape=jax.ShapeDtypeStruct((MOE_ROWS, D), BF16),
        compiler_params=_cparams(1),
        name="moe_gather",
    )(nu, idx3, idx3, src)


def _combine_kernel(pos_ref, pos_next_ref, ys_hbm, x_ref, gv_ref, gate_ref, fw_ref, op_ref, os_ref, buf, sem,
                    *, tm, nt):
    t = pl.program_id(0)
    slot = lax.rem(t, 2)

    def start(idx_ref, s):
        _start_rows(ys_hbm, idx_ref, 0, buf.at[s, 0], sem.at[s, 0], tm)
        _start_rows(ys_hbm, idx_ref, tm, buf.at[s, 1], sem.at[s, 1], tm)

    @pl.when(t == 0)
    def _():
        start(pos_ref, 0)

    @pl.when(t + 1 < nt)
    def _():
        start(pos_next_ref, 1 - slot)

    _wait_rows(ys_hbm, buf.at[slot, 0], sem.at[slot, 0], tm)
    _wait_rows(ys_hbm, buf.at[slot, 1], sem.at[slot, 1], tm)

    def fn(mods, g, s):
        (gate,) = mods
        gv = gv_ref[...]
        f = gv[:, 0:1] * buf[slot, 0] + gv[:, 1:2] * buf[slot, 1]
        x = x_ref[...].reshape(g, tm // g, D) + gate * f.reshape(g, tm // g, D)
        x = x.reshape(tm, D)
        ms = jnp.mean(x * x, axis=-1, keepdims=True)
        (op_ref, os_ref)[s][...] = x * lax.rsqrt(ms + EPS) * fw_ref[...]

    _grouped_apply(t, tm, [gate_ref], fn)


def _combine(pos3, ys, x, gv, mod, gate_blk, final_w, tm=256):
    nt = M // tm
    n_pt = MP // tm
    return pl.pallas_call(
        functools.partial(_combine_kernel, tm=tm, nt=nt),
        grid=(nt,),
        in_specs=[
            pl.BlockSpec((1, 1, 2 * tm), lambda t: (t, 0, 0), memory_space=pltpu.SMEM),
            pl.BlockSpec((1, 1, 2 * tm), lambda t: (jnp.minimum(t + 1, nt - 1), 0, 0), memory_space=pltpu.SMEM),
            pl.BlockSpec(memory_space=pl.ANY),
            pl.BlockSpec((tm, D), lambda t: (t, 0)),
            pl.BlockSpec((tm, 128), lambda t: (t, 0)),
            pl.BlockSpec((MOD_ROWS, D), lambda t: (0, gate_blk)),
            pl.BlockSpec((1, D), lambda t: (0, 0)),
        ],
        out_specs=[
            pl.BlockSpec((tm, D), lambda t: (jnp.minimum(t, n_pt - 1), 0)),
            pl.BlockSpec((tm, D), lambda t: (jnp.maximum(t - n_pt, 0), 0)),
        ],
        out_shape=[jax.ShapeDtypeStruct((MP, D), F32), jax.ShapeDtypeStruct((MS, D), F32)],
        scratch_shapes=[pltpu.VMEM((2, 2, tm, D), F32), pltpu.SemaphoreType.DMA((2, 2))],
        compiler_params=_cparams(1),
        name="moe_combine",
    )(pos3, pos3, ys, x, gv, mod, final_w)


def _ssd_kernel(z_ref, xbc_ref, dt_ref, cs_ref, h0_ref, cw_ref, cb_ref, dtb_ref, alog_ref, dsk_ref, nw_ref,
                e2_ref, et2_ref, y_ref, csn_ref, hn_ref, cbuf, hst, *, C, nc):
    c = pl.program_id(1)
    hp = 128 // C
    neg = jnp.float32(-jnp.inf)

    @pl.when(c == 0)
    def _():
        cbuf[0:8, :] = jnp.zeros((8, SSD_XBC), F32)
        cbuf[5:8, :] = cs_ref[...]
        hst[...] = h0_ref[...]

    cbuf[8:8 + C, :] = xbc_ref[...]

    @pl.when(c == nc - 1)
    def _():
        csn_ref[...] = cbuf[C + 5:C + 8, :]

    dt = _softplus(dt_ref[...] + dtb_ref[...])
    a = -jnp.exp(alog_ref[...]) * LOG2E
    row = lax.broadcasted_iota(I32, (C, C), 0)
    col = lax.broadcasted_iota(I32, (C, C), 1)
    tril = jnp.where(row >= col, 1.0, 0.0).astype(BF16)
    b = _dot_exact_rhs(tril, dt * a)
    eb = jnp.exp2(b)
    b_last = b[C - 1:C, :]
    dl = dt * jnp.exp2(b_last - b)
    stack = jnp.concatenate([dt, eb, dl], axis=0)
    s_hi = stack.astype(BF16)
    s_lo = (stack - s_hi.astype(F32)).astype(BF16)
    lhs2 = jnp.concatenate([s_hi, s_lo], axis=1)
    bst = jnp.concatenate([b] + [pltpu.roll(b, 128 - k, axis=1) for k in range(1, hp)], axis=0)
    bt = bst.T
    rdec = jnp.broadcast_to(jnp.exp2(bt[:, C - 1:C]), (128, 128))
    r_hi = rdec.astype(BF16)
    r_lo = (rdec - r_hi.astype(F32)).astype(BF16)
    r2 = jnp.concatenate([r_hi, r_lo], axis=0)

    lane = lax.broadcasted_iota(I32, (C, 128), 1)
    trow = lax.broadcasted_iota(I32, (C, 128), 0)
    causal = trow >= (lane & (C - 1))
    lane_seg = lane >> (C.bit_length() - 1)
    xlane_seg = lax.broadcasted_iota(I32, (C, hp * SSD_P), 1) >> (SSD_P.bit_length() - 1)

    def conv(cols):
        v = cbuf[0:C + 8, cols]
        acc = cb_ref[:, cols] + v[8:8 + C] * cw_ref[SSD_K - 1:SSD_K, cols]
        for k in range(SSD_K - 1):
            acc = acc + pltpu.roll(v, C + 3 - k, axis=0)[0:C] * cw_ref[k:k + 1, cols]
        return _silu(acc)

    for g in range(SSD_G):
        xc = slice(g * SSD_GW, (g + 1) * SSD_GW)
        xg = conv(xc)
        bg = conv(slice(SSD_DI + g * SSD_N, SSD_DI + (g + 1) * SSD_N)).astype(BF16)
        cg = conv(slice(SSD_DI + SSD_G * SSD_N + g * SSD_N, SSD_DI + SSD_G * SSD_N + (g + 1) * SSD_N)).astype(BF16)
        ex = jnp.dot(lhs2, e2_ref[:, xc], preferred_element_type=F32)
        dt_e, eb_e, dl_e = ex[0:C], ex[C:2 * C], ex[2 * C:3 * C]
        xdt = xg * dt_e
        cb2 = lax.dot_general(cg, jnp.concatenate([bg] * hp, axis=0), NT_DIMS, preferred_element_type=F32)
        hg = hst[g * SSD_GW:(g + 1) * SSD_GW, :]
        y_inter = lax.dot_general(cg, hg.astype(BF16), NT_DIMS, preferred_element_type=F32)
        ys = []
        for p in range(8 // hp):
            h0 = g * 8 + p * hp
            bcol = jnp.broadcast_to(b[:, h0:h0 + 1], (C, 128))
            for k in range(1, hp):
                bcol = jnp.where(lane_seg >= k, b[:, h0 + k:h0 + k + 1], bcol)
            lm = jnp.exp2(jnp.where(causal, bcol - bt[h0:h0 + 1, :], neg))
            m2 = (cb2 * lm).astype(BF16)
            xp = xdt[:, p * hp * SSD_P:(p + 1) * hp * SSD_P]
            rhs = jnp.concatenate([jnp.where(xlane_seg == k, xp, 0.0) for k in range(hp)], axis=0).astype(BF16)
            ys.append(jnp.dot(m2, rhs, preferred_element_type=F32))
        y = jnp.concatenate(ys, axis=1) + y_inter * eb_e + xg * dsk_ref[:, xc]
        y = y * _silu(z_ref[:, xc])
        ms = jnp.mean(y * y, axis=-1, keepdims=True)
        y_ref[:, xc] = (y * lax.rsqrt(ms + EPS) * nw_ref[:, xc]).astype(y_ref.dtype)
        upd = lax.dot_general((xg * dl_e).astype(BF16), bg, TN_DIMS, preferred_element_type=F32)
        dec = jnp.dot(et2_ref[g * SSD_GW:(g + 1) * SSD_GW, :], r2, preferred_element_type=F32)
        hst[g * SSD_GW:(g + 1) * SSD_GW, :] = hg * dec + upd

    cbuf[5:8, :] = cbuf[C + 5:C + 8, :]

    @pl.when(c == nc - 1)
    def _():
        hn_ref[...] = hst[...]


def _ssd_scan(z, xbc, dtr, conv_state, h0, consts, *, nb, seq, C, row0):
    nc = seq // C
    rb0 = row0 // C
    cw, cb, dtb, alog, dsk, nw, e2, et2 = consts
    rows = lambda b, c: (rb0 + b * nc + c, 0)
    fixed = lambda b, c: (0, 0)
    per_b = lambda b, c: (b, 0, 0)
    return pl.pallas_call(
        functools.partial(_ssd_kernel, C=C, nc=nc),
        grid=(nb, nc),
        in_specs=[
            pl.BlockSpec((C, SSD_DI), rows),
            pl.BlockSpec((C, SSD_XBC), rows),
            pl.BlockSpec((C, 128), rows),
            pl.BlockSpec((None, SSD_K - 1, SSD_XBC), per_b),
            pl.BlockSpec((None, SSD_DI, SSD_N), per_b),
            pl.BlockSpec((SSD_K, SSD_XBC), fixed),
            pl.BlockSpec((1, SSD_XBC), fixed),
            pl.BlockSpec((1, 128), fixed),
            pl.BlockSpec((1, 128), fixed),
            pl.BlockSpec((1, SSD_DI), fixed),
            pl.BlockSpec((1, SSD_DI), fixed),
            pl.BlockSpec((256, SSD_DI), fixed),
            pl.BlockSpec((SSD_DI, 256), fixed),
        ],
        out_specs=[
            pl.BlockSpec((C, SSD_DI), lambda b, c: (b * nc + c, 0)),
            pl.BlockSpec((None, SSD_K - 1, SSD_XBC), per_b),
            pl.BlockSpec((None, SSD_DI, SSD_N), per_b),
        ],
        out_shape=[
            jax.ShapeDtypeStruct((nb * seq, SSD_DI), BF16),
            jax.ShapeDtypeStruct((nb, SSD_K - 1, SSD_XBC), F32),
            jax.ShapeDtypeStruct((nb, SSD_DI, SSD_N), F32),
        ],
        scratch_shapes=[pltpu.VMEM((8 + C, SSD_XBC), F32), pltpu.VMEM((SSD_DI, SSD_N), F32)],
        compiler_params=_cparams(2),
        name=f"ssd_scan_c{C}",
    )(z, xbc, dtr, conv_state, h0, cw, cb, dtb, alog, dsk, nw, e2, et2)


def _gla_kernel(q_ref, f_ref, v_ref, g_ref, s0_ref, lb_ref, nw_ref, o_ref, sn_ref, st, b_s, c_s, *, T, nc):
    c = pl.program_id(1)
    neg = jnp.float32(-jnp.inf)

    @pl.when(c == 0)
    def _():
        for h in range(HG_H):
            st[h * 128:(h + 1) * 128, :] = s0_ref[h * 128:(h + 1) * 128, :].T

    row = lax.broadcasted_iota(I32, (T, T), 0)
    col = lax.broadcasted_iota(I32, (T, T), 1)
    sh = HG_BLK.bit_length() - 1
    same_blk = (row >> sh) == (col >> sh)
    bd = jnp.where(jnp.logical_and(same_blk, row >= col), 1.0, 0.0).astype(BF16)

    for h in range(HG_H):
        hl = slice(h * 128, (h + 1) * 128)
        fz = f_ref[:, hl]
        lb = lb_ref[:, hl]
        ls = jnp.minimum(fz, 0.0) - _log1p_unit(jnp.exp(-jnp.abs(fz)))
        a1 = jnp.log(lb)
        a2 = jnp.log1p(-lb) + ls
        lf = jnp.maximum(a1, a2) + _log1p_unit(jnp.exp(-jnp.abs(a1 - a2)))
        b2 = _dot_exact_rhs(bd, lf * LOG2E)
        b_s[:, hl] = b2
        c_s[:, hl] = b2 - (a2 - fz) * LOG2E

    trow = lax.broadcasted_iota(I32, (8, 128), 0)
    ones = jnp.ones((128, 128), BF16)
    hb = HG_BLK // 2

    def blk(j, carry):
        rows = pl.ds(pl.multiple_of(j * HG_BLK, HG_BLK), HG_BLK)
        for h in range(HG_H):
            hl = slice(h * 128, (h + 1) * 128)
            qb = q_ref[rows, hl]
            bb = b_s[rows, hl]
            cb = c_s[rows, hl]
            vb = v_ref[rows, hl]
            sth = st[hl, :]
            o = lax.dot_general((qb * jnp.exp2(bb)).astype(BF16), sth.astype(BF16), NT_DIMS,
                                preferred_element_type=F32)
            q_h, b_h = (qb[0:hb], qb[hb:]), (bb[0:hb], bb[hb:])
            pieces, owner = [], []
            for s in range(HG_BLK):
                cs = cb[s:s + 1, :]
                for half in range(2):
                    if s >= hb * (half + 1):
                        continue
                    e = b_h[half] - cs
                    if s >= hb * half:
                        e = jnp.where(trow >= s - hb * half, e, neg)
                    pieces.append(q_h[half] * jnp.exp2(e))
                    owner.append((s, half))
            att = jnp.dot(jnp.concatenate(pieces, axis=0).astype(BF16), ones, preferred_element_type=F32)
            o_h = [o[0:hb], o[hb:]]
            for n, (s, half) in enumerate(owner):
                o_h[half] = o_h[half] + att[n * hb:(n + 1) * hb] * vb[s:s + 1, :]
            o = jnp.concatenate(o_h, axis=0)
            b_end = bb[HG_BLK - 1:HG_BLK, :]
            khat = jnp.exp2(b_end - cb).astype(BF16)
            upd = lax.dot_general(vb.astype(BF16), khat, TN_DIMS, preferred_element_type=F32)
            st[hl, :] = sth * jnp.exp2(b_end) + upd
            ms = jnp.mean(o * o, axis=-1, keepdims=True)
            o_ref[rows, hl] = (o * lax.rsqrt(ms + EPS) * nw_ref[...] * _silu(g_ref[rows, hl])).astype(o_ref.dtype)
        return carry

    lax.fori_loop(0, T // HG_BLK, blk, 0)

    @pl.when(c == nc - 1)
    def _():
        for h in range(HG_H):
            sn_ref[h * 128:(h + 1) * 128, :] = st[h * 128:(h + 1) * 128, :].T


def _gla_scan(qfig, s0, lb, nw, *, nb, seq, T, row0):
    nc = seq // T
    rb0 = row0 // T
    per_b = lambda b, c: (b, 0, 0)

    def cols(k):
        return pl.BlockSpec((T, D), lambda b, c: (rb0 + b * nc + c, k))

    return pl.pallas_call(
        functools.partial(_gla_kernel, T=T, nc=nc),
        grid=(nb, nc),
        in_specs=[
            cols(0), cols(1), cols(2), cols(3),
            pl.BlockSpec((None, D, HG_DV), per_b),
            pl.BlockSpec((1, D), lambda b, c: (0, 0)),
            pl.BlockSpec((1, HG_DV), lambda b, c: (0, 0)),
        ],
        out_specs=[
            pl.BlockSpec((T, D), lambda b, c: (b * nc + c, 0)),
            pl.BlockSpec((None, D, HG_DV), per_b),
        ],
        out_shape=[
            jax.ShapeDtypeStruct((nb * seq, D), BF16),
            jax.ShapeDtypeStruct((nb, D, HG_DV), F32),
        ],
        scratch_shapes=[pltpu.VMEM((D, HG_DK), F32), pltpu.VMEM((T, D), F32), pltpu.VMEM((T, D), F32)],
        compiler_params=_cparams(2),
        name=f"gla_scan_t{T}",
    )(qfig, qfig, qfig, qfig, s0, lb, nw)


def _route_plan(top_i):
    e_flat = top_i.reshape(-1)
    onehot = (e_flat[:, None] == jnp.arange(NE, dtype=I32)[None, :]).astype(I32)
    csum = jnp.cumsum(onehot, axis=0)
    rank = jnp.sum(onehot * (csum - 1), axis=1)
    cnt = csum[-1]
    ntile = (cnt + MOE_SORT - 1) // MOE_SORT
    cum_t = jnp.cumsum(ntile)
    tile0 = cum_t - ntile
    pos = jnp.sum(onehot * tile0[None, :], axis=1) * MOE_SORT + rank
    n_used = cum_t[-1]
    t_ids = jnp.arange(MOE_TILES, dtype=I32)
    te = jnp.sum((t_ids[:, None] >= cum_t[None, :]).astype(I32), axis=1)
    te_last = jnp.sum((n_used - 1 >= cum_t).astype(I32))
    te = jnp.where(t_ids < n_used, te, te_last).astype(I32)
    prev = jnp.concatenate([jnp.full((1,), -1, I32), te[:-1]])
    first = jnp.logical_and(t_ids < n_used, te != prev).astype(I32)
    e_ids = jnp.arange(NE, dtype=I32)
    later = jnp.logical_and(e_ids[None, :] > e_ids[:, None], ntile[None, :] > 0)
    nxt = jnp.min(jnp.where(later, e_ids[None, :], NE), axis=1)
    nxt = jnp.where(nxt == NE, -1, nxt).astype(I32)
    misc = jnp.stack([n_used, te[0]]).astype(I32)
    tok = jnp.arange(2 * M, dtype=I32) // 2
    src = jnp.zeros((MOE_ROWS,), I32).at[pos].set(tok)
    n_steps = ((n_used + 1) // 2).reshape(1).astype(I32)
    return (te, first, nxt, misc), n_steps, src, pos.reshape(M, 2)


def kernel(x_prompt, x_sample, c_prompt, c_sample, state_ssd_conv, state_ssd, state_hgrn, ada_w, ada_b, norm_w,
           ssd_w_in, ssd_conv_w, ssd_conv_b, ssd_dt_bias, ssd_a_log, ssd_d, ssd_norm_w, ssd_w_out, hgrn_w_in,
           hgrn_lb_logits, hgrn_norm_w, hgrn_w_out, ffn_w_gu, ffn_w_down, moe_router, moe_w_gu, moe_w_down,
           final_norm_w):
    x = (x_prompt.reshape(MP, D), x_sample.reshape(MS, D))
    c_all = jnp.concatenate([c_prompt, jnp.zeros((MOD_S0 - NP, D), F32), c_sample], axis=0)
    norm_w4 = norm_w.reshape(4, 1, D)

    mods = [
        _mm_plain(c_all, ada_w, layer=l, col0=0, n_out=6 * D, tn=1024, tm=MOD_ROWS,
                  bias=ada_b[l].reshape(1, 6 * D), act_lhs=True, name="adaln")
        for l in range(2)
    ]

    hn = _prenorm(x, norm_w4, 0, mods[0], 0, 1, BF16)
    w_in_t = jnp.swapaxes(ssd_w_in, 1, 2)
    z = _mm_plain(hn, w_in_t, layer=0, col0=0, n_out=SSD_DI, tn=1024, tm=1024, w_t=True, name="ssd_in_z")
    xbc = _mm_plain(hn, w_in_t, layer=0, col0=SSD_DI, n_out=SSD_XBC, tn=1024, tm=1024, w_t=True,
                    name="ssd_in_xbc")
    dtr = _mm_plain(hn, w_in_t, layer=0, col0=SSD_DI + SSD_XBC, n_out=128, tn=128, tm=1024, w_t=True,
                    name="ssd_in_dt")

    pad64 = lambda v: jnp.pad(v.reshape(1, SSD_HEADS), ((0, 0), (0, 128 - SSD_HEADS)))
    head_of = jnp.arange(SSD_DI, dtype=I32) // SSD_P
    e1 = (jnp.arange(128, dtype=I32)[:, None] == head_of[None, :]).astype(BF16)
    e2 = jnp.concatenate([e1, e1], axis=0)
    et2 = jnp.concatenate([e1.T, e1.T], axis=1)
    consts = (ssd_conv_w[0], ssd_conv_b[0].reshape(1, SSD_XBC), pad64(ssd_dt_bias[0]), pad64(ssd_a_log[0]),
              jnp.repeat(ssd_d[0], SSD_P).reshape(1, SSD_DI), ssd_norm_w[0].reshape(1, SSD_DI), e2, et2)
    yp, p_conv, p_ssd = _ssd_scan(z, xbc, dtr, jnp.zeros((NP, SSD_K - 1, SSD_XBC), F32),
                                  jnp.zeros((NP, SSD_DI, SSD_N), F32), consts, nb=NP, seq=LP, C=64, row0=0)
    ysm, s_conv, s_ssd = _ssd_scan(z, xbc, dtr, state_ssd_conv[0], state_ssd[0].reshape(NS, SSD_DI, SSD_N), consts,
                                   nb=NS, seq=LS, C=32, row0=MP)
    x = _mm_resid((yp, ysm), ssd_w_out, x, mods[0], 2, k=SSD_DI, tn=1024, tm=512, name="ssd_out")

    hn = _prenorm(x, norm_w4, 1, mods[0], 3, 4, BF16)
    act = _mm_swiglu(hn, ffn_w_gu, FFN, tn=512, tm=1024, name="ffn_gu")
    x = _mm_resid(act, ffn_w_down, x, mods[0], 5, k=FFN, tn=1024, tm=512, name="ffn_down")

    hn = _prenorm(x, norm_w4, 2, mods[1], 0, 1, BF16)
    qfig = _mm_plain(hn, hgrn_w_in, layer=0, col0=0, n_out=4 * D, tn=1024, tm=1024, name="hgrn_in")
    p = jax.nn.softmax(hgrn_lb_logits.astype(F32), axis=0)
    lb = (jnp.cumsum(p, axis=0) - p[0])[1].reshape(1, D)
    nw_h = hgrn_norm_w[0].reshape(1, HG_DV)
    op, p_hgrn = _gla_scan(qfig, jnp.zeros((NP, D, HG_DV), F32), lb, nw_h, nb=NP, seq=LP, T=64, row0=0)
    osm, s_hgrn = _gla_scan(qfig, state_hgrn[0].reshape(NS, D, HG_DV), lb, nw_h, nb=NS, seq=LS, T=32, row0=MP)
    x = _mm_resid((op, osm), hgrn_w_out, x, mods[1], 2, k=D, tn=1024, tm=512, name="hgrn_out")

    hn32, gv, gi = _prenorm(x, norm_w4, 3, mods[1], 3, 4, F32,
                            router_pad=jnp.pad(moe_router[0], ((0, 0), (0, 128 - NE))))
    plan, n_steps, src, pos = _route_plan(gi[:, :2])
    xs = _gather_rows(n_steps, src.reshape(MOE_STEPS, 1, 2 * MOE_SORT), hn32)
    act = _moe_gu(plan, xs, moe_w_gu[0])
    ys = _moe_down(plan, act, moe_w_down[0])
    tmc = 256
    pos3 = jnp.concatenate([pos[:, 0].reshape(M // tmc, 1, tmc), pos[:, 1].reshape(M // tmc, 1, tmc)], axis=2)
    yo_p, yo_s = _combine(pos3, ys, x, gv, mods[1], 5, final_norm_w.reshape(1, D), tm=tmc)

    return (
        yo_p.reshape(NP, LP, D),
        yo_s.reshape(NS, LS, D),
        p_conv[None],
        p_ssd.reshape(1, NP, SSD_HEADS, SSD_P, SSD_N),
        p_hgrn.reshape(1, NP, HG_H, HG_DK, HG_DV),
        s_conv[None],
        s_ssd.reshape(1, NS, SSD_HEADS, SSD_P, SSD_N),
        s_hgrn.reshape(1, NS, HG_H, HG_DK, HG_DV),
    )
```

```python
import functools

import jax
import jax.numpy as jnp
from jax import lax
from jax.experimental import pallas as pl
from jax.experimental.pallas import tpu as pltpu

F32 = jnp.float32
BF16 = jnp.bfloat16
I32 = jnp.int32

EPS = 1e-6
LOG2E = 1.4426950408889634
D = 2048
NP, LP = 4, 2048
NS, LS = 32, 32
MP, MS = NP * LP, NS * LS
M = MP + MS
MOD_S0 = 8
MOD_ROWS = MOD_S0 + NS

SSD_DI = 4096
SSD_HEADS = 64
SSD_P = 64
SSD_G = 8
SSD_GW = SSD_DI // SSD_G
SSD_N = 128
SSD_XBC = SSD_DI + 2 * SSD_G * SSD_N
SSD_K = 4

HG_H = 16
HG_DK = 128
HG_DV = 128
HG_BLK = 16

FFN = 5632
NE = 8
EH = 7168
MOE_SORT = 256
MOE_TILES = (2 * M) // MOE_SORT + NE
MOE_STEPS = MOE_TILES // 2
MOE_ROWS = MOE_TILES * MOE_SORT
CAST_ROWS = 64

VMEM_LIMIT = 60 * 1024 * 1024

NT_DIMS = (((1,), (1,)), ((), ()))
TN_DIMS = (((0,), (0,)), ((), ()))


def _cparams(n_axes):
    return pltpu.CompilerParams(dimension_semantics=("arbitrary",) * n_axes,
                                vmem_limit_bytes=VMEM_LIMIT)


def _silu(x):
    h = 0.5 * x
    return h + h * jnp.tanh(h)


def _softplus(x):
    return jnp.maximum(x, 0.0) + jnp.log1p(jnp.exp(-jnp.abs(x)))


def _log1p_unit(u):
    return jnp.log(1.0 + u)


def _split3(x):
    hi = x.astype(BF16)
    r1 = x - hi.astype(F32)
    mid = r1.astype(BF16)
    lo = (r1 - mid.astype(F32)).astype(BF16)
    return hi, mid, lo


def _dot_exact_rhs(mat_bf16, x):
    hi, mid, lo = _split3(x)
    acc = jnp.dot(mat_bf16, hi, preferred_element_type=F32)
    acc = acc + jnp.dot(mat_bf16, mid, preferred_element_type=F32)
    return acc + jnp.dot(mat_bf16, lo, preferred_element_type=F32)


def _grouped_apply(i, tm, mod_refs, fn):
    n_pt = MP // tm
    tpb = LP // tm

    @pl.when(i < n_pt)
    def _():
        r = i // tpb
        fn([m[pl.ds(r, 1), :][None] for m in mod_refs], 1, 0)

    @pl.when(i >= n_pt)
    def _():
        g = tm // LS
        start = pl.multiple_of(MOD_S0 + (i - n_pt) * g, 8)
        fn([m[pl.ds(start, g), :][:, None, :] for m in mod_refs], g, 1)


def _row_operand(op, tm, width, ij):
    if not isinstance(op, tuple):
        return [pl.BlockSpec((tm, width), lambda *g: ij(*g))], [op]
    n_pt = MP // tm
    return [
        pl.BlockSpec((tm, width), lambda *g: (jnp.minimum(ij(*g)[0], n_pt - 1), ij(*g)[1])),
        pl.BlockSpec((tm, width), lambda *g: (jnp.maximum(ij(*g)[0] - n_pt, 0), ij(*g)[1])),
    ], list(op)


def _top2_gates(logits):
    lane = lax.broadcasted_iota(I32, logits.shape, 1)
    neg = jnp.float32(-jnp.inf)
    l1 = jnp.where(lane < NE, logits, neg)
    m1 = jnp.max(l1, axis=-1, keepdims=True)
    i1 = jnp.min(jnp.where(l1 == m1, lane, 128), axis=-1, keepdims=True)
    l2 = jnp.where(lane == i1, neg, l1)
    m2 = jnp.max(l2, axis=-1, keepdims=True)
    i2 = jnp.min(jnp.where(l2 == m2, lane, 128), axis=-1, keepdims=True)
    e = jnp.exp(m2 - m1)
    g1 = 1.0 / (1.0 + e)
    g2 = e * g1
    gv = jnp.where(lane == 0, g1, jnp.where(lane == 1, g2, 0.0))
    gi = jnp.where(lane == 0, i1, jnp.where(lane == 1, i2, 0))
    return gv, gi


def _prenorm_kernel(*refs, tm, nx, route):
    x_refs = refs[:nx]
    if route:
        nw_ref, sc_ref, sh_ref, r_ref, o_ref, gv_ref, gi_ref = refs[nx:]
    else:
        nw_ref, sc_ref, sh_ref, o_ref = refs[nx:]
    i = pl.program_id(0)

    def fn(mods, g, s):
        sc, sh = mods
        x = x_refs[s % nx][...]
        ms = jnp.mean(x * x, axis=-1, keepdims=True)
        y = x * lax.rsqrt(ms + EPS) * nw_ref[...]
        y = (y.reshape(g, tm // g, D) * (1.0 + sc) + sh).reshape(tm, D)
        o_ref[...] = y.astype(o_ref.dtype)
        if route:
            r = r_ref[...]
            y_hi, r_hi = y.astype(BF16), r.astype(BF16)
            y_lo = (y - y_hi.astype(F32)).astype(BF16)
            r_lo = (r - r_hi.astype(F32)).astype(BF16)
            logits = (jnp.dot(y_hi, r_hi, preferred_element_type=F32)
                      + jnp.dot(y_lo, r_hi, preferred_element_type=F32)
                      + jnp.dot(y_hi, r_lo, preferred_element_type=F32))
            gv_ref[...], gi_ref[...] = _top2_gates(logits)

    _grouped_apply(i, tm, [sc_ref, sh_ref], fn)


def _prenorm(x, norm_w4, k, mod, sh_blk, sc_blk, out_dtype, tm=512, router_pad=None):
    x_specs, x_args = _row_operand(x, tm, D, lambda i: (i, 0))
    route = router_pad is not None
    row_out = lambda w: pl.BlockSpec((tm, w), lambda i: (i, 0))
    out_specs = [row_out(D)]
    out_shape = [jax.ShapeDtypeStruct((M, D), out_dtype)]
    extra_specs, extra_args = [], []
    if route:
        extra_specs = [pl.BlockSpec((D, 128), lambda i: (0, 0))]
        extra_args = [router_pad]
        out_specs += [row_out(128), row_out(128)]
        out_shape += [jax.ShapeDtypeStruct((M, 128), F32), jax.ShapeDtypeStruct((M, 128), I32)]
    res = pl.pallas_call(
        functools.partial(_prenorm_kernel, tm=tm, nx=len(x_args), route=route),
        grid=(M // tm,),
        in_specs=x_specs + [
            pl.BlockSpec((None, 1, D), lambda i: (k, 0, 0)),
            pl.BlockSpec((MOD_ROWS, D), lambda i: (0, sc_blk)),
            pl.BlockSpec((MOD_ROWS, D), lambda i: (0, sh_blk)),
        ] + extra_specs,
        out_specs=out_specs,
        out_shape=out_shape,
        compiler_params=_cparams(1),
        name="prenorm_route" if route else "prenorm",
    )(*x_args, norm_w4, mod, mod, *extra_args)
    return res if route else res[0]


def _mm_plain_kernel(*refs, act_lhs, has_bias, valid_cols, w_t):
    if has_bias:
        lhs_ref, w_ref, b_ref, o_ref, wb = refs
    else:
        lhs_ref, w_ref, o_ref, wb = refs

    @pl.when(pl.program_id(1) == 0)
    def _():
        w = w_ref[...]
        if valid_cols is not None:
            w = jnp.where(lax.broadcasted_iota(I32, w.shape, 0 if w_t else 1) < valid_cols, w, 0.0)
        wb[...] = w.astype(BF16)

    lhs = lhs_ref[...]
    if act_lhs:
        lhs = _silu(lhs).astype(BF16)
    if w_t:
        acc = lax.dot_general(lhs, wb[...], NT_DIMS, preferred_element_type=F32)
    else:
        acc = jnp.dot(lhs, wb[...], preferred_element_type=F32)
    if has_bias:
        acc = acc + b_ref[...]
    o_ref[...] = acc.astype(o_ref.dtype)


def _mm_plain(lhs, w3, *, layer, col0, n_out, tn, tm, out_dtype=F32, bias=None, act_lhs=False, w_t=False,
              name="mm"):
    rows, k = lhs.shape
    off = col0 // tn
    w_cols = w3.shape[1 if w_t else 2] - col0
    valid_cols = w_cols if w_cols < n_out else None
    if w_t:
        w_spec = pl.BlockSpec((None, tn, k), lambda j, i: (layer, j + off, 0))
    else:
        w_spec = pl.BlockSpec((None, k, tn), lambda j, i: (layer, 0, j + off))
    in_specs = [pl.BlockSpec((tm, k), lambda j, i: (i, 0)), w_spec]
    args = [lhs, w3]
    if bias is not None:
        in_specs.append(pl.BlockSpec((1, tn), lambda j, i: (0, j)))
        args.append(bias)
    return pl.pallas_call(
        functools.partial(_mm_plain_kernel, act_lhs=act_lhs, has_bias=bias is not None, valid_cols=valid_cols,
                          w_t=w_t),
        grid=(n_out // tn, rows // tm),
        in_specs=in_specs,
        out_specs=pl.BlockSpec((tm, tn), lambda j, i: (i, j)),
        out_shape=jax.ShapeDtypeStruct((rows, n_out), out_dtype),
        scratch_shapes=[pltpu.VMEM((tn, k) if w_t else (k, tn), BF16)],
        compiler_params=_cparams(2),
        name=name,
    )(*args)


def _mm_resid_kernel(*refs, tm, tn, nl, nr):
    lhs_refs = refs[:nl]
    w_ref = refs[nl]
    res_refs = refs[nl + 1:nl + 1 + nr]
    gate_ref, o_ref, wb = refs[nl + 1 + nr:]
    i = pl.program_id(1)

    @pl.when(i == 0)
    def _():
        wb[...] = w_ref[...].astype(BF16)

    def fn(mods, g, s):
        (gate,) = mods
        acc = jnp.dot(lhs_refs[s % nl][...], wb[...], preferred_element_type=F32)
        out = res_refs[s % nr][...].reshape(g, tm // g, tn) + gate * acc.reshape(g, tm // g, tn)
        o_ref[...] = out.reshape(tm, tn)

    _grouped_apply(i, tm, [gate_ref], fn)


def _mm_resid(lhs, w3, res, mod, gate_blk, *, k, tn, tm, name):
    goff = gate_blk * (D // tn)
    l_specs, l_args = _row_operand(lhs, tm, k, lambda j, i: (i, 0))
    r_specs, r_args = _row_operand(res, tm, tn, lambda j, i: (i, j))
    return pl.pallas_call(
        functools.partial(_mm_resid_kernel, tm=tm, tn=tn, nl=len(l_args), nr=len(r_args)),
        grid=(D // tn, M // tm),
        in_specs=l_specs + [pl.BlockSpec((None, k, tn), lambda j, i: (0, 0, j),
                                         pipeline_mode=pl.Buffered(1))] + r_specs + [
            pl.BlockSpec((MOD_ROWS, tn), lambda j, i: (0, goff + j)),
        ],
        out_specs=pl.BlockSpec((tm, tn), lambda j, i: (i, j)),
        out_shape=jax.ShapeDtypeStruct((M, D), F32),
        scratch_shapes=[pltpu.VMEM((k, tn), BF16)],
        compiler_params=_cparams(2),
        name=name,
    )(*l_args, w3, *r_args, mod)


def _mm_swiglu_kernel(lhs_ref, wg_ref, wu_ref, o_ref, wgb, wub):
    @pl.when(pl.program_id(1) == 0)
    def _():
        wgb[...] = wg_ref[...].astype(BF16)
        wub[...] = wu_ref[...].astype(BF16)

    lhs = lhs_ref[...]
    gt = jnp.dot(lhs, wgb[...], preferred_element_type=F32)
    up = jnp.dot(lhs, wub[...], preferred_element_type=F32)
    o_ref[...] = (_silu(gt) * up).astype(o_ref.dtype)


def _mm_swiglu(lhs, w3, hidden, *, tn, tm, name):
    rows, k = lhs.shape
    nb = hidden // tn
    return pl.pallas_call(
        _mm_swiglu_kernel,
        grid=(nb, rows // tm),
        in_specs=[
            pl.BlockSpec((tm, k), lambda j, i: (i, 0)),
            pl.BlockSpec((None, k, tn), lambda j, i: (0, 0, j)),
            pl.BlockSpec((None, k, tn), lambda j, i: (0, 0, j + nb)),
        ],
        out_specs=pl.BlockSpec((tm, tn), lambda j, i: (i, j)),
        out_shape=jax.ShapeDtypeStruct((rows, hidden), BF16),
        scratch_shapes=[pltpu.VMEM((k, tn), BF16), pltpu.VMEM((k, tn), BF16)],
        compiler_params=_cparams(2),
        name=name,
    )(lhs, w3, w3)


def _expert_matmul_kernel(*refs, tn, nj, col_offs, epilogue, k0, has_prev):
    te_ref, first_ref, nxt_ref, misc_ref, lhs_ref, w_hbm = refs[:6]
    prev_ref = refs[6] if has_prev else None
    o_ref, stage, panel, sem = refs[6 + has_prev:]
    j = pl.program_id(0)
    i = pl.program_id(1)
    nu = misc_ref[0]
    first_e = misc_ref[1]
    half = MOE_SORT
    nw = len(col_offs)
    kk = stage.shape[1]

    def weight_copies(e, jj):
        c0 = pl.multiple_of(jj * tn, 128)
        return [pltpu.make_async_copy(w_hbm.at[e, pl.ds(k0, kk), pl.ds(off + c0, tn)], stage.at[n], sem.at[n])
                for n, off in enumerate(col_offs)]

    def change(t, very_first):
        e_new = te_ref[t]

        @pl.when(very_first)
        def _():
            for cp in weight_copies(e_new, j):
                cp.start()

        for cp in weight_copies(e_new, j):
            cp.wait()

        def cast_rows(c, carry):
            r = pl.multiple_of(c * CAST_ROWS, CAST_ROWS)
            for n in range(nw):
                panel[n, pl.ds(r, CAST_ROWS), :] = stage[n, pl.ds(r, CAST_ROWS), :].astype(BF16)
            return carry

        lax.fori_loop(0, stage.shape[1] // CAST_ROWS, cast_rows, 0)
        ne = nxt_ref[e_new]
        same_col = ne >= 0

        @pl.when(jnp.logical_or(same_col, j + 1 < nj))
        def _():
            for cp in weight_copies(jnp.where(same_col, ne, first_e), jnp.where(same_col, j, j + 1)):
                cp.start()

    def rows_dot(r0, rows):
        lhs = lhs_ref[r0:r0 + rows, :]
        accs = [jnp.dot(lhs, panel[n], preferred_element_type=F32) for n in range(nw)]
        res = epilogue(*accs)
        if has_prev:
            res = prev_ref[r0:r0 + rows, :] + res
        o_ref[r0:r0 + rows, :] = res.astype(o_ref.dtype)

    t0 = 2 * i
    t1 = t0 + 1
    v0 = t0 < nu
    v1 = t1 < nu
    f0 = first_ref[t0] == 1
    f1 = first_ref[t1] == 1

    @pl.when(f0)
    def _():
        change(t0, jnp.logical_and(j == 0, i == 0))

    @pl.when(jnp.logical_and(v1, jnp.logical_not(f1)))
    def _():
        rows_dot(0, 2 * half)

    @pl.when(f1)
    def _():
        rows_dot(0, half)
        change(t1, False)
        rows_dot(half, half)

    @pl.when(jnp.logical_and(v0, jnp.logical_not(v1)))
    def _():
        rows_dot(0, half)
        o_ref[half:, :] = jnp.zeros((half, tn), o_ref.dtype)

    @pl.when(jnp.logical_not(v0))
    def _():
        o_ref[...] = jnp.zeros_like(o_ref)


def _expert_matmul(plan, lhs, w, *, k, n_out, tn, col_offs, epilogue, out_dtype, name, k0=0, prev=None):
    te, first, nxt, misc = plan
    nj = n_out // tn
    nw = len(col_offs)
    kb = k0 // k
    row_tile = lambda w_: pl.BlockSpec((2 * MOE_SORT, w_), lambda j, i, *_: (i, j))
    in_specs = [
        pl.BlockSpec((2 * MOE_SORT, k), lambda j, i, *_: (i, kb)),
        pl.BlockSpec(memory_space=pl.ANY),
    ]
    args = [lhs, w]
    if prev is not None:
        in_specs.append(row_tile(tn))
        args.append(prev)
    grid_spec = pltpu.PrefetchScalarGridSpec(
        num_scalar_prefetch=4,
        grid=(nj, MOE_STEPS),
        in_specs=in_specs,
        out_specs=row_tile(tn),
        scratch_shapes=[pltpu.VMEM((nw, k, tn), F32), pltpu.VMEM((nw, k, tn), BF16),
                        pltpu.SemaphoreType.DMA((nw,))],
    )
    return pl.pallas_call(
        functools.partial(_expert_matmul_kernel, tn=tn, nj=nj, col_offs=col_offs, epilogue=epilogue, k0=k0,
                          has_prev=prev is not None),
        grid_spec=grid_spec,
        out_shape=jax.ShapeDtypeStruct((MOE_ROWS, n_out), out_dtype),
        compiler_params=_cparams(2),
        name=name,
    )(te, first, nxt, misc, *args)


def _moe_gu(plan, xs, w_gu):
    return _expert_matmul(plan, xs, w_gu, k=D, n_out=EH, tn=1024, col_offs=(0, EH),
                          epilogue=lambda gt, up: _silu(gt) * up, out_dtype=BF16, name="moe_gu")


def _moe_down(plan, act, w_down):
    kh = EH // 2
    common = dict(k=kh, n_out=D, tn=1024, col_offs=(0,), epilogue=lambda acc: acc, out_dtype=F32)
    part = _expert_matmul(plan, act, w_down, k0=0, name="moe_down_a", **common)
    return _expert_matmul(plan, act, w_down, k0=kh, prev=part, name="moe_down_b", **common)


def _row_copy(src_hbm, row, buf, r, sem):
    return pltpu.make_async_copy(src_hbm.at[pl.ds(row, 1), :], buf.at[pl.ds(r, 1), :], sem)


def _start_rows(src_hbm, idx_ref, off, dst, sem, n):
    def body(p, carry):
        for q in range(2):
            r = 2 * p + q
            _row_copy(src_hbm, idx_ref[0, 0, off + r], dst, r, sem).start(priority=q)
        return carry

    lax.fori_loop(0, n // 2, body, 0, unroll=4)


def _wait_rows(src_hbm, dst, sem, n):
    pltpu.make_async_copy(src_hbm.at[pl.ds(0, n), :], dst, sem).wait()


def _gather_kernel(nu_ref, idx_ref, idx_next_ref, src_hbm, o_ref, buf, sem, *, tm):
    t = pl.program_id(0)
    nu = nu_ref[0]
    slot = lax.rem(t, 2)

    @pl.when(t == 0)
    def _():
        _start_rows(src_hbm, idx_ref, 0, buf.at[0], sem.at[0], tm)

    @pl.when(t + 1 < nu)
    def _():
        _start_rows(src_hbm, idx_next_ref, 0, buf.at[1 - slot], sem.at[1 - slot], tm)

    @pl.when(t < nu)
    def _():
        _wait_rows(src_hbm, buf.at[slot], sem.at[slot], tm)
        o_ref[...] = buf[slot].astype(o_ref.dtype)

    @pl.when(t >= nu)
    def _():
        o_ref[...] = jnp.zeros_like(o_ref)


def _gather_rows(nu, idx3, src):
    tm = 2 * MOE_SORT
    nt = MOE_STEPS
    grid_spec = pltpu.PrefetchScalarGridSpec(
        num_scalar_prefetch=1,
        grid=(nt,),
        in_specs=[
            pl.BlockSpec((1, 1, tm), lambda t, nu: (t, 0, 0), memory_space=pltpu.SMEM),
            pl.BlockSpec((1, 1, tm), lambda t, nu: (jnp.minimum(t + 1, nt - 1), 0, 0), memory_space=pltpu.SMEM),
            pl.BlockSpec(memory_space=pl.ANY),
        ],
        out_specs=pl.BlockSpec((tm, D), lambda t, nu: (t, 0)),
        scratch_shapes=[pltpu.VMEM((2, tm, D), F32), pltpu.SemaphoreType.DMA((2,))],
    )
    return pl.pallas_call(
        functools.partial(_gather_kernel, tm=tm),
        grid_spec=grid_spec,
        out_shape=jax.ShapeDtypeStruct((MOE_ROWS, D), BF16),
        compiler_params=_cparams(1),
        name="moe_gather",
    )(nu, idx3, idx3, src)


def _combine_kernel(pos_ref, pos_next_ref, ys_hbm, x_ref, gv_ref, gate_ref, fw_ref, op_ref, os_ref, buf, sem,
                    *, tm, nt):
    t = pl.program_id(0)
    slot = lax.rem(t, 2)

    def start(idx_ref, s):
        _start_rows(ys_hbm, idx_ref, 0, buf.at[s, 0], sem.at[s, 0], tm)
        _start_rows(ys_hbm, idx_ref, tm, buf.at[s, 1], sem.at[s, 1], tm)

    @pl.when(t == 0)
    def _():
        start(pos_ref, 0)

    @pl.when(t + 1 < nt)
    def _():
        start(pos_next_ref, 1 - slot)

    _wait_rows(ys_hbm, buf.at[slot, 0], sem.at[slot, 0], tm)
    _wait_rows(ys_hbm, buf.at[slot, 1], sem.at[slot, 1], tm)

    def fn(mods, g, s):
        (gate,) = mods
        gv = gv_ref[...]
        f = gv[:, 0:1] * buf[slot, 0] + gv[:, 1:2] * buf[slot, 1]
        x = x_ref[...].reshape(g, tm // g, D) + gate * f.reshape(g, tm // g, D)
        x = x.reshape(tm, D)
        ms = jnp.mean(x * x, axis=-1, keepdims=True)
        (op_ref, os_ref)[s][...] = x * lax.rsqrt(ms + EPS) * fw_ref[...]

    _grouped_apply(t, tm, [gate_ref], fn)


def _combine(pos3, ys, x, gv, mod, gate_blk, final_w, tm=256):
    nt = M // tm
    n_pt = MP // tm
    return pl.pallas_call(
        functools.partial(_combine_kernel, tm=tm, nt=nt),
        grid=(nt,),
        in_specs=[
            pl.BlockSpec((1, 1, 2 * tm), lambda t: (t, 0, 0), memory_space=pltpu.SMEM),
            pl.BlockSpec((1, 1, 2 * tm), lambda t: (jnp.minimum(t + 1, nt - 1), 0, 0), memory_space=pltpu.SMEM),
            pl.BlockSpec(memory_space=pl.ANY),
            pl.BlockSpec((tm, D), lambda t: (t, 0)),
            pl.BlockSpec((tm, 128), lambda t: (t, 0)),
            pl.BlockSpec((MOD_ROWS, D), lambda t: (0, gate_blk)),
            pl.BlockSpec((1, D), lambda t: (0, 0)),
        ],
        out_specs=[
            pl.BlockSpec((tm, D), lambda t: (jnp.minimum(t, n_pt - 1), 0)),
            pl.BlockSpec((tm, D), lambda t: (jnp.maximum(t - n_pt, 0), 0)),
        ],
        out_shape=[jax.ShapeDtypeStruct((MP, D), F32), jax.ShapeDtypeStruct((MS, D), F32)],
        scratch_shapes=[pltpu.VMEM((2, 2, tm, D), F32), pltpu.SemaphoreType.DMA((2, 2))],
        compiler_params=_cparams(1),
        name="moe_combine",
    )(pos3, pos3, ys, x, gv, mod, final_w)


def _ssd_kernel(z_ref, xbc_ref, dt_ref, cs_ref, h0_ref, cw_ref, cb_ref, dtb_ref, alog_ref, dsk_ref, nw_ref,
                e2_ref, et2_ref, y_ref, csn_ref, hn_ref, cbuf, hst, *, C, nc):
    c = pl.program_id(1)
    hp = 128 // C
    neg = jnp.float32(-jnp.inf)

    @pl.when(c == 0)
    def _():
        cbuf[0:8, :] = jnp.zeros((8, SSD_XBC), F32)
        cbuf[5:8, :] = cs_ref[...]
        hst[...] = h0_ref[...]

    cbuf[8:8 + C, :] = xbc_ref[...]

    @pl.when(c == nc - 1)
    def _():
        csn_ref[...] = cbuf[C + 5:C + 8, :]

    dt = _softplus(dt_ref[...] + dtb_ref[...])
    a = -jnp.exp(alog_ref[...]) * LOG2E
    row = lax.broadcasted_iota(I32, (C, C), 0)
    col = lax.broadcasted_iota(I32, (C, C), 1)
    tril = jnp.where(row >= col, 1.0, 0.0).astype(BF16)
    b = _dot_exact_rhs(tril, dt * a)
    eb = jnp.exp2(b)
    b_last = b[C - 1:C, :]
    dl = dt * jnp.exp2(b_last - b)
    stack = jnp.concatenate([dt, eb, dl], axis=0)
    s_hi = stack.astype(BF16)
    s_lo = (stack - s_hi.astype(F32)).astype(BF16)
    lhs2 = jnp.concatenate([s_hi, s_lo], axis=1)
    bst = jnp.concatenate([b] + [pltpu.roll(b, 128 - k, axis=1) for k in range(1, hp)], axis=0)
    bt = bst.T
    rdec = jnp.broadcast_to(jnp.exp2(bt[:, C - 1:C]), (128, 128))
    r_hi = rdec.astype(BF16)
    r_lo = (rdec - r_hi.astype(F32)).astype(BF16)
    r2 = jnp.concatenate([r_hi, r_lo], axis=0)

    lane = lax.broadcasted_iota(I32, (C, 128), 1)
    trow = lax.broadcasted_iota(I32, (C, 128), 0)
    causal = trow >= (lane & (C - 1))
    lane_seg = lane >> (C.bit_length() - 1)
    xlane_seg = lax.broadcasted_iota(I32, (C, hp * SSD_P), 1) >> (SSD_P.bit_length() - 1)

    def conv(cols):
        v = cbuf[0:C + 8, cols]
        acc = cb_ref[:, cols] + v[8:8 + C] * cw_ref[SSD_K - 1:SSD_K, cols]
        for k in range(SSD_K - 1):
            acc = acc + pltpu.roll(v, C + 3 - k, axis=0)[0:C] * cw_ref[k:k + 1, cols]
        return _silu(acc)

    for g in range(SSD_G):
        xc = slice(g * SSD_GW, (g + 1) * SSD_GW)
        xg = conv(xc)
        bg = conv(slice(SSD_DI + g * SSD_N, SSD_DI + (g + 1) * SSD_N)).astype(BF16)
        cg = conv(slice(SSD_DI + SSD_G * SSD_N + g * SSD_N, SSD_DI + SSD_G * SSD_N + (g + 1) * SSD_N)).astype(BF16)
        ex = jnp.dot(lhs2, e2_ref[:, xc], preferred_element_type=F32)
        dt_e, eb_e, dl_e = ex[0:C], ex[C:2 * C], ex[2 * C:3 * C]
        xdt = xg * dt_e
        cb2 = lax.dot_general(cg, jnp.concatenate([bg] * hp, axis=0), NT_DIMS, preferred_element_type=F32)
        hg = hst[g * SSD_GW:(g + 1) * SSD_GW, :]
        y_inter = lax.dot_general(cg, hg.astype(BF16), NT_DIMS, preferred_element_type=F32)
        ys = []
        for p in range(8 // hp):
            h0 = g * 8 + p * hp
            bcol = jnp.broadcast_to(b[:, h0:h0 + 1], (C, 128))
            for k in range(1, hp):
                bcol = jnp.where(lane_seg >= k, b[:, h0 + k:h0 + k + 1], bcol)
            lm = jnp.exp2(jnp.where(causal, bcol - bt[h0:h0 + 1, :], neg))
            m2 = (cb2 * lm).astype(BF16)
            xp = xdt[:, p * hp * SSD_P:(p + 1) * hp * SSD_P]
            rhs = jnp.concatenate([jnp.where(xlane_seg == k, xp, 0.0) for k in range(hp)], axis=0).astype(BF16)
            ys.append(jnp.dot(m2, rhs, preferred_element_type=F32))
        y = jnp.concatenate(ys, axis=1) + y_inter * eb_e + xg * dsk_ref[:, xc]
        y = y * _silu(z_ref[:, xc])
        ms = jnp.mean(y * y, axis=-1, keepdims=True)
        y_ref[:, xc] = (y * lax.rsqrt(ms + EPS) * nw_ref[:, xc]).astype(y_ref.dtype)
        upd = lax.dot_general((xg * dl_e).astype(BF16), bg, TN_DIMS, preferred_element_type=F32)
        dec = jnp.dot(et2_ref[g * SSD_GW:(g + 1) * SSD_GW, :], r2, preferred_element_type=F32)
        hst[g * SSD_GW:(g + 1) * SSD_GW, :] = hg * dec + upd

    cbuf[5:8, :] = cbuf[C + 5:C + 8, :]

    @pl.when(c == nc - 1)
    def _():
        hn_ref[...] = hst[...]


def _ssd_scan(z, xbc, dtr, conv_state, h0, consts, *, nb, seq, C, row0):
    nc = seq // C
    rb0 = row0 // C
    cw, cb, dtb, alog, dsk, nw, e2, et2 = consts
    rows = lambda b, c: (rb0 + b * nc + c, 0)
    fixed = lambda b, c: (0, 0)
    per_b = lambda b, c: (b, 0, 0)
    return pl.pallas_call(
        functools.partial(_ssd_kernel, C=C, nc=nc),
        grid=(nb, nc),
        in_specs=[
            pl.BlockSpec((C, SSD_DI), rows),
            pl.BlockSpec((C, SSD_XBC), rows),
            pl.BlockSpec((C, 128), rows),
            pl.BlockSpec((None, SSD_K - 1, SSD_XBC), per_b),
            pl.BlockSpec((None, SSD_DI, SSD_N), per_b),
            pl.BlockSpec((SSD_K, SSD_XBC), fixed),
            pl.BlockSpec((1, SSD_XBC), fixed),
            pl.BlockSpec((1, 128), fixed),
            pl.BlockSpec((1, 128), fixed),
            pl.BlockSpec((1, SSD_DI), fixed),
            pl.BlockSpec((1, SSD_DI), fixed),
            pl.BlockSpec((256, SSD_DI), fixed),
            pl.BlockSpec((SSD_DI, 256), fixed),
        ],
        out_specs=[
            pl.BlockSpec((C, SSD_DI), lambda b, c: (b * nc + c, 0)),
            pl.BlockSpec((None, SSD_K - 1, SSD_XBC), per_b),
            pl.BlockSpec((None, SSD_DI, SSD_N), per_b),
        ],
        out_shape=[
            jax.ShapeDtypeStruct((nb * seq, SSD_DI), BF16),
            jax.ShapeDtypeStruct((nb, SSD_K - 1, SSD_XBC), F32),
            jax.ShapeDtypeStruct((nb, SSD_DI, SSD_N), F32),
        ],
        scratch_shapes=[pltpu.VMEM((8 + C, SSD_XBC), F32), pltpu.VMEM((SSD_DI, SSD_N), F32)],
        compiler_params=_cparams(2),
        name=f"ssd_scan_c{C}",
    )(z, xbc, dtr, conv_state, h0, cw, cb, dtb, alog, dsk, nw, e2, et2)


def _gla_kernel(q_ref, f_ref, v_ref, g_ref, s0_ref, lb_ref, nw_ref, o_ref, sn_ref, st, b_s, c_s, *, T, nc):
    c = pl.program_id(1)
    neg = jnp.float32(-jnp.inf)

    @pl.when(c == 0)
    def _():
        for h in range(HG_H):
            st[h * 128:(h + 1) * 128, :] = s0_ref[h * 128:(h + 1) * 128, :].T

    row = lax.broadcasted_iota(I32, (T, T), 0)
    col = lax.broadcasted_iota(I32, (T, T), 1)
    sh = HG_BLK.bit_length() - 1
    same_blk = (row >> sh) == (col >> sh)
    bd = jnp.where(jnp.logical_and(same_blk, row >= col), 1.0, 0.0).astype(BF16)

    for h in range(HG_H):
        hl = slice(h * 128, (h + 1) * 128)
        fz = f_ref[:, hl]
        lb = lb_ref[:, hl]
        ls = jnp.minimum(fz, 0.0) - _log1p_unit(jnp.exp(-jnp.abs(fz)))
        a1 = jnp.log(lb)
        a2 = jnp.log1p(-lb) + ls
        lf = jnp.maximum(a1, a2) + _log1p_unit(jnp.exp(-jnp.abs(a1 - a2)))
        b2 = _dot_exact_rhs(bd, lf * LOG2E)
        b_s[:, hl] = b2
        c_s[:, hl] = b2 - (a2 - fz) * LOG2E

    trow = lax.broadcasted_iota(I32, (8, 128), 0)
    ones = jnp.ones((128, 128), BF16)
    hb = HG_BLK // 2

    def blk(j, carry):
        rows = pl.ds(pl.multiple_of(j * HG_BLK, HG_BLK), HG_BLK)
        for h in range(HG_H):
            hl = slice(h * 128, (h + 1) * 128)
            qb = q_ref[rows, hl]
            bb = b_s[rows, hl]
            cb = c_s[rows, hl]
            vb = v_ref[rows, hl]
            sth = st[hl, :]
            o = lax.dot_general((qb * jnp.exp2(bb)).astype(BF16), sth.astype(BF16), NT_DIMS,
                                preferred_element_type=F32)
            q_h, b_h = (qb[0:hb], qb[hb:]), (bb[0:hb], bb[hb:])
            pieces, owner = [], []
            for s in range(HG_BLK):
                cs = cb[s:s + 1, :]
                for half in range(2):
                    if s >= hb * (half + 1):
                        continue
                    e = b_h[half] - cs
                    if s >= hb * half:
                        e = jnp.where(trow >= s - hb * half, e, neg)
                    pieces.append(q_h[half] * jnp.exp2(e))
                    owner.append((s, half))
            att = jnp.dot(jnp.concatenate(pieces, axis=0).astype(BF16), ones, preferred_element_type=F32)
            o_h = [o[0:hb], o[hb:]]
            for n, (s, half) in enumerate(owner):
                o_h[half] = o_h[half] + att[n * hb:(n + 1) * hb] * vb[s:s + 1, :]
            o = jnp.concatenate(o_h, axis=0)
            b_end = bb[HG_BLK - 1:HG_BLK, :]
            khat = jnp.exp2(b_end - cb).astype(BF16)
            upd = lax.dot_general(vb.astype(BF16), khat, TN_DIMS, preferred_element_type=F32)
            st[hl, :] = sth * jnp.exp2(b_end) + upd
            ms = jnp.mean(o * o, axis=-1, keepdims=True)
            o_ref[rows, hl] = (o * lax.rsqrt(ms + EPS) * nw_ref[...] * _silu(g_ref[rows, hl])).astype(o_ref.dtype)
        return carry

    lax.fori_loop(0, T // HG_BLK, blk, 0)

    @pl.when(c == nc - 1)
    def _():
        for h in range(HG_H):
            sn_ref[h * 128:(h + 1) * 128, :] = st[h * 128:(h + 1) * 128, :].T


def _gla_scan(qfig, s0, lb, nw, *, nb, seq, T, row0):
    nc = seq // T
    rb0 = row0 // T
    per_b = lambda b, c: (b, 0, 0)

    def cols(k):
        return pl.BlockSpec((T, D), lambda b, c: (rb0 + b * nc + c, k))

    return pl.pallas_call(
        functools.partial(_gla_kernel, T=T, nc=nc),
        grid=(nb, nc),
        in_specs=[
            cols(0), cols(1), cols(2), cols(3),
            pl.BlockSpec((None, D, HG_DV), per_b),
            pl.BlockSpec((1, D), lambda b, c: (0, 0)),
            pl.BlockSpec((1, HG_DV), lambda b, c: (0, 0)),
        ],
        out_specs=[
            pl.BlockSpec((T, D), lambda b, c: (b * nc + c, 0)),
            pl.BlockSpec((None, D, HG_DV), per_b),
        ],
        out_shape=[
            jax.ShapeDtypeStruct((nb * seq, D), BF16),
            jax.ShapeDtypeStruct((nb, D, HG_DV), F32),
        ],
        scratch_shapes=[pltpu.VMEM((D, HG_DK), F32), pltpu.VMEM((T, D), F32), pltpu.VMEM((T, D), F32)],
        compiler_params=_cparams(2),
        name=f"gla_scan_t{T}",
    )(qfig, qfig, qfig, qfig, s0, lb, nw)


def _route_plan(top_i):
    e_flat = top_i.reshape(-1)
    onehot = (e_flat[:, None] == jnp.arange(NE, dtype=I32)[None, :]).astype(I32)
    csum = jnp.cumsum(onehot, axis=0)
    rank = jnp.sum(onehot * (csum - 1), axis=1)
    cnt = csum[-1]
    ntile = (cnt + MOE_SORT - 1) // MOE_SORT
    cum_t = jnp.cumsum(ntile)
    tile0 = cum_t - ntile
    pos = jnp.sum(onehot * tile0[None, :], axis=1) * MOE_SORT + rank
    n_used = cum_t[-1]
    t_ids = jnp.arange(MOE_TILES, dtype=I32)
    te = jnp.sum((t_ids[:, None] >= cum_t[None, :]).astype(I32), axis=1)
    te_last = jnp.sum((n_used - 1 >= cum_t).astype(I32))
    te = jnp.where(t_ids < n_used, te, te_last).astype(I32)
    prev = jnp.concatenate([jnp.full((1,), -1, I32), te[:-1]])
    first = jnp.logical_and(t_ids < n_used, te != prev).astype(I32)
    e_ids = jnp.arange(NE, dtype=I32)
    later = jnp.logical_and(e_ids[None, :] > e_ids[:, None], ntile[None, :] > 0)
    nxt = jnp.min(jnp.where(later, e_ids[None, :], NE), axis=1)
    nxt = jnp.where(nxt == NE, -1, nxt).astype(I32)
    misc = jnp.stack([n_used, te[0]]).astype(I32)
    tok = jnp.arange(2 * M, dtype=I32) // 2
    src = jnp.zeros((MOE_ROWS,), I32).at[pos].set(tok)
    n_steps = ((n_used + 1) // 2).reshape(1).astype(I32)
    return (te, first, nxt, misc), n_steps, src, pos.reshape(M, 2)


def kernel(x_prompt, x_sample, c_prompt, c_sample, state_ssd_conv, state_ssd, state_hgrn, ada_w, ada_b, norm_w,
           ssd_w_in, ssd_conv_w, ssd_conv_b, ssd_dt_bias, ssd_a_log, ssd_d, ssd_norm_w, ssd_w_out, hgrn_w_in,
           hgrn_lb_logits, hgrn_norm_w, hgrn_w_out, ffn_w_gu, ffn_w_down, moe_router, moe_w_gu, moe_w_down,
           final_norm_w):
    x = (x_prompt.reshape(MP, D), x_sample.reshape(MS, D))
    c_all = jnp.concatenate([c_prompt, jnp.zeros((MOD_S0 - NP, D), F32), c_sample], axis=0)
    norm_w4 = norm_w.reshape(4, 1, D)

    mods = [
        _mm_plain(c_all, ada_w, layer=l, col0=0, n_out=6 * D, tn=1024, tm=MOD_ROWS,
                  bias=ada_b[l].reshape(1, 6 * D), act_lhs=True, name="adaln")
        for l in range(2)
    ]

    hn = _prenorm(x, norm_w4, 0, mods[0], 0, 1, BF16)
    w_in_t = jnp.swapaxes(ssd_w_in, 1, 2)
    z = _mm_plain(hn, w_in_t, layer=0, col0=0, n_out=SSD_DI, tn=1024, tm=1024, w_t=True, name="ssd_in_z")
    xbc = _mm_plain(hn, w_in_t, layer=0, col0=SSD_DI, n_out=SSD_XBC, tn=1024, tm=1024, w_t=True,
                    name="ssd_in_xbc")
    dtr = _mm_plain(hn, w_in_t, layer=0, col0=SSD_DI + SSD_XBC, n_out=128, tn=128, tm=1024, w_t=True,
                    name="ssd_in_dt")

    pad64 = lambda v: jnp.pad(v.reshape(1, SSD_HEADS), ((0, 0), (0, 128 - SSD_HEADS)))
    head_of = jnp.arange(SSD_DI, dtype=I32) // SSD_P
    e1 = (jnp.arange(128, dtype=I32)[:, None] == head_of[None, :]).astype(BF16)
    e2 = jnp.concatenate([e1, e1], axis=0)
    et2 = jnp.concatenate([e1.T, e1.T], axis=1)
    consts = (ssd_conv_w[0], ssd_conv_b[0].reshape(1, SSD_XBC), pad64(ssd_dt_bias[0]), pad64(ssd_a_log[0]),
              jnp.repeat(ssd_d[0], SSD_P).reshape(1, SSD_DI), ssd_norm_w[0].reshape(1, SSD_DI), e2, et2)
    yp, p_conv, p_ssd = _ssd_scan(z, xbc, dtr, jnp.zeros((NP, SSD_K - 1, SSD_XBC), F32),
                                  jnp.zeros((NP, SSD_DI, SSD_N), F32), consts, nb=NP, seq=LP, C=64, row0=0)
    ysm, s_conv, s_ssd = _ssd_scan(z, xbc, dtr, state_ssd_conv[0], state_ssd[0].reshape(NS, SSD_DI, SSD_N), consts,
                                   nb=NS, seq=LS, C=32, row0=MP)
    x = _mm_resid((yp, ysm), ssd_w_out, x, mods[0], 2, k=SSD_DI, tn=1024, tm=512, name="ssd_out")

    hn = _prenorm(x, norm_w4, 1, mods[0], 3, 4, BF16)
    act = _mm_swiglu(hn, ffn_w_gu, FFN, tn=512, tm=1024, name="ffn_gu")
    x = _mm_resid(act, ffn_w_down, x, mods[0], 5, k=FFN, tn=1024, tm=512, name="ffn_down")

    hn = _prenorm(x, norm_w4, 2, mods[1], 0, 1, BF16)
    qfig = _mm_plain(hn, hgrn_w_in, layer=0, col0=0, n_out=4 * D, tn=1024, tm=1024, name="hgrn_in")
    p = jax.nn.softmax(hgrn_lb_logits.astype(F32), axis=0)
    lb = (jnp.cumsum(p, axis=0) - p[0])[1].reshape(1, D)
    nw_h = hgrn_norm_w[0].reshape(1, HG_DV)
    op, p_hgrn = _gla_scan(qfig, jnp.zeros((NP, D, HG_DV), F32), lb, nw_h, nb=NP, seq=LP, T=128, row0=0)
    osm, s_hgrn = _gla_scan(qfig, state_hgrn[0].reshape(NS, D, HG_DV), lb, nw_h, nb=NS, seq=LS, T=32, row0=MP)
    x = _mm_resid((op, osm), hgrn_w_out, x, mods[1], 2, k=D, tn=1024, tm=512, name="hgrn_out")

    hn32, gv, gi = _prenorm(x, norm_w4, 3, mods[1], 3, 4, F32,
                            router_pad=jnp.pad(moe_router[0], ((0, 0), (0, 128 - NE))))
    plan, n_steps, src, pos = _route_plan(gi[:, :2])
    xs = _gather_rows(n_steps, src.reshape(MOE_STEPS, 1, 2 * MOE_SORT), hn32)
    act = _moe_gu(plan, xs, moe_w_gu[0])
    ys = _moe_down(plan, act, moe_w_down[0])
    tmc = 256
    pos3 = jnp.concatenate([pos[:, 0].reshape(M // tmc, 1, tmc), pos[:, 1].reshape(M // tmc, 1, tmc)], axis=2)
    yo_p, yo_s = _combine(pos3, ys, x, gv, mods[1], 5, final_norm_w.reshape(1, D), tm=tmc)

    return (
        yo_p.reshape(NP, LP, D),
        yo_s.reshape(NS, LS, D),
        p_conv[None],
        p_ssd.reshape(1, NP, SSD_HEADS, SSD_P, SSD_N),
        p_hgrn.reshape(1, NP, HG_H, HG_DK, HG_DV),
        s_conv[None],
        s_ssd.reshape(1, NS, SSD_HEADS, SSD_P, SSD_N),
        s_hgrn.reshape(1, NS, HG_H, HG_DK, HG_DV),
    )
```
